```python
import jax, jax.numpy as jnp
from jax import lax
import numpy as np

D_MODEL = 1024
BATCH = 8
SEQ = 4096
DEPTH = 1

CHUNK = 64
RWKV_HEADS = 8
RWKV_HEAD_DIM = 64
RWKV_WIDTH = RWKV_HEADS * RWKV_HEAD_DIM
DECAY_LORA = 64
ICLR_LORA = 64
GATE_LORA = 128
GDN_HEADS = 4
GDN_HEAD_DIM = 128
GDN_WIDTH = GDN_HEADS * GDN_HEAD_DIM
GDN_CONV = 4
FFN_HIDDEN = 2816
FFN_CONV = 3
NORM_EPS = 1e-6
L2_EPS = 1e-6
RWKV_GN_EPS = 64e-5

RWKV_SHIFT_WIDTH = 3 * RWKV_WIDTH + DECAY_LORA + ICLR_LORA + GATE_LORA
IN_SPLITS = (RWKV_SHIFT_WIDTH, 3 * GDN_WIDTH, GDN_WIDTH, GDN_HEADS, GDN_HEADS, D_MODEL, D_MODEL)
IN_WIDTH = RWKV_SHIFT_WIDTH + 4 * GDN_WIDTH + 2 * GDN_HEADS + 2 * D_MODEL

kernel_name = 'hybrid_rwkv7_gdn_gated_merge_block'


def _split(t, sizes):
    cuts = [int(c) for c in np.cumsum(sizes)[:-1]]
    return jnp.split(t, cuts, axis=-1)


def rms_norm(t, gain, eps=NORM_EPS):
    tf = t.astype(jnp.float32)
    y = tf * lax.rsqrt(jnp.mean(tf * tf, axis=-1, keepdims=True) + eps)
    return (y * gain.astype(jnp.float32)).astype(t.dtype)


def l2norm(t):
    tf = t.astype(jnp.float32)
    return (tf * lax.rsqrt(jnp.sum(tf * tf, axis=-1, keepdims=True) + L2_EPS)).astype(t.dtype)


def causal_depthwise_conv(t, w):
    width = w.shape[0]
    T = t.shape[1]
    tp = jnp.pad(t, ((0, 0), (width - 1, 0), (0, 0)))
    out = tp[:, 0:T] * w[0]
    for i in range(1, width):
        out = out + tp[:, i:i + T] * w[i]
    return out


def token_shift(t):
    return jnp.pad(t, ((0, 0), (1, 0), (0, 0)))[:, :-1]


def group_norm_heads(y, w, b):
    yf = y.astype(jnp.float32)
    mean = jnp.mean(yf, axis=-1, keepdims=True)
    var = jnp.mean(jnp.square(yf - mean), axis=-1, keepdims=True)
    yn = (yf - mean) * lax.rsqrt(var + RWKV_GN_EPS)
    H, D = y.shape[-2], y.shape[-1]
    return (yn * w.reshape(H, D) + b.reshape(H, D)).astype(y.dtype)


def wkv7_scan(r, w, k, v, a, b):
    dtype = r.dtype
    B, T, H, D = r.shape
    xs = tuple(jnp.moveaxis(t.astype(jnp.float32), 1, 0) for t in (r, w, k, v, a, b))

    def step(S, inp):
        r_t, w_t, k_t, v_t, a_t, b_t = inp
        sa = jnp.einsum('bhvk,bhk->bhv', S, a_t)
        S = S * w_t[:, :, None, :] + sa[..., None] * b_t[:, :, None, :] + v_t[..., None] * k_t[:, :, None, :]
        y = jnp.einsum('bhvk,bhk->bhv', S, r_t)
        return S, y

    S0 = jnp.zeros((B, H, D, D), jnp.float32)
    _, y = lax.scan(step, S0, xs)
    return jnp.moveaxis(y, 0, 1).astype(dtype)


def rwkv7_mix(p, mu, w0, w2, a0, a2, g2, k_k, k_a, r_k, ln_w, ln_b):
    B, T, _ = p.shape
    p = p + (token_shift(p) - p) * mu
    r, k, v, wl, al, gl = _split(p, (RWKV_WIDTH, RWKV_WIDTH, RWKV_WIDTH, DECAY_LORA, ICLR_LORA, GATE_LORA))
    w_log = -jax.nn.softplus(-(w0 + jnp.tanh(wl) @ w2)) - 0.5
    a = jax.nn.sigmoid(a0 + al @ a2)
    g = jax.nn.sigmoid(gl) @ g2

    def heads(t):
        return t.reshape(B, T, RWKV_HEADS, RWKV_HEAD_DIM)

    kk = l2norm(heads(k * k_k))
    k = k * (1 + (a - 1) * k_a)
    r_h, k_h, v_h, a_h = heads(r), heads(k), heads(v), heads(a)
    decay = jnp.exp(-jnp.exp(heads(w_log).astype(jnp.float32)))
    y = wkv7_scan(r_h, decay, k_h, v_h, -kk, kk * a_h)
    y = group_norm_heads(y, ln_w, ln_b)
    y = y + jnp.sum(r_h * k_h * r_k, axis=-1, keepdims=True) * v_h
    return y.reshape(B, T, RWKV_WIDTH) * g


def chunk_gated_delta_rule(q, k, v, g, beta):
    dtype = v.dtype
    B, T, H, Dk = q.shape
    Dv = v.shape[-1]
    N = T // CHUNK

    def to_chunks(t):
        t = t.astype(jnp.float32).reshape((B, N, CHUNK, H) + t.shape[3:])
        return jnp.moveaxis(t, 3, 1)

    q = to_chunks(q) * (Dk ** -0.5)
    k, v, g, beta = to_chunks(k), to_chunks(v), to_chunks(g), to_chunks(beta)
    gc = jnp.cumsum(g, axis=-1)
    causal = jnp.tril(jnp.ones((CHUNK, CHUNK), bool))
    strict = jnp.tril(jnp.ones((CHUNK, CHUNK), bool), -1)
    diff = gc[..., :, None] - gc[..., None, :]
    decay = jnp.where(causal, jnp.exp(jnp.where(causal, diff, 0.0)), 0.0)
    k_beta = k * beta[..., None]
    v_beta = v * beta[..., None]
    Lmat = jnp.where(strict, jnp.einsum('bhncd,bhnsd->bhncs', k_beta, k) * decay, 0.0)
    eye = jnp.eye(CHUNK, dtype=jnp.float32)
    Tinv = lax.linalg.triangular_solve(Lmat + eye, jnp.broadcast_to(eye, Lmat.shape),
                                       left_side=True, lower=True, unit_diagonal=True)
    u = jnp.einsum('bhncs,bhnsd->bhncd', Tinv, v_beta)
    wk = jnp.einsum('bhncs,bhnsd->bhncd', Tinv, k_beta * jnp.exp(gc)[..., None])
    attn = jnp.where(causal, jnp.einsum('bhncd,bhnsd->bhncs', q, k) * decay, 0.0)
    q_dec = q * jnp.exp(gc)[..., None]
    g_last = gc[..., -1]
    k_dec = k * jnp.exp(g_last[..., None] - gc)[..., None]
    xs = (jnp.moveaxis(q_dec, 2, 0), jnp.moveaxis(wk, 2, 0), jnp.moveaxis(u, 2, 0),
          jnp.moveaxis(attn, 2, 0), jnp.moveaxis(k_dec, 2, 0), jnp.moveaxis(g_last, 2, 0))

    def step(S, inp):
        q_n, w_n, u_n, attn_n, k_n, gl_n = inp
        v_new = u_n - jnp.einsum('bhcd,bhde->bhce', w_n, S)
        o = jnp.einsum('bhcd,bhde->bhce', q_n, S) + jnp.einsum('bhcs,bhse->bhce', attn_n, v_new)
        S = S * jnp.exp(gl_n)[..., None, None] + jnp.einsum('bhcd,bhce->bhde', k_n, v_new)
        return S, o

    S0 = jnp.zeros((B, H, Dk, Dv), jnp.float32)
    _, o = lax.scan(step, S0, xs)
    o = jnp.transpose(o, (1, 0, 3, 2, 4)).reshape(B, T, H, Dv)
    return o.astype(dtype)


def gated_deltanet_mix(qkv, z, a_raw, b_raw, conv_w, a_log, dt_bias, norm_w):
    B, T, _ = qkv.shape
    qkv = jax.nn.silu(causal_depthwise_conv(qkv, conv_w))
    q, k, v = _split(qkv, (GDN_WIDTH, GDN_WIDTH, GDN_WIDTH))
    q = l2norm(q.reshape(B, T, GDN_HEADS, GDN_HEAD_DIM))
    k = l2norm(k.reshape(B, T, GDN_HEADS, GDN_HEAD_DIM))
    v = v.reshape(B, T, GDN_HEADS, GDN_HEAD_DIM)
    beta = jax.nn.sigmoid(b_raw)
    g = -jnp.exp(a_log.astype(jnp.float32)) * jax.nn.softplus(a_raw.astype(jnp.float32) + dt_bias.astype(jnp.float32))
    o = chunk_gated_delta_rule(q, k, v, g, beta)
    o = rms_norm(o, norm_w) * jax.nn.silu(z.reshape(B, T, GDN_HEADS, GDN_HEAD_DIM))
    return o.reshape(B, T, GDN_WIDTH)


def setup_inputs(seed: int = 0) -> dict:
    key = jax.random.key(seed)
    ks = jax.random.split(key, 32)
    L = DEPTH

    def nrm(k, shape, scale):
        return jax.random.normal(k, shape, jnp.float32) * scale

    dt = jnp.exp(jax.random.uniform(ks[17], (L, GDN_HEADS), minval=float(np.log(1e-3)), maxval=float(np.log(1e-1))))
    return {
        'x': nrm(ks[0], (BATCH, SEQ, D_MODEL), 1.0),
        'norm1_g': 1.0 + nrm(ks[1], (L, D_MODEL), 0.02),
        'w_in': nrm(ks[2], (L, D_MODEL, IN_WIDTH), D_MODEL ** -0.5),
        'rwkv_mu': jax.random.uniform(ks[3], (L, RWKV_SHIFT_WIDTH)),
        'rwkv_w0': jax.random.uniform(ks[4], (L, RWKV_WIDTH), minval=-6.5, maxval=-1.0),
        'rwkv_w2': nrm(ks[5], (L, DECAY_LORA, RWKV_WIDTH), 0.5 * DECAY_LORA ** -0.5),
        'rwkv_a0': nrm(ks[6], (L, RWKV_WIDTH), 0.1),
        'rwkv_a2': nrm(ks[7], (L, ICLR_LORA, RWKV_WIDTH), 0.5 * ICLR_LORA ** -0.5),
        'rwkv_g2': nrm(ks[8], (L, GATE_LORA, RWKV_WIDTH), GATE_LORA ** -0.5),
        'rwkv_k_k': 0.85 + nrm(ks[9], (L, RWKV_WIDTH), 0.05),
        'rwkv_k_a': 1.0 + nrm(ks[10], (L, RWKV_WIDTH), 0.05),
        'rwkv_r_k': nrm(ks[11], (L, RWKV_HEADS, RWKV_HEAD_DIM), 0.1),
        'rwkv_ln_w': 1.0 + nrm(ks[12], (L, RWKV_WIDTH), 0.02),
        'rwkv_ln_b': nrm(ks[13], (L, RWKV_WIDTH), 0.02),
        'rwkv_proj': nrm(ks[14], (L, RWKV_WIDTH, D_MODEL), RWKV_WIDTH ** -0.5),
        'gdn_conv_w': nrm(ks[15], (L, GDN_CONV, 3 * GDN_WIDTH), GDN_CONV ** -0.5),
        'gdn_a_log': jnp.log(jax.random.uniform(ks[16], (L, GDN_HEADS), minval=1.0, maxval=16.0)),
        'gdn_dt_bias': dt + jnp.log(-jnp.expm1(-dt)),
        'gdn_norm_w': 1.0 + nrm(ks[18], (L, GDN_HEAD_DIM), 0.02),
        'gdn_proj': nrm(ks[19], (L, GDN_WIDTH, D_MODEL), GDN_WIDTH ** -0.5),
        'w_out': nrm(ks[20], (L, D_MODEL, D_MODEL), D_MODEL ** -0.5),
        'norm2_g': 1.0 + nrm(ks[21], (L, D_MODEL), 0.02),
        'ffn_up': nrm(ks[22], (L, D_MODEL, 2 * FFN_HIDDEN), D_MODEL ** -0.5),
        'ffn_conv_w': nrm(ks[23], (L, FFN_CONV, 2 * FFN_HIDDEN), FFN_CONV ** -0.5),
        'ffn_down': nrm(ks[24], (L, FFN_HIDDEN, D_MODEL), FFN_HIDDEN ** -0.5),
        'final_g': 1.0 + nrm(ks[25], (D_MODEL,), 0.02),
    }


def reference(x, norm1_g, w_in, rwkv_mu, rwkv_w0, rwkv_w2, rwkv_a0, rwkv_a2, rwkv_g2, rwkv_k_k,
              rwkv_k_a, rwkv_r_k, rwkv_ln_w, rwkv_ln_b, rwkv_proj, gdn_conv_w, gdn_a_log, gdn_dt_bias,
              gdn_norm_w, gdn_proj, w_out, norm2_g, ffn_up, ffn_conv_w, ffn_down, final_g):
    for l in range(DEPTH):
        u = rms_norm(x, norm1_g[l])
        p = u @ w_in[l]
        p_rwkv, qkv, z, a_raw, b_raw, gate_a, gate_b = _split(p, IN_SPLITS)
        y_a = rwkv7_mix(p_rwkv, rwkv_mu[l], rwkv_w0[l], rwkv_w2[l], rwkv_a0[l], rwkv_a2[l], rwkv_g2[l],
                        rwkv_k_k[l], rwkv_k_a[l], rwkv_r_k[l], rwkv_ln_w[l], rwkv_ln_b[l]) @ rwkv_proj[l]
        y_b = gated_deltanet_mix(qkv, z, a_raw, b_raw, gdn_conv_w[l], gdn_a_log[l], gdn_dt_bias[l],
                                 gdn_norm_w[l]) @ gdn_proj[l]
        mixed = jax.nn.sigmoid(gate_a) * y_a + jax.nn.sigmoid(gate_b) * y_b
        x = x + mixed @ w_out[l]
        h = rms_norm(x, norm2_g[l]) @ ffn_up[l]
        h = causal_depthwise_conv(h, ffn_conv_w[l])
        h_gate, h_up = _split(h, (FFN_HIDDEN, FFN_HIDDEN))
        x = x + (jax.nn.silu(h_gate) * h_up) @ ffn_down[l]
    return rms_norm(x, final_g)
```

```python
import functools

import jax
import jax.numpy as jnp
from jax import lax
from jax.experimental import pallas as pl
from jax.experimental.pallas import tpu as pltpu

F32 = jnp.float32
BF16 = jnp.bfloat16

CHUNK = 64
RWKV_HEADS = 8
RWKV_HEAD_DIM = 64
RWKV_WIDTH = RWKV_HEADS * RWKV_HEAD_DIM
DECAY_LORA = 64
ICLR_LORA = 64
GATE_LORA = 128
GDN_HEADS = 4
GDN_HEAD_DIM = 128
GDN_WIDTH = GDN_HEADS * GDN_HEAD_DIM
GDN_CONV = 4
FFN_CONV = 3
NORM_EPS = 1e-6
L2_EPS = 1e-6
RWKV_GN_EPS = 64e-5
RWKV_SHIFT_WIDTH = 3 * RWKV_WIDTH + DECAY_LORA + ICLR_LORA + GATE_LORA
LANES = 128
SUBLANES = 8
AB_PAD = LANES

VMEM_LIMIT = 52 * 1024 * 1024


def _sigmoid(x):
    return 1.0 / (1.0 + jnp.exp(-x))


def _softplus(x):
    return jnp.maximum(x, 0.0) + jnp.log(1.0 + jnp.exp(-jnp.abs(x)))


def _mm(a, b):
    return jnp.dot(a.astype(BF16), b.astype(BF16), preferred_element_type=F32)


def _mmf(a, b):
    return jnp.dot(a, b, preferred_element_type=F32, precision=lax.Precision.HIGHEST)


def _nt(a, b):
    return lax.dot_general(a, b, (((1,), (1,)), ((), ())), preferred_element_type=F32,
                           precision=lax.Precision.HIGHEST)


def _tn(a, b):
    return lax.dot_general(a, b, (((0,), (0,)), ((), ())), preferred_element_type=F32,
                           precision=lax.Precision.HIGHEST)


def _unit_lower_inverse(neg_l, eye):
    m = neg_l
    t = eye + m
    steps = max(CHUNK.bit_length() - 2, 0)
    for _ in range(steps):
        m = _mmf(m, m)
        t = t + _mmf(t, m)
    return t


def _inproj_body(x_ref, g_ref, w_ref, *out_refs):
    x = x_ref[...]
    ms = jnp.mean(x * x, axis=-1, keepdims=True)
    u = (x * lax.rsqrt(ms + NORM_EPS) * g_ref[...]).astype(BF16)
    off = 0
    for ref in out_refs:
        width = ref.shape[-1]
        ref[...] = jnp.dot(u, w_ref[:, off:off + width], preferred_element_type=F32)
        off += width


def _inproj(x2, gain, w_packed, widths, tm):
    n, d = x2.shape
    total = w_packed.shape[1]
    return pl.pallas_call(
        _inproj_body,
        grid=(n // tm,),
        in_specs=[
            pl.BlockSpec((tm, d), lambda i: (i, 0)),
            pl.BlockSpec((1, d), lambda i: (0, 0)),
            pl.BlockSpec((d, total), lambda i: (0, 0), pipeline_mode=pl.Buffered(1)),
        ],
        out_specs=[pl.BlockSpec((tm, w), lambda i: (i, 0)) for w in widths],
        out_shape=[jax.ShapeDtypeStruct((n, w), F32) for w in widths],
        compiler_params=pltpu.CompilerParams(dimension_semantics=("arbitrary",), vmem_limit_bytes=VMEM_LIMIT),
        name="inproj",
    )(x2, gain, w_packed)


def _rwkv_body(p_ref, mu_ref, w0_ref, w2_ref, a0_ref, a2_ref, g2_ref, kk_ref, ka_ref, rk_ref, lnw_ref,
               lnb_ref, bd_ref, y_ref, prev_ref, s_ref, ybuf_ref):
    c = p_ref.shape[0]
    hd = RWKV_HEAD_DIM

    @pl.when(pl.program_id(1) == 0)
    def _():
        prev_ref[...] = jnp.zeros_like(prev_ref)
        s_ref[...] = jnp.zeros_like(s_ref)

    p = p_ref[...]
    row = lax.broadcasted_iota(jnp.int32, (c, 1), 0)
    shifted = jnp.where(row == 0, prev_ref[0:1, :], pltpu.roll(p, 1, 0))
    prev_ref[0:1, :] = p[c - 1:c, :]
    pm = p + (shifted - p) * mu_ref[...]

    w = RWKV_WIDTH
    r = pm[:, 0:w]
    k = pm[:, w:2 * w]
    v = pm[:, 2 * w:3 * w]
    o0 = 3 * w
    wl = pm[:, o0:o0 + DECAY_LORA]
    al = pm[:, o0 + DECAY_LORA:o0 + DECAY_LORA + ICLR_LORA]
    gl = pm[:, o0 + DECAY_LORA + ICLR_LORA:o0 + DECAY_LORA + ICLR_LORA + GATE_LORA]

    w_log = -_softplus(-(w0_ref[...] + _mm(jnp.tanh(wl), w2_ref[...]))) - 0.5
    lw = -jnp.exp(w_log)
    a = _sigmoid(a0_ref[...] + _mm(al, a2_ref[...]))
    g = _mm(_sigmoid(gl), g2_ref[...])

    bd = bd_ref[...]
    kx = k * kk_ref[...]
    kk = kx * lax.rsqrt(_mmf(kx * kx, bd) + L2_EPS)
    k2 = k * (1.0 + (a - 1.0) * ka_ref[...])

    ri = lax.broadcasted_iota(jnp.int32, (c, c), 0)
    ci = lax.broadcasted_iota(jnp.int32, (c, c), 1)
    strict = ri > ci
    incl = ri >= ci
    eye = (ri == ci).astype(F32)

    cum = _mmf(incl.astype(F32), lw)
    tot = cum[c - 1:c, :]
    e_pos = jnp.exp(cum)
    e_neg = jnp.exp(-cum)
    e_prev = jnp.exp(cum - lw)
    e_rest = jnp.exp(tot - cum)
    w_tot = jnp.exp(tot)

    a_t = -kk * e_prev
    b_vec = kk * a
    b_t = b_vec * e_neg
    k_t = k2 * e_neg
    r_t = r * e_pos
    b_h = b_vec * e_rest
    k_h = k2 * e_rest

    for h in range(RWKV_HEADS):
        sl = slice(h * hd, (h + 1) * hd)
        s = s_ref[h]
        at, bt, kt, rt, vh = a_t[:, sl], b_t[:, sl], k_t[:, sl], r_t[:, sl], v[:, sl]
        a_ab = jnp.where(strict, _nt(at, bt), 0.0)
        a_ak = jnp.where(strict, _nt(at, kt), 0.0)
        a_rb = jnp.where(incl, _nt(rt, bt), 0.0)
        a_rk = jnp.where(incl, _nt(rt, kt), 0.0)
        tinv = _unit_lower_inverse(a_ab, eye)
        u = _mmf(tinv, _nt(at, s) + _mmf(a_ak, vh))
        ybuf_ref[:, sl] = _nt(rt, s) + _mmf(a_rb, u) + _mmf(a_rk, vh)
        s_ref[h] = s * w_tot[:, sl] + _tn(u, b_h[:, sl]) + _tn(vh, k_h[:, sl])

    y = ybuf_ref[...]
    inv_hd = 1.0 / hd
    mean = _mmf(y, bd) * inv_hd
    yc = y - mean
    var = _mmf(yc * yc, bd) * inv_hd
    yn = yc * lax.rsqrt(var + RWKV_GN_EPS) * lnw_ref[...] + lnb_ref[...]
    bonus = _mmf(r * k2 * rk_ref[...], bd) * v
    y_ref[...] = (yn + bonus) * g


def _rwkv(p3, params, bd):
    b, t, wd = p3.shape
    c = CHUNK
    full = lambda arr: pl.BlockSpec(arr.shape, lambda i, j: (0,) * arr.ndim)
    return pl.pallas_call(
        _rwkv_body,
        grid=(b, t // c),
        in_specs=[pl.BlockSpec((None, c, wd), lambda i, j: (i, j, 0))] + [full(a) for a in params] + [full(bd)],
        out_specs=pl.BlockSpec((None, c, RWKV_WIDTH), lambda i, j: (i, j, 0)),
        out_shape=jax.ShapeDtypeStruct((b, t, RWKV_WIDTH), F32),
        scratch_shapes=[
            pltpu.VMEM((SUBLANES, wd), F32),
            pltpu.VMEM((RWKV_HEADS, RWKV_HEAD_DIM, RWKV_HEAD_DIM), F32),
            pltpu.VMEM((c, RWKV_WIDTH), F32),
        ],
        compiler_params=pltpu.CompilerParams(dimension_semantics=("arbitrary", "arbitrary"),
                                             vmem_limit_bytes=VMEM_LIMIT),
        name="rwkv",
    )(p3, *params, bd)


def _gdn_body(qkv_ref, z_ref, ab_ref, cw_ref, alog_ref, dtb_ref, nw_ref, y_ref, prev_ref, s_ref):
    c = qkv_ref.shape[0]
    hd = GDN_HEAD_DIM

    @pl.when(pl.program_id(1) == 0)
    def _():
        prev_ref[...] = jnp.zeros_like(prev_ref)
        s_ref[...] = jnp.zeros_like(s_ref)

    x = qkv_ref[...]
    xp = jnp.concatenate([prev_ref[...], x], axis=0)
    conv = x * cw_ref[GDN_CONV - 1:GDN_CONV, :]
    for i in range(GDN_CONV - 1):
        lo = SUBLANES - (GDN_CONV - 1) + i
        conv = conv + xp[lo:lo + c, :] * cw_ref[i:i + 1, :]
    prev_ref[...] = x[c - SUBLANES:c, :]
    qkv = conv * _sigmoid(conv)
    w = GDN_WIDTH

    ab = ab_ref[...]
    g_all = -jnp.exp(alog_ref[...]) * _softplus(ab + dtb_ref[...])
    beta_all = _sigmoid(ab)

    ri = lax.broadcasted_iota(jnp.int32, (c, c), 0)
    ci = lax.broadcasted_iota(jnp.int32, (c, c), 1)
    strict = ri > ci
    incl = ri >= ci
    eye = (ri == ci).astype(F32)

    gc_all = _mmf(incl.astype(F32), g_all)
    gc_t = gc_all.T

    z = z_ref[...]
    for h in range(GDN_HEADS):
        sl = slice(h * hd, (h + 1) * hd)
        gcol = gc_all[:, h:h + 1]
        grow = gc_t[h:h + 1, :]
        beta = beta_all[:, GDN_HEADS + h:GDN_HEADS + h + 1]
        decay = jnp.where(incl, jnp.exp(jnp.where(incl, gcol - grow, 0.0)), 0.0)
        qh = qkv[:, sl]
        kh = qkv[:, w + h * hd:w + (h + 1) * hd]
        vh = qkv[:, 2 * w + h * hd:2 * w + (h + 1) * hd]
        qh = qh * lax.rsqrt(jnp.sum(qh * qh, axis=-1, keepdims=True) + L2_EPS) * (hd ** -0.5)
        kh = kh * lax.rsqrt(jnp.sum(kh * kh, axis=-1, keepdims=True) + L2_EPS)
        kb = kh * beta
        vb = vh * beta
        lmat = jnp.where(strict, _nt(kb, kh) * decay, 0.0)
        tinv = _unit_lower_inverse(-lmat, eye)
        eg = jnp.exp(gcol)
        u = _mmf(tinv, vb)
        wk = _mmf(tinv, kb * eg)
        attn = _nt(qh, kh) * decay
        glast = gc_all[c - 1:c, h:h + 1]
        k_dec = kh * jnp.exp(glast - gcol)
        s = s_ref[h]
        v_new = u - _mmf(wk, s)
        o = _mmf(qh * eg, s) + _mmf(attn, v_new)
        s_ref[h] = s * jnp.exp(glast) + _tn(k_dec, v_new)
        on = o * lax.rsqrt(jnp.mean(o * o, axis=-1, keepdims=True) + NORM_EPS) * nw_ref[...]
        zh = z[:, sl]
        y_ref[:, sl] = on * (zh * _sigmoid(zh))


def _gdn(qkv3, z3, ab3, cw, alog, dtb, nw):
    b, t, wd = qkv3.shape
    c = CHUNK
    full = lambda arr: pl.BlockSpec(arr.shape, lambda i, j: (0,) * arr.ndim)
    return pl.pallas_call(
        _gdn_body,
        grid=(b, t // c),
        in_specs=[
            pl.BlockSpec((None, c, wd), lambda i, j: (i, j, 0)),
            pl.BlockSpec((None, c, GDN_WIDTH), lambda i, j: (i, j, 0)),
            pl.BlockSpec((None, c, AB_PAD), lambda i, j: (i, j, 0)),
            full(cw), full(alog), full(dtb), full(nw),
        ],
        out_specs=pl.BlockSpec((None, c, GDN_WIDTH), lambda i, j: (i, j, 0)),
        out_shape=jax.ShapeDtypeStruct((b, t, GDN_WIDTH), F32),
        scratch_shapes=[
            pltpu.VMEM((SUBLANES, wd), F32),
            pltpu.VMEM((GDN_HEADS, GDN_HEAD_DIM, GDN_HEAD_DIM), F32),
        ],
        compiler_params=pltpu.CompilerParams(dimension_semantics=("arbitrary", "arbitrary"),
                                             vmem_limit_bytes=VMEM_LIMIT),
        name="gdn",
    )(qkv3, z3, ab3, cw, alog, dtb, nw)


def _mix_body(x_ref, ya_ref, yb_ref, gate_ref, rp_ref, gp_ref, wo_ref, o_ref):
    d = x_ref.shape[-1]
    ya = jnp.dot(ya_ref[...].astype(BF16), rp_ref[...], preferred_element_type=F32)
    yb = jnp.dot(yb_ref[...].astype(BF16), gp_ref[...], preferred_element_type=F32)
    gate = gate_ref[...]
    mixed = _sigmoid(gate[:, :d]) * ya + _sigmoid(gate[:, d:]) * yb
    o_ref[...] = x_ref[...] + jnp.dot(mixed.astype(BF16), wo_ref[...], preferred_element_type=F32)


def _mix(x2, ya2, yb2, gate2, rp, gp, wo, tm):
    n, d = x2.shape
    full = lambda arr: pl.BlockSpec(arr.shape, lambda i: (0, 0))
    return pl.pallas_call(
        _mix_body,
        grid=(n // tm,),
        in_specs=[
            pl.BlockSpec((tm, d), lambda i: (i, 0)),
            pl.BlockSpec((tm, ya2.shape[1]), lambda i: (i, 0)),
            pl.BlockSpec((tm, yb2.shape[1]), lambda i: (i, 0)),
            pl.BlockSpec((tm, 2 * d), lambda i: (i, 0)),
            full(rp), full(gp), full(wo),
        ],
        out_specs=pl.BlockSpec((tm, d), lambda i: (i, 0)),
        out_shape=jax.ShapeDtypeStruct((n, d), F32),
        compiler_params=pltpu.CompilerParams(dimension_semantics=("arbitrary",), vmem_limit_bytes=VMEM_LIMIT),
        name="mix",
    )(x2, ya2, yb2, gate2, rp, gp, wo)


def _ffn_body(x_ref, g_ref, up_ref, cw_ref, down_ref, fg_ref, o_ref, prev_ref, *, col_block, apply_final):
    tm = x_ref.shape[0]
    hidden = down_ref.shape[0]

    @pl.when(pl.program_id(1) == 0)
    def _():
        prev_ref[...] = jnp.zeros_like(prev_ref)

    x = x_ref[...]
    ms = jnp.mean(x * x, axis=-1, keepdims=True)
    xn = (x * lax.rsqrt(ms + NORM_EPS) * g_ref[...]).astype(BF16)

    def conv_cols(lo):
        h = jnp.dot(xn, up_ref[:, lo:lo + col_block], preferred_element_type=F32)
        hp = jnp.concatenate([prev_ref[:, lo:lo + col_block], h], axis=0)
        out = h * cw_ref[FFN_CONV - 1:FFN_CONV, lo:lo + col_block]
        for i in range(FFN_CONV - 1):
            s = SUBLANES - (FFN_CONV - 1) + i
            out = out + hp[s:s + tm, :] * cw_ref[i:i + 1, lo:lo + col_block]
        prev_ref[:, lo:lo + col_block] = h[tm - SUBLANES:tm, :]
        return out

    acc = x
    for j in range(hidden // col_block):
        lo = j * col_block
        hg = conv_cols(lo)
        hu = conv_cols(hidden + lo)
        act = (hg * _sigmoid(hg) * hu).astype(BF16)
        acc = acc + jnp.dot(act, down_ref[lo:lo + col_block, :], preferred_element_type=F32)
    if apply_final:
        ms2 = jnp.mean(acc * acc, axis=-1, keepdims=True)
        acc = acc * lax.rsqrt(ms2 + NORM_EPS) * fg_ref[...]
    o_ref[...] = acc


def _ffn(x3, gain, up, cw, down, final_g, tm, apply_final):
    b, t, d = x3.shape
    hidden = down.shape[0]
    full = lambda arr: pl.BlockSpec(arr.shape, lambda i, j: (0, 0))
    single = lambda arr: pl.BlockSpec(arr.shape, lambda i, j: (0, 0), pipeline_mode=pl.Buffered(1))
    body = functools.partial(_ffn_body, col_block=2 * LANES, apply_final=apply_final)
    return pl.pallas_call(
        body,
        grid=(b, t // tm),
        in_specs=[
            pl.BlockSpec((None, tm, d), lambda i, j: (i, j, 0)),
            full(gain), single(up), full(cw), single(down), full(final_g),
        ],
        out_specs=pl.BlockSpec((None, tm, d), lambda i, j: (i, j, 0)),
        out_shape=jax.ShapeDtypeStruct((b, t, d), F32),
        scratch_shapes=[pltpu.VMEM((SUBLANES, 2 * hidden), F32)],
        compiler_params=pltpu.CompilerParams(dimension_semantics=("arbitrary", "arbitrary"),
                                             vmem_limit_bytes=VMEM_LIMIT),
        name="ffn",
    )(x3, gain, up, cw, down, final_g)


def _row(vec):
    return vec.reshape(1, -1).astype(F32)


def _pad_lanes(vec, offset=0):
    out = jnp.zeros((1, LANES), F32)
    return lax.dynamic_update_slice(out, vec.reshape(1, -1).astype(F32), (0, offset))


def kernel(x, norm1_g, w_in, rwkv_mu, rwkv_w0, rwkv_w2, rwkv_a0, rwkv_a2, rwkv_g2, rwkv_k_k, rwkv_k_a,
           rwkv_r_k, rwkv_ln_w, rwkv_ln_b, rwkv_proj, gdn_conv_w, gdn_a_log, gdn_dt_bias, gdn_norm_w,
           gdn_proj, w_out, norm2_g, ffn_up, ffn_conv_w, ffn_down, final_g):
    b, t, d = x.shape
    n = b * t
    depth = norm1_g.shape[0]
    tm = 256

    hidx = jnp.arange(RWKV_WIDTH) // RWKV_HEAD_DIM
    bd = (hidx[:, None] == hidx[None, :]).astype(F32)

    n_ab = 2 * GDN_HEADS
    c_ab = RWKV_SHIFT_WIDTH + 4 * GDN_WIDTH
    widths = (RWKV_SHIFT_WIDTH, 3 * GDN_WIDTH, GDN_WIDTH, AB_PAD, 2 * d)

    for l in range(depth):
        wl = w_in[l]
        w_packed = jnp.concatenate(
            [wl[:, :c_ab + n_ab], jnp.zeros((d, AB_PAD - n_ab), wl.dtype), wl[:, c_ab + n_ab:]], axis=1).astype(BF16)
        p_rwkv, qkv, z, ab, gate = _inproj(x.reshape(n, d), _row(norm1_g[l]), w_packed, widths, tm)

        rwkv_params = (_row(rwkv_mu[l]), _row(rwkv_w0[l]), rwkv_w2[l].astype(BF16), _row(rwkv_a0[l]),
                       rwkv_a2[l].astype(BF16), rwkv_g2[l].astype(BF16), _row(rwkv_k_k[l]), _row(rwkv_k_a[l]),
                       _row(rwkv_r_k[l]), _row(rwkv_ln_w[l]), _row(rwkv_ln_b[l]))
        y_a = _rwkv(p_rwkv.reshape(b, t, -1), rwkv_params, bd)

        y_b = _gdn(qkv.reshape(b, t, -1), z.reshape(b, t, -1), ab.reshape(b, t, -1), gdn_conv_w[l].astype(F32),
                   _pad_lanes(gdn_a_log[l]), _pad_lanes(gdn_dt_bias[l]), _row(gdn_norm_w[l]))

        x1 = _mix(x.reshape(n, d), y_a.reshape(n, -1), y_b.reshape(n, -1), gate, rwkv_proj[l].astype(BF16),
                  gdn_proj[l].astype(BF16), w_out[l].astype(BF16), tm)

        x = _ffn(x1.reshape(b, t, d), _row(norm2_g[l]), ffn_up[l].astype(BF16), ffn_conv_w[l].astype(F32),
                 ffn_down[l].astype(BF16), _row(final_g), tm, apply_final=(l == depth - 1))
    return x
```

```python
import functools

import jax
import jax.numpy as jnp
from jax import lax
from jax.experimental import pallas as pl
from jax.experimental.pallas import tpu as pltpu

F32 = jnp.float32
BF16 = jnp.bfloat16

CHUNK = 64
RWKV_HEADS = 8
RWKV_HEAD_DIM = 64
RWKV_WIDTH = RWKV_HEADS * RWKV_HEAD_DIM
DECAY_LORA = 64
ICLR_LORA = 64
GATE_LORA = 128
GDN_HEADS = 4
GDN_HEAD_DIM = 128
GDN_WIDTH = GDN_HEADS * GDN_HEAD_DIM
GDN_CONV = 4
FFN_CONV = 3
NORM_EPS = 1e-6
L2_EPS = 1e-6
RWKV_GN_EPS = 64e-5
RWKV_SHIFT_WIDTH = 3 * RWKV_WIDTH + DECAY_LORA + ICLR_LORA + GATE_LORA
LANES = 128
SUBLANES = 8
MXU_TILE = 256
AB_PAD = LANES
GROUP = MXU_TILE // CHUNK
RWKV_GROUPS = RWKV_WIDTH // MXU_TILE

VMEM_LIMIT = 52 * 1024 * 1024


def _sigmoid(x):
    return 1.0 / (1.0 + jnp.exp(-x))


def _softplus(x):
    return jnp.maximum(x, 0.0) + jnp.log(1.0 + jnp.exp(-jnp.abs(x)))


def _dot(a, b):
    return jnp.dot(a.astype(BF16), b.astype(BF16), preferred_element_type=F32)


def _dot_nt(a, b):
    return lax.dot_general(a.astype(BF16), b.astype(BF16), (((1,), (1,)), ((), ())), preferred_element_type=F32)


def _dot_tn(a, b):
    return lax.dot_general(a.astype(BF16), b.astype(BF16), (((0,), (0,)), ((), ())), preferred_element_type=F32)


def _bdiag(x, mask16):
    reps = MXU_TILE // x.shape[0]
    return jnp.concatenate([x.astype(BF16)] * reps, axis=0) * mask16


def _cat_masks():
    ri = lax.broadcasted_iota(jnp.int32, (CHUNK, MXU_TILE), 0)
    ci = lax.broadcasted_iota(jnp.int32, (CHUNK, MXU_TILE), 1) % CHUNK
    return ri > ci, ri >= ci, (ri == ci).astype(F32)


def _unit_lower_inverse_cat(m, eye_cat, mask16):
    t = eye_cat + m
    mk = _dot(m, _bdiag(m, mask16))
    levels = CHUNK.bit_length() - 1
    for _ in range(levels - 2):
        both = _dot(jnp.concatenate([mk, t], axis=0), _bdiag(mk, mask16))
        mk = both[:CHUNK]
        t = t + both[CHUNK:]
    return t + _dot(t, _bdiag(mk, mask16))


def _inproj_body(x_ref, g_ref, w_ref, *out_refs):
    x = x_ref[...]
    ms = jnp.mean(x * x, axis=-1, keepdims=True)
    u = (x * lax.rsqrt(ms + NORM_EPS) * g_ref[...]).astype(BF16)
    off = 0
    for ref in out_refs:
        width = ref.shape[-1]
        ref[...] = jnp.dot(u, w_ref[:, off:off + width], preferred_element_type=F32)
        off += width


def _inproj(x2, gain, w_packed, widths, tm):
    n, d = x2.shape
    total = w_packed.shape[1]
    return pl.pallas_call(
        _inproj_body,
        grid=(n // tm,),
        in_specs=[
            pl.BlockSpec((tm, d), lambda i: (i, 0)),
            pl.BlockSpec((1, d), lambda i: (0, 0)),
            pl.BlockSpec((d, total), lambda i: (0, 0), pipeline_mode=pl.Buffered(1)),
        ],
        out_specs=[pl.BlockSpec((tm, w), lambda i: (i, 0)) for w in widths],
        out_shape=[jax.ShapeDtypeStruct((n, w), F32) for w in widths],
        compiler_params=pltpu.CompilerParams(dimension_semantics=("arbitrary",), vmem_limit_bytes=VMEM_LIMIT),
        name="inproj",
    )(x2, gain, w_packed)


def _rwkv_body(p_ref, mu_ref, w0_ref, w2_ref, a0_ref, a2_ref, g2_ref, kk_ref, ka_ref, rk_ref, lnw_ref,
               lnb_ref, m16_ref, mf_ref, y_ref, prev_ref, s_ref):
    c = p_ref.shape[0]
    hd = RWKV_HEAD_DIM

    @pl.when(pl.program_id(1) == 0)
    def _():
        prev_ref[...] = jnp.zeros_like(prev_ref)
        s_ref[...] = jnp.zeros_like(s_ref)

    p = p_ref[...]
    row = lax.broadcasted_iota(jnp.int32, (c, 1), 0)
    shifted = jnp.where(row == 0, prev_ref[0:1, :], pltpu.roll(p, 1, 0))
    prev_ref[0:1, :] = p[c - 1:c, :]
    pm = p + (shifted - p) * mu_ref[...]

    w = RWKV_WIDTH
    r = pm[:, 0:w]
    k = pm[:, w:2 * w]
    v = pm[:, 2 * w:3 * w]
    o0 = 3 * w
    wl = pm[:, o0:o0 + DECAY_LORA]
    al = pm[:, o0 + DECAY_LORA:o0 + DECAY_LORA + ICLR_LORA]
    gl = pm[:, o0 + DECAY_LORA + ICLR_LORA:o0 + DECAY_LORA + ICLR_LORA + GATE_LORA]

    w_log = -_softplus(-(w0_ref[...] + _dot(jnp.tanh(wl), w2_ref[...]))) - 0.5
    lw = (-jnp.exp(w_log)).astype(BF16).astype(F32)
    a = _sigmoid(a0_ref[...] + _dot(al, a2_ref[...]))
    g = _dot(_sigmoid(gl), g2_ref[...])

    mask16 = m16_ref[...]
    maskf = mf_ref[...]

    def head_sums(*xs):
        stacked = jnp.concatenate([x[:, q * MXU_TILE:(q + 1) * MXU_TILE] for x in xs for q in range(RWKV_GROUPS)], axis=0)
        res = jnp.dot(stacked.astype(BF16), mask16, preferred_element_type=F32)
        outs = []
        for i in range(len(xs)):
            base = i * RWKV_GROUPS * c
            outs.append(jnp.concatenate([res[base + q * c:base + (q + 1) * c] for q in range(RWKV_GROUPS)], axis=1))
        return outs

    kx = k * kk_ref[...]
    k2 = k * (1.0 + (a - 1.0) * ka_ref[...])
    kx_ss, rk_sum = head_sums(kx * kx, r * k2 * rk_ref[...])
    kk = kx * lax.rsqrt(kx_ss + L2_EPS)

    ri = lax.broadcasted_iota(jnp.int32, (c, c), 0)
    ci = lax.broadcasted_iota(jnp.int32, (c, c), 1)
    cum = jnp.dot((ri >= ci).astype(F32).astype(BF16), lw.astype(BF16), preferred_element_type=F32)
    tot = cum[c - 1:c, :]
    e_pos = jnp.exp(cum)
    e_neg = jnp.exp(-cum)
    e_prev = jnp.exp(cum - lw)
    e_rest = jnp.exp(tot - cum)
    w_tot = jnp.exp(tot)

    b_vec = kk * a
    l2 = jnp.concatenate([-kk * e_prev, r * e_pos], axis=0).astype(BF16)
    b_t = (b_vec * e_neg).astype(BF16)
    k_t = (k2 * e_neg).astype(BF16)
    bk = jnp.concatenate([b_vec * e_rest, k2 * e_rest], axis=0).astype(BF16)
    v16 = v.astype(BF16)

    strict_cat, incl_cat, eye_cat = _cat_masks()
    mask2 = jnp.concatenate([strict_cat, incl_cat], axis=0)

    ys = []
    for q in range(RWKV_GROUPS):
        gs = slice(q * MXU_TILE, (q + 1) * MXU_TILE)
        s_bd = s_ref[q]
        w3 = jnp.concatenate([_bdiag(b_t[:, gs], mask16), _bdiag(k_t[:, gs], mask16), s_bd.astype(BF16)], axis=0)
        pxs = _dot_nt(l2[:, gs], w3)
        pb = jnp.where(mask2, pxs[:, 0:MXU_TILE], 0.0)
        pk = jnp.where(mask2, pxs[:, MXU_TILE:2 * MXU_TILE], 0.0)
        xa = pxs[:, 2 * MXU_TILE:] + _dot(pk, _bdiag(v16[:, gs], mask16))
        tinv = _unit_lower_inverse_cat(pb[:c], eye_cat, mask16)
        u = _dot(tinv, _bdiag(xa[:c], mask16))
        ys.append(xa[c:] + _dot(pb[c:], _bdiag(u, mask16)))
        uv = jnp.concatenate([u, v[:, gs]], axis=0)
        s_ref[q] = s_bd * w_tot[:, gs] + _dot_tn(uv, bk[:, gs]) * maskf

    y = jnp.concatenate(ys, axis=1)
    inv_hd = 1.0 / hd
    (mean,) = head_sums(y)
    yc = y - mean * inv_hd
    (var,) = head_sums(yc * yc)
    yn = yc * lax.rsqrt(var * inv_hd + RWKV_GN_EPS) * lnw_ref[...] + lnb_ref[...]
    y_ref[...] = (yn + rk_sum * v) * g


def _rwkv(p3, params, mask16, maskf):
    b, t, wd = p3.shape
    c = CHUNK
    full = lambda arr: pl.BlockSpec(arr.shape, lambda i, j: (0,) * arr.ndim)
    return pl.pallas_call(
        _rwkv_body,
        grid=(b, t // c),
        in_specs=[pl.BlockSpec((None, c, wd), lambda i, j: (i, j, 0))] + [full(a) for a in params]
        + [full(mask16), full(maskf)],
        out_specs=pl.BlockSpec((None, c, RWKV_WIDTH), lambda i, j: (i, j, 0)),
        out_shape=jax.ShapeDtypeStruct((b, t, RWKV_WIDTH), F32),
        scratch_shapes=[
            pltpu.VMEM((SUBLANES, wd), F32),
            pltpu.VMEM((RWKV_GROUPS, MXU_TILE, MXU_TILE), F32),
        ],
        compiler_params=pltpu.CompilerParams(dimension_semantics=("arbitrary", "arbitrary"),
                                             vmem_limit_bytes=VMEM_LIMIT),
        name="rwkv",
    )(p3, *params, mask16, maskf)


def _gdn_body(qkv_ref, z_ref, ab_ref, cw_ref, alog_ref, dtb_ref, nw_ref, m16_ref, hm16_ref, y_ref, prev_ref, s_ref):
    c = qkv_ref.shape[0]
    hd = GDN_HEAD_DIM
    nh = GDN_HEADS

    @pl.when(pl.program_id(1) == 0)
    def _():
        prev_ref[...] = jnp.zeros_like(prev_ref)
        s_ref[...] = jnp.zeros_like(s_ref)

    x = qkv_ref[...]
    xp = jnp.concatenate([prev_ref[...], x], axis=0)
    conv = x * cw_ref[GDN_CONV - 1:GDN_CONV, :]
    for i in range(GDN_CONV - 1):
        lo = SUBLANES - (GDN_CONV - 1) + i
        conv = conv + xp[lo:lo + c, :] * cw_ref[i:i + 1, :]
    prev_ref[...] = x[c - SUBLANES:c, :]
    qkv = conv * _sigmoid(conv)
    w = GDN_WIDTH

    ab = ab_ref[...]
    g_all = -jnp.exp(alog_ref[...]) * _softplus(ab + dtb_ref[...])
    beta_all = _sigmoid(ab)

    ri = lax.broadcasted_iota(jnp.int32, (c, c), 0)
    ci = lax.broadcasted_iota(jnp.int32, (c, c), 1)
    tril16 = (ri >= ci).astype(F32).astype(BF16)
    g_hi = g_all.astype(BF16)
    rem = g_all - g_hi.astype(F32)
    g_mid = rem.astype(BF16)
    g_lo = (rem - g_mid.astype(F32)).astype(BF16)
    gcp = jnp.dot(tril16, jnp.concatenate([g_hi, g_mid, g_lo], axis=1), preferred_element_type=F32)
    gc_all = gcp[:, 0:LANES] + gcp[:, LANES:2 * LANES] + gcp[:, 2 * LANES:3 * LANES]
    eg_all = jnp.exp(gc_all)
    glast_all = gc_all[c - 1:c, :]
    erest_all = jnp.exp(glast_all - gc_all)
    elast_all = jnp.exp(glast_all)

    mask16 = m16_ref[...]
    strict_cat, incl_cat, eye_cat = _cat_masks()
    lane_head = lax.broadcasted_iota(jnp.int32, (c, MXU_TILE), 1) // CHUNK

    f_cat = jnp.zeros((c, MXU_TILE), F32)
    for h in range(nh):
        f_cat = jnp.where(lane_head == h, gc_all[:, h:h + 1], f_cat)
    r_cat = jnp.sum(f_cat * eye_cat, axis=0, keepdims=True)
    decay = jnp.where(incl_cat, jnp.exp(jnp.where(incl_cat, f_cat - r_cat, 0.0)), 0.0)

    qs, ks, kbs = [], [], []
    for h in range(nh):
        qh = qkv[:, h * hd:(h + 1) * hd]
        kh = qkv[:, w + h * hd:w + (h + 1) * hd]
        qs.append(qh * lax.rsqrt(jnp.sum(qh * qh, axis=-1, keepdims=True) + L2_EPS) * (hd ** -0.5))
        kh = kh * lax.rsqrt(jnp.sum(kh * kh, axis=-1, keepdims=True) + L2_EPS)
        ks.append(kh)
        kbs.append(kh * beta_all[:, nh + h:nh + h + 1])
    k_all = jnp.concatenate(ks, axis=1).astype(BF16)
    lhs = jnp.concatenate([jnp.concatenate(kbs, axis=1), jnp.concatenate(qs, axis=1)], axis=0)
    k_rows = jnp.concatenate([k_all] * nh, axis=0) * hm16_ref[...]
    kq = _dot_nt(lhs, k_rows)
    lmat = jnp.where(strict_cat, kq[:c] * decay, 0.0)
    attn = kq[c:] * decay
    tinv = _unit_lower_inverse_cat(-lmat, eye_cat, mask16)

    z = z_ref[...]
    for h in range(nh):
        sl = slice(h * hd, (h + 1) * hd)
        tl = slice(h * c, (h + 1) * c)
        vh = qkv[:, 2 * w + h * hd:2 * w + (h + 1) * hd]
        beta = beta_all[:, nh + h:nh + h + 1]
        eg = eg_all[:, h:h + 1]
        uw = _dot(tinv[:, tl], jnp.concatenate([vh * beta, kbs[h] * eg], axis=1))
        s = s_ref[h]
        wq = _dot(jnp.concatenate([uw[:, hd:], qs[h] * eg], axis=0), s)
        v_new = uw[:, :hd] - wq[:c]
        o = wq[c:] + _dot(attn[:, tl], v_new)
        s_ref[h] = s * elast_all[:, h:h + 1] + _dot_tn(ks[h] * erest_all[:, h:h + 1], v_new)
        on = o * lax.rsqrt(jnp.mean(o * o, axis=-1, keepdims=True) + NORM_EPS) * nw_ref[...]
        zh = z[:, sl]
        y_ref[:, sl] = on * (zh * _sigmoid(zh))


def _gdn(qkv3, z3, ab3, cw, alog, dtb, nw, mask16, hmask16):
    b, t, wd = qkv3.shape
    c = CHUNK
    full = lambda arr: pl.BlockSpec(arr.shape, lambda i, j: (0,) * arr.ndim)
    return pl.pallas_call(
        _gdn_body,
        grid=(b, t // c),
        in_specs=[
            pl.BlockSpec((None, c, wd), lambda i, j: (i, j, 0)),
            pl.BlockSpec((None, c, GDN_WIDTH), lambda i, j: (i, j, 0)),
            pl.BlockSpec((None, c, AB_PAD), lambda i, j: (i, j, 0)),
            full(cw), full(alog), full(dtb), full(nw), full(mask16), full(hmask16),
        ],
        out_specs=pl.BlockSpec((None, c, GDN_WIDTH), lambda i, j: (i, j, 0)),
        out_shape=jax.ShapeDtypeStruct((b, t, GDN_WIDTH), F32),
        scratch_shapes=[
            pltpu.VMEM((SUBLANES, wd), F32),
            pltpu.VMEM((GDN_HEADS, GDN_HEAD_DIM, GDN_HEAD_DIM), F32),
        ],
        compiler_params=pltpu.CompilerParams(dimension_semantics=("arbitrary", "arbitrary"),
                                             vmem_limit_bytes=VMEM_LIMIT),
        name="gdn",
    )(qkv3, z3, ab3, cw, alog, dtb, nw, mask16, hmask16)


def _mix_body(x_ref, ya_ref, yb_ref, gate_ref, rp_ref, gp_ref, wo_ref, o_ref):
    d = x_ref.shape[-1]
    ya = jnp.dot(ya_ref[...].astype(BF16), rp_ref[...], preferred_element_type=F32)
    yb = jnp.dot(yb_ref[...].astype(BF16), gp_ref[...], preferred_element_type=F32)
    gate = gate_ref[...]
    mixed = _sigmoid(gate[:, :d]) * ya + _sigmoid(gate[:, d:]) * yb
    o_ref[...] = x_ref[...] + jnp.dot(mixed.astype(BF16), wo_ref[...], preferred_element_type=F32)


def _mix(x2, ya2, yb2, gate2, rp, gp, wo, tm):
    n, d = x2.shape
    full = lambda arr: pl.BlockSpec(arr.shape, lambda i: (0, 0))
    return pl.pallas_call(
        _mix_body,
        grid=(n // tm,),
        in_specs=[
            pl.BlockSpec((tm, d), lambda i: (i, 0)),
            pl.BlockSpec((tm, ya2.shape[1]), lambda i: (i, 0)),
            pl.BlockSpec((tm, yb2.shape[1]), lambda i: (i, 0)),
            pl.BlockSpec((tm, 2 * d), lambda i: (i, 0)),
            full(rp), full(gp), full(wo),
        ],
        out_specs=pl.BlockSpec((tm, d), lambda i: (i, 0)),
        out_shape=jax.ShapeDtypeStruct((n, d), F32),
        compiler_params=pltpu.CompilerParams(dimension_semantics=("arbitrary",), vmem_limit_bytes=VMEM_LIMIT),
        name="mix",
    )(x2, ya2, yb2, gate2, rp, gp, wo)


def _ffn_body(x_ref, g_ref, up_ref, cw_ref, down_ref, fg_ref, o_ref, prev_ref, *, col_block, apply_final):
    tm = x_ref.shape[0]
    hidden = down_ref.shape[0]

    @pl.when(pl.program_id(1) == 0)
    def _():
        prev_ref[...] = jnp.zeros_like(prev_ref)

    x = x_ref[...]
    ms = jnp.mean(x * x, axis=-1, keepdims=True)
    xn = (x * lax.rsqrt(ms + NORM_EPS) * g_ref[...]).astype(BF16)

    def conv_cols(lo):
        h = jnp.dot(xn, up_ref[:, lo:lo + col_block], preferred_element_type=F32)
        hp = jnp.concatenate([prev_ref[:, lo:lo + col_block], h], axis=0)
        out = h * cw_ref[FFN_CONV - 1:FFN_CONV, lo:lo + col_block]
        for i in range(FFN_CONV - 1):
            s = SUBLANES - (FFN_CONV - 1) + i
            out = out + hp[s:s + tm, :] * cw_ref[i:i + 1, lo:lo + col_block]
        prev_ref[:, lo:lo + col_block] = h[tm - SUBLANES:tm, :]
        return out

    acc = x
    for j in range(hidden // col_block):
        lo = j * col_block
        hg = conv_cols(lo)
        hu = conv_cols(hidden + lo)
        act = (hg * _sigmoid(hg) * hu).astype(BF16)
        acc = acc + jnp.dot(act, down_ref[lo:lo + col_block, :], preferred_element_type=F32)
    if apply_final:
        ms2 = jnp.mean(acc * acc, axis=-1, keepdims=True)
        acc = acc * lax.rsqrt(ms2 + NORM_EPS) * fg_ref[...]
    o_ref[...] = acc


def _ffn(x3, gain, up, cw, down, final_g, tm, apply_final):
    b, t, d = x3.shape
    hidden = down.shape[0]
    full = lambda arr: pl.BlockSpec(arr.shape, lambda i, j: (0, 0))
    single = lambda arr: pl.BlockSpec(arr.shape, lambda i, j: (0, 0), pipeline_mode=pl.Buffered(1))
    body = functools.partial(_ffn_body, col_block=MXU_TILE, apply_final=apply_final)
    return pl.pallas_call(
        body,
        grid=(b, t // tm),
        in_specs=[
            pl.BlockSpec((None, tm, d), lambda i, j: (i, j, 0)),
            full(gain), single(up), full(cw), single(down), full(final_g),
        ],
        out_specs=pl.BlockSpec((None, tm, d), lambda i, j: (i, j, 0)),
        out_shape=jax.ShapeDtypeStruct((b, t, d), F32),
        scratch_shapes=[pltpu.VMEM((SUBLANES, 2 * hidden), F32)],
        compiler_params=pltpu.CompilerParams(dimension_semantics=("arbitrary", "arbitrary"),
                                             vmem_limit_bytes=VMEM_LIMIT),
        name="ffn",
    )(x3, gain, up, cw, down, final_g)


def _row(vec):
    return vec.reshape(1, -1).astype(F32)


def _pad_lanes(vec, offset=0):
    out = jnp.zeros((1, LANES), F32)
    return lax.dynamic_update_slice(out, vec.reshape(1, -1).astype(F32), (0, offset))


def kernel(x, norm1_g, w_in, rwkv_mu, rwkv_w0, rwkv_w2, rwkv_a0, rwkv_a2, rwkv_g2, rwkv_k_k, rwkv_k_a,
           rwkv_r_k, rwkv_ln_w, rwkv_ln_b, rwkv_proj, gdn_conv_w, gdn_a_log, gdn_dt_bias, gdn_norm_w,
           gdn_proj, w_out, norm2_g, ffn_up, ffn_conv_w, ffn_down, final_g):
    b, t, d = x.shape
    n = b * t
    depth = norm1_g.shape[0]
    tm = 256

    blk = jnp.arange(MXU_TILE) // CHUNK
    maskf = (blk[:, None] == blk[None, :]).astype(F32)
    mask16 = maskf.astype(BF16)
    hmask16 = (jnp.arange(GDN_HEADS * CHUNK)[:, None] // CHUNK == jnp.arange(GDN_WIDTH)[None, :] // GDN_HEAD_DIM).astype(BF16)

    n_ab = 2 * GDN_HEADS
    c_ab = RWKV_SHIFT_WIDTH + 4 * GDN_WIDTH
    widths = (RWKV_SHIFT_WIDTH, 3 * GDN_WIDTH, GDN_WIDTH, AB_PAD, 2 * d)

    for l in range(depth):
        wl = w_in[l]
        w_packed = jnp.concatenate(
            [wl[:, :c_ab + n_ab], jnp.zeros((d, AB_PAD - n_ab), wl.dtype), wl[:, c_ab + n_ab:]], axis=1).astype(BF16)
        p_rwkv, qkv, z, ab, gate = _inproj(x.reshape(n, d), _row(norm1_g[l]), w_packed, widths, tm)

        rwkv_params = (_row(rwkv_mu[l]), _row(rwkv_w0[l]), rwkv_w2[l].astype(BF16), _row(rwkv_a0[l]),
                       rwkv_a2[l].astype(BF16), rwkv_g2[l].astype(BF16), _row(rwkv_k_k[l]), _row(rwkv_k_a[l]),
                       _row(rwkv_r_k[l]), _row(rwkv_ln_w[l]), _row(rwkv_ln_b[l]))
        y_a = _rwkv(p_rwkv.reshape(b, t, -1), rwkv_params, mask16, maskf)

        y_b = _gdn(qkv.reshape(b, t, -1), z.reshape(b, t, -1), ab.reshape(b, t, -1), gdn_conv_w[l].astype(F32),
                   _pad_lanes(gdn_a_log[l]), _pad_lanes(gdn_dt_bias[l]), _row(gdn_norm_w[l]), mask16, hmask16)

        x1 = _mix(x.reshape(n, d), y_a.reshape(n, -1), y_b.reshape(n, -1), gate, rwkv_proj[l].astype(BF16),
                  gdn_proj[l].astype(BF16), w_out[l].astype(BF16), tm)

        x = _ffn(x1.reshape(b, t, d), _row(norm2_g[l]), ffn_up[l].astype(BF16), ffn_conv_w[l].astype(F32),
                 ffn_down[l].astype(BF16), _row(final_g), tm, apply_final=(l == depth - 1))
    return x
```

```python
import functools

import jax
import jax.numpy as jnp
from jax import lax
from jax.experimental import pallas as pl
from jax.experimental.pallas import tpu as pltpu

F32 = jnp.float32
BF16 = jnp.bfloat16

CHUNK = 64
RWKV_HEADS = 8
RWKV_HEAD_DIM = 64
RWKV_WIDTH = RWKV_HEADS * RWKV_HEAD_DIM
DECAY_LORA = 64
ICLR_LORA = 64
GATE_LORA = 128
GDN_HEADS = 4
GDN_HEAD_DIM = 128
GDN_WIDTH = GDN_HEADS * GDN_HEAD_DIM
GDN_CONV = 4
FFN_CONV = 3
NORM_EPS = 1e-6
L2_EPS = 1e-6
RWKV_GN_EPS = 64e-5
RWKV_SHIFT_WIDTH = 3 * RWKV_WIDTH + DECAY_LORA + ICLR_LORA + GATE_LORA
LANES = 128
SUBLANES = 8
MXU_TILE = 256
AB_PAD = LANES
GROUP = MXU_TILE // CHUNK
RWKV_GROUPS = RWKV_WIDTH // MXU_TILE

VMEM_LIMIT = 52 * 1024 * 1024


def _sigmoid(x):
    return 1.0 / (1.0 + jnp.exp(-x))


def _softplus(x):
    return jnp.maximum(x, 0.0) + jnp.log(1.0 + jnp.exp(-jnp.abs(x)))


def _dot(a, b):
    return jnp.dot(a.astype(BF16), b.astype(BF16), preferred_element_type=F32)


def _dot_nt(a, b):
    return lax.dot_general(a.astype(BF16), b.astype(BF16), (((1,), (1,)), ((), ())), preferred_element_type=F32)


def _dot_tn(a, b):
    return lax.dot_general(a.astype(BF16), b.astype(BF16), (((0,), (0,)), ((), ())), preferred_element_type=F32)


def _bdiag(x, mask16):
    reps = MXU_TILE // x.shape[0]
    return jnp.concatenate([x.astype(BF16)] * reps, axis=0) * mask16


def _cat_masks():
    ri = lax.broadcasted_iota(jnp.int32, (CHUNK, MXU_TILE), 0)
    ci = lax.broadcasted_iota(jnp.int32, (CHUNK, MXU_TILE), 1) % CHUNK
    return ri > ci, ri >= ci, (ri == ci).astype(F32)


def _unit_lower_inverse_cat(ms, eye_cat, mask16):
    ts = [eye_cat + m for m in ms]
    mks = [_dot(m, _bdiag(m, mask16)) for m in ms]
    levels = CHUNK.bit_length() - 1
    for _ in range(levels - 2):
        boths = [_dot(jnp.concatenate([mk, t], axis=0), _bdiag(mk, mask16)) for mk, t in zip(mks, ts)]
        mks = [both[:CHUNK] for both in boths]
        ts = [t + both[CHUNK:] for t, both in zip(ts, boths)]
    return [t + _dot(t, _bdiag(mk, mask16)) for t, mk in zip(ts, mks)]


def _inproj_body(x_ref, g_ref, w_ref, *out_refs):
    x = x_ref[...]
    ms = jnp.mean(x * x, axis=-1, keepdims=True)
    u = (x * lax.rsqrt(ms + NORM_EPS) * g_ref[...]).astype(BF16)
    off = 0
    for ref in out_refs:
        width = ref.shape[-1]
        ref[...] = jnp.dot(u, w_ref[:, off:off + width], preferred_element_type=F32)
        off += width


def _inproj(x2, gain, w_packed, widths, tm):
    n, d = x2.shape
    total = w_packed.shape[1]
    return pl.pallas_call(
        _inproj_body,
        grid=(n // tm,),
        in_specs=[
            pl.BlockSpec((tm, d), lambda i: (i, 0)),
            pl.BlockSpec((1, d), lambda i: (0, 0)),
            pl.BlockSpec((d, total), lambda i: (0, 0), pipeline_mode=pl.Buffered(1)),
        ],
        out_specs=[pl.BlockSpec((tm, w), lambda i: (i, 0)) for w in widths],
        out_shape=[jax.ShapeDtypeStruct((n, w), F32) for w in widths],
        compiler_params=pltpu.CompilerParams(dimension_semantics=("arbitrary",), vmem_limit_bytes=VMEM_LIMIT),
        name="inproj",
    )(x2, gain, w_packed)


def _rwkv_body(p_ref, mu_ref, w0_ref, w2_ref, a0_ref, a2_ref, g2_ref, kk_ref, ka_ref, rk_ref, lnw_ref,
               lnb_ref, m16_ref, mf_ref, y_ref, prev_ref, s_ref):
    bb, c, wd = p_ref.shape
    rows = bb * c
    hd = RWKV_HEAD_DIM

    @pl.when(pl.program_id(1) == 0)
    def _():
        prev_ref[...] = jnp.zeros_like(prev_ref)
        s_ref[...] = jnp.zeros_like(s_ref)

    row = lax.broadcasted_iota(jnp.int32, (c, 1), 0)
    ps, shifts = [], []
    for bi in range(bb):
        pb_ = p_ref[bi]
        shifts.append(jnp.where(row == 0, prev_ref[bi, 0:1, :], pltpu.roll(pb_, 1, 0)))
        prev_ref[bi, 0:1, :] = pb_[c - 1:c, :]
        ps.append(pb_)
    p = jnp.concatenate(ps, axis=0)
    pm = p + (jnp.concatenate(shifts, axis=0) - p) * mu_ref[...]

    w = RWKV_WIDTH
    r = pm[:, 0:w]
    k = pm[:, w:2 * w]
    v = pm[:, 2 * w:3 * w]
    o0 = 3 * w
    wl = pm[:, o0:o0 + DECAY_LORA]
    al = pm[:, o0 + DECAY_LORA:o0 + DECAY_LORA + ICLR_LORA]
    gl = pm[:, o0 + DECAY_LORA + ICLR_LORA:o0 + DECAY_LORA + ICLR_LORA + GATE_LORA]

    w_log = -_softplus(-(w0_ref[...] + _dot(jnp.tanh(wl), w2_ref[...]))) - 0.5
    lw = (-jnp.exp(w_log)).astype(BF16).astype(F32)
    a = _sigmoid(a0_ref[...] + _dot(al, a2_ref[...]))
    g = _dot(_sigmoid(gl), g2_ref[...])

    mask16 = m16_ref[...]
    maskf = mf_ref[...]

    def head_sums(*xs):
        stacked = jnp.concatenate([x[:, q * MXU_TILE:(q + 1) * MXU_TILE] for x in xs for q in range(RWKV_GROUPS)], axis=0)
        res = jnp.dot(stacked.astype(BF16), mask16, preferred_element_type=F32)
        outs = []
        for i in range(len(xs)):
            base = i * RWKV_GROUPS * rows
            outs.append(jnp.concatenate([res[base + q * rows:base + (q + 1) * rows] for q in range(RWKV_GROUPS)], axis=1))
        return outs

    kx = k * kk_ref[...]
    k2 = k * (1.0 + (a - 1.0) * ka_ref[...])
    kx_ss, rk_sum = head_sums(kx * kx, r * k2 * rk_ref[...])
    kk = kx * lax.rsqrt(kx_ss + L2_EPS)

    ri = lax.broadcasted_iota(jnp.int32, (rows, rows), 0)
    ci = lax.broadcasted_iota(jnp.int32, (rows, rows), 1)
    tril = ((ri >= ci) & (ri // c == ci // c)).astype(F32).astype(BF16)
    cum = jnp.dot(tril, lw.astype(BF16), preferred_element_type=F32)
    tot = jnp.concatenate([jnp.broadcast_to(cum[(bi + 1) * c - 1:(bi + 1) * c, :], (c, w)) for bi in range(bb)], axis=0)
    e_pos = jnp.exp(cum)
    e_neg = jnp.exp(-cum)
    e_prev = jnp.exp(cum - lw)
    e_rest = jnp.exp(tot - cum)
    w_tot = jnp.exp(tot)

    b_vec = kk * a
    a_t = (-kk * e_prev).astype(BF16)
    r_t = (r * e_pos).astype(BF16)
    b_t = (b_vec * e_neg).astype(BF16)
    k_t = (k2 * e_neg).astype(BF16)
    b_h = (b_vec * e_rest).astype(BF16)
    k_h = (k2 * e_rest).astype(BF16)
    v16 = v.astype(BF16)

    strict_cat, incl_cat, eye_cat = _cat_masks()
    mask2 = jnp.concatenate([strict_cat, incl_cat], axis=0)

    chains = [(bi, q) for bi in range(bb) for q in range(RWKV_GROUPS)]
    rs_of = lambda bi: slice(bi * c, (bi + 1) * c)
    gs_of = lambda q: slice(q * MXU_TILE, (q + 1) * MXU_TILE)
    s_bds = [s_ref[bi, q] for bi, q in chains]
    pxs = []
    for (bi, q), s_bd in zip(chains, s_bds):
        rs, gs = rs_of(bi), gs_of(q)
        l2 = jnp.concatenate([a_t[rs, gs], r_t[rs, gs]], axis=0)
        w3 = jnp.concatenate([_bdiag(b_t[rs, gs], mask16), _bdiag(k_t[rs, gs], mask16), s_bd.astype(BF16)], axis=0)
        pxs.append(_dot_nt(l2, w3))
    pbs = [jnp.where(mask2, px[:, 0:MXU_TILE], 0.0) for px in pxs]
    pks = [jnp.where(mask2, px[:, MXU_TILE:2 * MXU_TILE], 0.0) for px in pxs]
    xas = [px[:, 2 * MXU_TILE:] + _dot(pk, _bdiag(v16[rs_of(bi), gs_of(q)], mask16))
           for (bi, q), px, pk in zip(chains, pxs, pks)]
    tinvs = _unit_lower_inverse_cat([pb[:c] for pb in pbs], eye_cat, mask16)
    us = [_dot(tinv, _bdiag(xa[:c], mask16)) for tinv, xa in zip(tinvs, xas)]
    ys = [xa[c:] + _dot(pb[c:], _bdiag(u, mask16)) for xa, pb, u in zip(xas, pbs, us)]
    for (bi, q), s_bd, u in zip(chains, s_bds, us):
        rs, gs = rs_of(bi), gs_of(q)
        uv = jnp.concatenate([u.astype(BF16), v16[rs, gs]], axis=0)
        bk = jnp.concatenate([b_h[rs, gs], k_h[rs, gs]], axis=0)
        s_ref[bi, q] = s_bd * w_tot[bi * c:bi * c + 1, gs] + _dot_tn(uv, bk) * maskf
    y_rows = [jnp.concatenate(ys[bi * RWKV_GROUPS:(bi + 1) * RWKV_GROUPS], axis=1) for bi in range(bb)]

    y = jnp.concatenate(y_rows, axis=0)
    inv_hd = 1.0 / hd
    (mean,) = head_sums(y)
    yc = y - mean * inv_hd
    (var,) = head_sums(yc * yc)
    yn = yc * lax.rsqrt(var * inv_hd + RWKV_GN_EPS) * lnw_ref[...] + lnb_ref[...]
    out = (yn + rk_sum * v) * g
    for bi in range(bb):
        y_ref[bi] = out[bi * c:(bi + 1) * c, :]


def _rwkv(p3, params, mask16, maskf, bb):
    b, t, wd = p3.shape
    c = CHUNK
    full = lambda arr: pl.BlockSpec(arr.shape, lambda i, j: (0,) * arr.ndim)
    return pl.pallas_call(
        _rwkv_body,
        grid=(b // bb, t // c),
        in_specs=[pl.BlockSpec((bb, c, wd), lambda i, j: (i, j, 0))] + [full(a) for a in params]
        + [full(mask16), full(maskf)],
        out_specs=pl.BlockSpec((bb, c, RWKV_WIDTH), lambda i, j: (i, j, 0)),
        out_shape=jax.ShapeDtypeStruct((b, t, RWKV_WIDTH), F32),
        scratch_shapes=[
            pltpu.VMEM((bb, SUBLANES, wd), F32),
            pltpu.VMEM((bb, RWKV_GROUPS, MXU_TILE, MXU_TILE), F32),
        ],
        compiler_params=pltpu.CompilerParams(dimension_semantics=("arbitrary", "arbitrary"),
                                             vmem_limit_bytes=VMEM_LIMIT),
        name="rwkv",
    )(p3, *params, mask16, maskf)


def _gdn_body(qkv_ref, z_ref, ab_ref, cw_ref, alog_ref, dtb_ref, nw_ref, m16_ref, hm16_ref, y_ref, prev_ref, s_ref):
    bb, c, _ = qkv_ref.shape
    rows = bb * c
    hd = GDN_HEAD_DIM
    nh = GDN_HEADS

    @pl.when(pl.program_id(1) == 0)
    def _():
        prev_ref[...] = jnp.zeros_like(prev_ref)
        s_ref[...] = jnp.zeros_like(s_ref)

    convs = []
    for bi in range(bb):
        x = qkv_ref[bi]
        xp = jnp.concatenate([prev_ref[bi], x], axis=0)
        conv = x * cw_ref[GDN_CONV - 1:GDN_CONV, :]
        for i in range(GDN_CONV - 1):
            lo = SUBLANES - (GDN_CONV - 1) + i
            conv = conv + xp[lo:lo + c, :] * cw_ref[i:i + 1, :]
        prev_ref[bi] = x[c - SUBLANES:c, :]
        convs.append(conv)
    conv = jnp.concatenate(convs, axis=0)
    qkv = conv * _sigmoid(conv)
    w = GDN_WIDTH

    ab = jnp.concatenate([ab_ref[bi] for bi in range(bb)], axis=0)
    g_all = -jnp.exp(alog_ref[...]) * _softplus(ab + dtb_ref[...])
    beta_all = _sigmoid(ab)

    ri = lax.broadcasted_iota(jnp.int32, (rows, rows), 0)
    ci = lax.broadcasted_iota(jnp.int32, (rows, rows), 1)
    tril16 = ((ri >= ci) & (ri // c == ci // c)).astype(F32).astype(BF16)
    g_hi = g_all.astype(BF16)
    rem = g_all - g_hi.astype(F32)
    g_mid = rem.astype(BF16)
    g_lo = (rem - g_mid.astype(F32)).astype(BF16)
    gcp = jnp.dot(tril16, jnp.concatenate([g_hi, g_mid, g_lo], axis=1), preferred_element_type=F32)
    gc_all = gcp[:, 0:LANES] + gcp[:, LANES:2 * LANES] + gcp[:, 2 * LANES:3 * LANES]
    eg_all = jnp.exp(gc_all)

    mask16 = m16_ref[...]
    hmask16 = hm16_ref[...]
    strict_cat, incl_cat, eye_cat = _cat_masks()
    lane_head = lax.broadcasted_iota(jnp.int32, (c, MXU_TILE), 1) // CHUNK

    qs, ks, kbs = [], [], []
    for h in range(nh):
        qh = qkv[:, h * hd:(h + 1) * hd]
        kh = qkv[:, w + h * hd:w + (h + 1) * hd]
        qs.append(qh * lax.rsqrt(jnp.sum(qh * qh, axis=-1, keepdims=True) + L2_EPS) * (hd ** -0.5))
        kh = kh * lax.rsqrt(jnp.sum(kh * kh, axis=-1, keepdims=True) + L2_EPS)
        ks.append(kh)
        kbs.append(kh * beta_all[:, nh + h:nh + h + 1])
    k_all = jnp.concatenate(ks, axis=1).astype(BF16)
    kb_all = jnp.concatenate(kbs, axis=1).astype(BF16)
    q_all = jnp.concatenate(qs, axis=1).astype(BF16)

    rs_of = lambda bi: slice(bi * c, (bi + 1) * c)
    gc_bs = [gc_all[rs_of(bi)] for bi in range(bb)]
    glasts = [gc_b[c - 1:c, :] for gc_b in gc_bs]
    erests = [jnp.exp(glast - gc_b) for glast, gc_b in zip(glasts, gc_bs)]
    elasts = [jnp.exp(glast) for glast in glasts]

    decays = []
    for gc_b in gc_bs:
        f_cat = jnp.zeros((c, MXU_TILE), F32)
        for h in range(nh):
            f_cat = jnp.where(lane_head == h, gc_b[:, h:h + 1], f_cat)
        r_cat = jnp.sum(f_cat * eye_cat, axis=0, keepdims=True)
        decays.append(jnp.where(incl_cat, jnp.exp(jnp.where(incl_cat, f_cat - r_cat, 0.0)), 0.0))

    kqs = [_dot_nt(jnp.concatenate([kb_all[rs_of(bi)], q_all[rs_of(bi)]], axis=0),
                   jnp.concatenate([k_all[rs_of(bi)]] * nh, axis=0) * hmask16) for bi in range(bb)]
    attns = [kq[c:] * decay for kq, decay in zip(kqs, decays)]
    tinvs = _unit_lower_inverse_cat([-jnp.where(strict_cat, kq[:c] * decay, 0.0) for kq, decay in zip(kqs, decays)],
                                    eye_cat, mask16)

    chains = [(bi, h) for bi in range(bb) for h in range(nh)]
    tl_of = lambda h: slice(h * c, (h + 1) * c)
    egs = [eg_all[rs_of(bi), h:h + 1] for bi, h in chains]
    uws = []
    for (bi, h), eg in zip(chains, egs):
        rs = rs_of(bi)
        vh = qkv[rs, 2 * w + h * hd:2 * w + (h + 1) * hd]
        beta = beta_all[rs, nh + h:nh + h + 1]
        uws.append(_dot(tinvs[bi][:, tl_of(h)], jnp.concatenate([vh * beta, kbs[h][rs] * eg], axis=1)))
    ss = [s_ref[bi, h] for bi, h in chains]
    wqs = [_dot(jnp.concatenate([uw[:, hd:], qs[h][rs_of(bi)] * eg], axis=0), s)
           for (bi, h), uw, eg, s in zip(chains, uws, egs, ss)]
    v_news = [uw[:, :hd] - wq[:c] for uw, wq in zip(uws, wqs)]
    os_ = [wq[c:] + _dot(attns[bi][:, tl_of(h)], v_new) for (bi, h), wq, v_new in zip(chains, wqs, v_news)]
    for (bi, h), s, v_new in zip(chains, ss, v_news):
        s_ref[bi, h] = s * elasts[bi][:, h:h + 1] + _dot_tn(ks[h][rs_of(bi)] * erests[bi][:, h:h + 1], v_new)
    for (bi, h), o in zip(chains, os_):
        sl = slice(h * hd, (h + 1) * hd)
        on = o * lax.rsqrt(jnp.mean(o * o, axis=-1, keepdims=True) + NORM_EPS) * nw_ref[...]
        zh = z_ref[bi, :, sl]
        y_ref[bi, :, sl] = on * (zh * _sigmoid(zh))


def _gdn(qkv3, z3, ab3, cw, alog, dtb, nw, mask16, hmask16, bb):
    b, t, wd = qkv3.shape
    c = CHUNK
    full = lambda arr: pl.BlockSpec(arr.shape, lambda i, j: (0,) * arr.ndim)
    return pl.pallas_call(
        _gdn_body,
        grid=(b // bb, t // c),
        in_specs=[
            pl.BlockSpec((bb, c, wd), lambda i, j: (i, j, 0)),
            pl.BlockSpec((bb, c, GDN_WIDTH), lambda i, j: (i, j, 0)),
            pl.BlockSpec((bb, c, AB_PAD), lambda i, j: (i, j, 0)),
            full(cw), full(alog), full(dtb), full(nw), full(mask16), full(hmask16),
        ],
        out_specs=pl.BlockSpec((bb, c, GDN_WIDTH), lambda i, j: (i, j, 0)),
        out_shape=jax.ShapeDtypeStruct((b, t, GDN_WIDTH), F32),
        scratch_shapes=[
            pltpu.VMEM((bb, SUBLANES, wd), F32),
            pltpu.VMEM((bb, GDN_HEADS, GDN_HEAD_DIM, GDN_HEAD_DIM), F32),
        ],
        compiler_params=pltpu.CompilerParams(dimension_semantics=("arbitrary", "arbitrary"),
                                             vmem_limit_bytes=VMEM_LIMIT),
        name="gdn",
    )(qkv3, z3, ab3, cw, alog, dtb, nw, mask16, hmask16)


def _mix_body(x_ref, ya_ref, yb_ref, gate_ref, rp_ref, gp_ref, wo_ref, o_ref):
    d = x_ref.shape[-1]
    ya = jnp.dot(ya_ref[...].astype(BF16), rp_ref[...], preferred_element_type=F32)
    yb = jnp.dot(yb_ref[...].astype(BF16), gp_ref[...], preferred_element_type=F32)
    gate = gate_ref[...]
    mixed = _sigmoid(gate[:, :d]) * ya + _sigmoid(gate[:, d:]) * yb
    o_ref[...] = x_ref[...] + jnp.dot(mixed.astype(BF16), wo_ref[...], preferred_element_type=F32)


def _mix(x2, ya2, yb2, gate2, rp, gp, wo, tm):
    n, d = x2.shape
    full = lambda arr: pl.BlockSpec(arr.shape, lambda i: (0, 0))
    return pl.pallas_call(
        _mix_body,
        grid=(n // tm,),
        in_specs=[
            pl.BlockSpec((tm, d), lambda i: (i, 0)),
            pl.BlockSpec((tm, ya2.shape[1]), lambda i: (i, 0)),
            pl.BlockSpec((tm, yb2.shape[1]), lambda i: (i, 0)),
            pl.BlockSpec((tm, 2 * d), lambda i: (i, 0)),
            full(rp), full(gp), full(wo),
        ],
        out_specs=pl.BlockSpec((tm, d), lambda i: (i, 0)),
        out_shape=jax.ShapeDtypeStruct((n, d), F32),
        compiler_params=pltpu.CompilerParams(dimension_semantics=("arbitrary",), vmem_limit_bytes=VMEM_LIMIT),
        name="mix",
    )(x2, ya2, yb2, gate2, rp, gp, wo)


def _ffn_body(x_ref, g_ref, up_ref, cw_ref, down_ref, fg_ref, o_ref, prev_ref, *, col_block, apply_final):
    tm = x_ref.shape[0]
    hidden = down_ref.shape[0]

    @pl.when(pl.program_id(1) == 0)
    def _():
        prev_ref[...] = jnp.zeros_like(prev_ref)

    x = x_ref[...]
    ms = jnp.mean(x * x, axis=-1, keepdims=True)
    xn = (x * lax.rsqrt(ms + NORM_EPS) * g_ref[...]).astype(BF16)

    def conv_cols(lo):
        h = jnp.dot(xn, up_ref[:, lo:lo + col_block], preferred_element_type=F32)
        hp = jnp.concatenate([prev_ref[:, lo:lo + col_block], h], axis=0)
        out = h * cw_ref[FFN_CONV - 1:FFN_CONV, lo:lo + col_block]
        for i in range(FFN_CONV - 1):
            s = SUBLANES - (FFN_CONV - 1) + i
            out = out + hp[s:s + tm, :] * cw_ref[i:i + 1, lo:lo + col_block]
        prev_ref[:, lo:lo + col_block] = h[tm - SUBLANES:tm, :]
        return out

    acc = x
    for j in range(hidden // col_block):
        lo = j * col_block
        hg = conv_cols(lo)
        hu = conv_cols(hidden + lo)
        act = (hg * _sigmoid(hg) * hu).astype(BF16)
        acc = acc + jnp.dot(act, down_ref[lo:lo + col_block, :], preferred_element_type=F32)
    if apply_final:
        ms2 = jnp.mean(acc * acc, axis=-1, keepdims=True)
        acc = acc * lax.rsqrt(ms2 + NORM_EPS) * fg_ref[...]
    o_ref[...] = acc


def _ffn(x3, gain, up, cw, down, final_g, tm, apply_final):
    b, t, d = x3.shape
    hidden = down.shape[0]
    full = lambda arr: pl.BlockSpec(arr.shape, lambda i, j: (0, 0))
    single = lambda arr: pl.BlockSpec(arr.shape, lambda i, j: (0, 0), pipeline_mode=pl.Buffered(1))
    body = functools.partial(_ffn_body, col_block=MXU_TILE, apply_final=apply_final)
    return pl.pallas_call(
        body,
        grid=(b, t // tm),
        in_specs=[
            pl.BlockSpec((None, tm, d), lambda i, j: (i, j, 0)),
            full(gain), single(up), full(cw), single(down), full(final_g),
        ],
        out_specs=pl.BlockSpec((None, tm, d), lambda i, j: (i, j, 0)),
        out_shape=jax.ShapeDtypeStruct((b, t, d), F32),
        scratch_shapes=[pltpu.VMEM((SUBLANES, 2 * hidden), F32)],
        compiler_params=pltpu.CompilerParams(dimension_semantics=("arbitrary", "arbitrary"),
                                             vmem_limit_bytes=VMEM_LIMIT),
        name="ffn",
    )(x3, gain, up, cw, down, final_g)


def _row(vec):
    return vec.reshape(1, -1).astype(F32)


def _pad_lanes(vec, offset=0):
    out = jnp.zeros((1, LANES), F32)
    return lax.dynamic_update_slice(out, vec.reshape(1, -1).astype(F32), (0, offset))


def kernel(x, norm1_g, w_in, rwkv_mu, rwkv_w0, rwkv_w2, rwkv_a0, rwkv_a2, rwkv_g2, rwkv_k_k, rwkv_k_a,
           rwkv_r_k, rwkv_ln_w, rwkv_ln_b, rwkv_proj, gdn_conv_w, gdn_a_log, gdn_dt_bias, gdn_norm_w,
           gdn_proj, w_out, norm2_g, ffn_up, ffn_conv_w, ffn_down, final_g):
    b, t, d = x.shape
    n = b * t
    depth = norm1_g.shape[0]
    tm = 256
    bb = 4

    blk = jnp.arange(MXU_TILE) // CHUNK
    maskf = (blk[:, None] == blk[None, :]).astype(F32)
    mask16 = maskf.astype(BF16)
    hmask16 = (jnp.arange(GDN_HEADS * CHUNK)[:, None] // CHUNK == jnp.arange(GDN_WIDTH)[None, :] // GDN_HEAD_DIM).astype(BF16)

    n_ab = 2 * GDN_HEADS
    c_ab = RWKV_SHIFT_WIDTH + 4 * GDN_WIDTH
    widths = (RWKV_SHIFT_WIDTH, 3 * GDN_WIDTH, GDN_WIDTH, AB_PAD, 2 * d)

    for l in range(depth):
        wl = w_in[l]
        w_packed = jnp.concatenate(
            [wl[:, :c_ab + n_ab], jnp.zeros((d, AB_PAD - n_ab), wl.dtype), wl[:, c_ab + n_ab:]], axis=1).astype(BF16)
        p_rwkv, qkv, z, ab, gate = _inproj(x.reshape(n, d), _row(norm1_g[l]), w_packed, widths, tm)

        rwkv_params = (_row(rwkv_mu[l]), _row(rwkv_w0[l]), rwkv_w2[l].astype(BF16), _row(rwkv_a0[l]),
                       rwkv_a2[l].astype(BF16), rwkv_g2[l].astype(BF16), _row(rwkv_k_k[l]), _row(rwkv_k_a[l]),
                       _row(rwkv_r_k[l]), _row(rwkv_ln_w[l]), _row(rwkv_ln_b[l]))
        y_a = _rwkv(p_rwkv.reshape(b, t, -1), rwkv_params, mask16, maskf, bb)

        y_b = _gdn(qkv.reshape(b, t, -1), z.reshape(b, t, -1), ab.reshape(b, t, -1), gdn_conv_w[l].astype(F32),
                   _pad_lanes(gdn_a_log[l]), _pad_lanes(gdn_dt_bias[l]), _row(gdn_norm_w[l]), mask16, hmask16, bb)

        x1 = _mix(x.reshape(n, d), y_a.reshape(n, -1), y_b.reshape(n, -1), gate, rwkv_proj[l].astype(BF16),
                  gdn_proj[l].astype(BF16), w_out[l].astype(BF16), tm)

        x = _ffn(x1.reshape(b, t, d), _row(norm2_g[l]), ffn_up[l].astype(BF16), ffn_conv_w[l].astype(F32),
                 ffn_down[l].astype(BF16), _row(final_g), tm, apply_final=(l == depth - 1))
    return x
```

```python
import functools

import jax
import jax.numpy as jnp
from jax import lax
from jax.experimental import pallas as pl
from jax.experimental.pallas import tpu as pltpu

F32 = jnp.float32
BF16 = jnp.bfloat16

CHUNK = 64
RWKV_HEADS = 8
RWKV_HEAD_DIM = 64
RWKV_WIDTH = RWKV_HEADS * RWKV_HEAD_DIM
DECAY_LORA = 64
ICLR_LORA = 64
GATE_LORA = 128
GDN_HEADS = 4
GDN_HEAD_DIM = 128
GDN_WIDTH = GDN_HEADS * GDN_HEAD_DIM
GDN_CONV = 4
FFN_CONV = 3
NORM_EPS = 1e-6
L2_EPS = 1e-6
RWKV_GN_EPS = 64e-5
RWKV_SHIFT_WIDTH = 3 * RWKV_WIDTH + DECAY_LORA + ICLR_LORA + GATE_LORA
LANES = 128
SUBLANES = 8
MXU_TILE = 256
AB_PAD = LANES
GROUP = MXU_TILE // CHUNK
RWKV_GROUPS = RWKV_WIDTH // MXU_TILE

VMEM_LIMIT = 52 * 1024 * 1024


def _sigmoid(x):
    return 1.0 / (1.0 + jnp.exp(-x))


def _softplus(x):
    return jnp.maximum(x, 0.0) + jnp.log(1.0 + jnp.exp(-jnp.abs(x)))


def _dot(a, b):
    return jnp.dot(a.astype(BF16), b.astype(BF16), preferred_element_type=F32)


def _dot_nt(a, b):
    return lax.dot_general(a.astype(BF16), b.astype(BF16), (((1,), (1,)), ((), ())), preferred_element_type=F32)


def _dot_tn(a, b):
    return lax.dot_general(a.astype(BF16), b.astype(BF16), (((0,), (0,)), ((), ())), preferred_element_type=F32)


def _bdiag(x, mask16):
    reps = MXU_TILE // x.shape[0]
    return jnp.concatenate([x.astype(BF16)] * reps, axis=0) * mask16


def _cat_masks():
    ri = lax.broadcasted_iota(jnp.int32, (CHUNK, MXU_TILE), 0)
    ci = lax.broadcasted_iota(jnp.int32, (CHUNK, MXU_TILE), 1) % CHUNK
    return ri > ci, ri >= ci, (ri == ci).astype(F32)


def _unit_lower_inverse_cat(ms, eye_cat, mask16):
    ts = [eye_cat + m for m in ms]
    mks = [_dot(m, _bdiag(m, mask16)) for m in ms]
    levels = CHUNK.bit_length() - 1
    for _ in range(levels - 2):
        boths = [_dot(jnp.concatenate([mk, t], axis=0), _bdiag(mk, mask16)) for mk, t in zip(mks, ts)]
        mks = [both[:CHUNK] for both in boths]
        ts = [t + both[CHUNK:] for t, both in zip(ts, boths)]
    return [t + _dot(t, _bdiag(mk, mask16)) for t, mk in zip(ts, mks)]


def _inproj_body(x_ref, g_ref, w_ref, *out_refs):
    x = x_ref[...]
    ms = jnp.mean(x * x, axis=-1, keepdims=True)
    u = (x * lax.rsqrt(ms + NORM_EPS) * g_ref[...]).astype(BF16)
    off = 0
    for ref in out_refs:
        width = ref.shape[-1]
        ref[...] = jnp.dot(u, w_ref[:, off:off + width], preferred_element_type=F32).astype(ref.dtype)
        off += width


def _inproj(x2, gain, w_packed, widths, dtypes, tm):
    n, d = x2.shape
    total = w_packed.shape[1]
    return pl.pallas_call(
        _inproj_body,
        grid=(n // tm,),
        in_specs=[
            pl.BlockSpec((tm, d), lambda i: (i, 0)),
            pl.BlockSpec((1, d), lambda i: (0, 0)),
            pl.BlockSpec((d, total), lambda i: (0, 0), pipeline_mode=pl.Buffered(1)),
        ],
        out_specs=[pl.BlockSpec((tm, w), lambda i: (i, 0)) for w in widths],
        out_shape=[jax.ShapeDtypeStruct((n, w), dt) for w, dt in zip(widths, dtypes)],
        compiler_params=pltpu.CompilerParams(dimension_semantics=("arbitrary",), vmem_limit_bytes=VMEM_LIMIT),
        name="inproj",
    )(x2, gain, w_packed)


def _rwkv_body(p_ref, mu_ref, w0_ref, w2_ref, a0_ref, a2_ref, g2_ref, kk_ref, ka_ref, rk_ref, lnw_ref,
               lnb_ref, m16_ref, mf_ref, y_ref, prev_ref, s_ref):
    bb, c, wd = p_ref.shape
    rows = bb * c
    hd = RWKV_HEAD_DIM

    @pl.when(pl.program_id(1) == 0)
    def _():
        prev_ref[...] = jnp.zeros_like(prev_ref)
        s_ref[...] = jnp.zeros_like(s_ref)

    row = lax.broadcasted_iota(jnp.int32, (c, 1), 0)
    ps, shifts = [], []
    for bi in range(bb):
        pb_ = p_ref[bi].astype(F32)
        shifts.append(jnp.where(row == 0, prev_ref[bi, 0:1, :], pltpu.roll(pb_, 1, 0)))
        prev_ref[bi, 0:1, :] = pb_[c - 1:c, :]
        ps.append(pb_)
    p = jnp.concatenate(ps, axis=0)
    pm = p + (jnp.concatenate(shifts, axis=0) - p) * mu_ref[...]

    w = RWKV_WIDTH
    r = pm[:, 0:w]
    k = pm[:, w:2 * w]
    v = pm[:, 2 * w:3 * w]
    o0 = 3 * w
    wl = pm[:, o0:o0 + DECAY_LORA]
    al = pm[:, o0 + DECAY_LORA:o0 + DECAY_LORA + ICLR_LORA]
    gl = pm[:, o0 + DECAY_LORA + ICLR_LORA:o0 + DECAY_LORA + ICLR_LORA + GATE_LORA]

    w_log = -_softplus(-(w0_ref[...] + _dot(jnp.tanh(wl), w2_ref[...]))) - 0.5
    lw = (-jnp.exp(w_log)).astype(BF16).astype(F32)
    a = _sigmoid(a0_ref[...] + _dot(al, a2_ref[...]))
    g = _dot(_sigmoid(gl), g2_ref[...])

    mask16 = m16_ref[...]
    maskf = mf_ref[...]

    def head_sums(*xs):
        stacked = jnp.concatenate([x[:, q * MXU_TILE:(q + 1) * MXU_TILE] for x in xs for q in range(RWKV_GROUPS)], axis=0)
        res = jnp.dot(stacked.astype(BF16), mask16, preferred_element_type=F32)
        outs = []
        for i in range(len(xs)):
            base = i * RWKV_GROUPS * rows
            outs.append(jnp.concatenate([res[base + q * rows:base + (q + 1) * rows] for q in range(RWKV_GROUPS)], axis=1))
        return outs

    kx = k * kk_ref[...]
    k2 = k * (1.0 + (a - 1.0) * ka_ref[...])
    kx_ss, rk_sum = head_sums(kx * kx, r * k2 * rk_ref[...])
    kk = kx * lax.rsqrt(kx_ss + L2_EPS)

    ri = lax.broadcasted_iota(jnp.int32, (rows, rows), 0)
    ci = lax.broadcasted_iota(jnp.int32, (rows, rows), 1)
    tril = ((ri >= ci) & (ri // c == ci // c)).astype(F32).astype(BF16)
    cum = jnp.dot(tril, lw.astype(BF16), preferred_element_type=F32)
    tot = jnp.concatenate([jnp.broadcast_to(cum[(bi + 1) * c - 1:(bi + 1) * c, :], (c, w)) for bi in range(bb)], axis=0)
    e_pos = jnp.exp(cum)
    e_neg = jnp.exp(-cum)
    e_prev = jnp.exp(cum - lw)
    e_rest = jnp.exp(tot - cum)
    w_tot = jnp.exp(tot)

    b_vec = kk * a
    a_t = (-kk * e_prev).astype(BF16)
    r_t = (r * e_pos).astype(BF16)
    b_t = (b_vec * e_neg).astype(BF16)
    k_t = (k2 * e_neg).astype(BF16)
    b_h = (b_vec * e_rest).astype(BF16)
    k_h = (k2 * e_rest).astype(BF16)
    v16 = v.astype(BF16)

    strict_cat, incl_cat, eye_cat = _cat_masks()
    mask2 = jnp.concatenate([strict_cat, incl_cat], axis=0)

    chains = [(bi, q) for bi in range(bb) for q in range(RWKV_GROUPS)]
    rs_of = lambda bi: slice(bi * c, (bi + 1) * c)
    gs_of = lambda q: slice(q * MXU_TILE, (q + 1) * MXU_TILE)
    s_bds = [s_ref[bi, q] for bi, q in chains]
    pxs = []
    for (bi, q), s_bd in zip(chains, s_bds):
        rs, gs = rs_of(bi), gs_of(q)
        l2 = jnp.concatenate([a_t[rs, gs], r_t[rs, gs]], axis=0)
        w3 = jnp.concatenate([_bdiag(b_t[rs, gs], mask16), _bdiag(k_t[rs, gs], mask16), s_bd.astype(BF16)], axis=0)
        pxs.append(_dot_nt(l2, w3))
    pbs = [jnp.where(mask2, px[:, 0:MXU_TILE], 0.0) for px in pxs]
    pks = [jnp.where(mask2, px[:, MXU_TILE:2 * MXU_TILE], 0.0) for px in pxs]
    xas = [px[:, 2 * MXU_TILE:] + _dot(pk, _bdiag(v16[rs_of(bi), gs_of(q)], mask16))
           for (bi, q), px, pk in zip(chains, pxs, pks)]
    tinvs = _unit_lower_inverse_cat([pb[:c] for pb in pbs], eye_cat, mask16)
    us = [_dot(tinv, _bdiag(xa[:c], mask16)) for tinv, xa in zip(tinvs, xas)]
    ys = [xa[c:] + _dot(pb[c:], _bdiag(u, mask16)) for xa, pb, u in zip(xas, pbs, us)]
    for (bi, q), s_bd, u in zip(chains, s_bds, us):
        rs, gs = rs_of(bi), gs_of(q)
        uv = jnp.concatenate([u.astype(BF16), v16[rs, gs]], axis=0)
        bk = jnp.concatenate([b_h[rs, gs], k_h[rs, gs]], axis=0)
        s_ref[bi, q] = s_bd * w_tot[bi * c:bi * c + 1, gs] + _dot_tn(uv, bk) * maskf
    y_rows = [jnp.concatenate(ys[bi * RWKV_GROUPS:(bi + 1) * RWKV_GROUPS], axis=1) for bi in range(bb)]

    y = jnp.concatenate(y_rows, axis=0)
    inv_hd = 1.0 / hd
    (mean,) = head_sums(y)
    yc = y - mean * inv_hd
    (var,) = head_sums(yc * yc)
    yn = yc * lax.rsqrt(var * inv_hd + RWKV_GN_EPS) * lnw_ref[...] + lnb_ref[...]
    out = (yn + rk_sum * v) * g
    for bi in range(bb):
        y_ref[bi] = out[bi * c:(bi + 1) * c, :].astype(y_ref.dtype)


def _rwkv(p3, params, mask16, maskf, bb):
    b, t, wd = p3.shape
    c = CHUNK
    full = lambda arr: pl.BlockSpec(arr.shape, lambda i, j: (0,) * arr.ndim)
    return pl.pallas_call(
        _rwkv_body,
        grid=(b // bb, t // c),
        in_specs=[pl.BlockSpec((bb, c, wd), lambda i, j: (i, j, 0))] + [full(a) for a in params]
        + [full(mask16), full(maskf)],
        out_specs=pl.BlockSpec((bb, c, RWKV_WIDTH), lambda i, j: (i, j, 0)),
        out_shape=jax.ShapeDtypeStruct((b, t, RWKV_WIDTH), BF16),
        scratch_shapes=[
            pltpu.VMEM((bb, SUBLANES, wd), F32),
            pltpu.VMEM((bb, RWKV_GROUPS, MXU_TILE, MXU_TILE), F32),
        ],
        compiler_params=pltpu.CompilerParams(dimension_semantics=("arbitrary", "arbitrary"),
                                             vmem_limit_bytes=VMEM_LIMIT),
        name="rwkv",
    )(p3, *params, mask16, maskf)


def _gdn_body(qkv_ref, z_ref, ab_ref, cw_ref, alog_ref, dtb_ref, nw_ref, m16_ref, hm16_ref, y_ref, prev_ref, s_ref):
    bb, c, _ = qkv_ref.shape
    rows = bb * c
    hd = GDN_HEAD_DIM
    nh = GDN_HEADS

    @pl.when(pl.program_id(1) == 0)
    def _():
        prev_ref[...] = jnp.zeros_like(prev_ref)
        s_ref[...] = jnp.zeros_like(s_ref)

    convs = []
    for bi in range(bb):
        x = qkv_ref[bi].astype(F32)
        xp = jnp.concatenate([prev_ref[bi], x], axis=0)
        conv = x * cw_ref[GDN_CONV - 1:GDN_CONV, :]
        for i in range(GDN_CONV - 1):
            lo = SUBLANES - (GDN_CONV - 1) + i
            conv = conv + xp[lo:lo + c, :] * cw_ref[i:i + 1, :]
        prev_ref[bi] = x[c - SUBLANES:c, :]
        convs.append(conv)
    conv = jnp.concatenate(convs, axis=0)
    qkv = conv * _sigmoid(conv)
    w = GDN_WIDTH

    ab = jnp.concatenate([ab_ref[bi] for bi in range(bb)], axis=0)
    g_all = -jnp.exp(alog_ref[...]) * _softplus(ab + dtb_ref[...])
    beta_all = _sigmoid(ab)

    ri = lax.broadcasted_iota(jnp.int32, (rows, rows), 0)
    ci = lax.broadcasted_iota(jnp.int32, (rows, rows), 1)
    tril16 = ((ri >= ci) & (ri // c == ci // c)).astype(F32).astype(BF16)
    g_hi = g_all.astype(BF16)
    rem = g_all - g_hi.astype(F32)
    g_mid = rem.astype(BF16)
    g_lo = (rem - g_mid.astype(F32)).astype(BF16)
    gcp = jnp.dot(tril16, jnp.concatenate([g_hi, g_mid, g_lo], axis=1), preferred_element_type=F32)
    gc_all = gcp[:, 0:LANES] + gcp[:, LANES:2 * LANES] + gcp[:, 2 * LANES:3 * LANES]
    eg_all = jnp.exp(gc_all)

    mask16 = m16_ref[...]
    hmask16 = hm16_ref[...]
    strict_cat, incl_cat, eye_cat = _cat_masks()
    lane_head = lax.broadcasted_iota(jnp.int32, (c, MXU_TILE), 1) // CHUNK

    qs, ks, kbs = [], [], []
    for h in range(nh):
        qh = qkv[:, h * hd:(h + 1) * hd]
        kh = qkv[:, w + h * hd:w + (h + 1) * hd]
        qs.append(qh * lax.rsqrt(jnp.sum(qh * qh, axis=-1, keepdims=True) + L2_EPS) * (hd ** -0.5))
        kh = kh * lax.rsqrt(jnp.sum(kh * kh, axis=-1, keepdims=True) + L2_EPS)
        ks.append(kh)
        kbs.append(kh * beta_all[:, nh + h:nh + h + 1])
    k_all = jnp.concatenate(ks, axis=1).astype(BF16)
    kb_all = jnp.concatenate(kbs, axis=1).astype(BF16)
    q_all = jnp.concatenate(qs, axis=1).astype(BF16)

    rs_of = lambda bi: slice(bi * c, (bi + 1) * c)
    gc_bs = [gc_all[rs_of(bi)] for bi in range(bb)]
    glasts = [gc_b[c - 1:c, :] for gc_b in gc_bs]
    erests = [jnp.exp(glast - gc_b) for glast, gc_b in zip(glasts, gc_bs)]
    elasts = [jnp.exp(glast) for glast in glasts]

    decays = []
    for gc_b in gc_bs:
        f_cat = jnp.zeros((c, MXU_TILE), F32)
        for h in range(nh):
            f_cat = jnp.where(lane_head == h, gc_b[:, h:h + 1], f_cat)
        r_cat = jnp.sum(f_cat * eye_cat, axis=0, keepdims=True)
        decays.append(jnp.where(incl_cat, jnp.exp(jnp.where(incl_cat, f_cat - r_cat, 0.0)), 0.0))

    kqs = [_dot_nt(jnp.concatenate([kb_all[rs_of(bi)], q_all[rs_of(bi)]], axis=0),
                   jnp.concatenate([k_all[rs_of(bi)]] * nh, axis=0) * hmask16) for bi in range(bb)]
    attns = [kq[c:] * decay for kq, decay in zip(kqs, decays)]
    tinvs = _unit_lower_inverse_cat([-jnp.where(strict_cat, kq[:c] * decay, 0.0) for kq, decay in zip(kqs, decays)],
                                    eye_cat, mask16)

    chains = [(bi, h) for bi in range(bb) for h in range(nh)]
    tl_of = lambda h: slice(h * c, (h + 1) * c)
    egs = [eg_all[rs_of(bi), h:h + 1] for bi, h in chains]
    uws = []
    for (bi, h), eg in zip(chains, egs):
        rs = rs_of(bi)
        vh = qkv[rs, 2 * w + h * hd:2 * w + (h + 1) * hd]
        beta = beta_all[rs, nh + h:nh + h + 1]
        uws.append(_dot(tinvs[bi][:, tl_of(h)], jnp.concatenate([vh * beta, kbs[h][rs] * eg], axis=1)))
    ss = [s_ref[bi, h] for bi, h in chains]
    wqs = [_dot(jnp.concatenate([uw[:, hd:], qs[h][rs_of(bi)] * eg], axis=0), s)
           for (bi, h), uw, eg, s in zip(chains, uws, egs, ss)]
    v_news = [uw[:, :hd] - wq[:c] for uw, wq in zip(uws, wqs)]
    os_ = [wq[c:] + _dot(attns[bi][:, tl_of(h)], v_new) for (bi, h), wq, v_new in zip(chains, wqs, v_news)]
    for (bi, h), s, v_new in zip(chains, ss, v_news):
        s_ref[bi, h] = s * elasts[bi][:, h:h + 1] + _dot_tn(ks[h][rs_of(bi)] * erests[bi][:, h:h + 1], v_new)
    for (bi, h), o in zip(chains, os_):
        sl = slice(h * hd, (h + 1) * hd)
        on = o * lax.rsqrt(jnp.mean(o * o, axis=-1, keepdims=True) + NORM_EPS) * nw_ref[...]
        zh = z_ref[bi, :, sl].astype(F32)
        y_ref[bi, :, sl] = (on * (zh * _sigmoid(zh))).astype(y_ref.dtype)


def _gdn(qkv3, z3, ab3, cw, alog, dtb, nw, mask16, hmask16, bb):
    b, t, wd = qkv3.shape
    c = CHUNK
    full = lambda arr: pl.BlockSpec(arr.shape, lambda i, j: (0,) * arr.ndim)
    return pl.pallas_call(
        _gdn_body,
        grid=(b // bb, t // c),
        in_specs=[
            pl.BlockSpec((bb, c, wd), lambda i, j: (i, j, 0)),
            pl.BlockSpec((bb, c, GDN_WIDTH), lambda i, j: (i, j, 0)),
            pl.BlockSpec((bb, c, AB_PAD), lambda i, j: (i, j, 0)),
            full(cw), full(alog), full(dtb), full(nw), full(mask16), full(hmask16),
        ],
        out_specs=pl.BlockSpec((bb, c, GDN_WIDTH), lambda i, j: (i, j, 0)),
        out_shape=jax.ShapeDtypeStruct((b, t, GDN_WIDTH), BF16),
        scratch_shapes=[
            pltpu.VMEM((bb, SUBLANES, wd), F32),
            pltpu.VMEM((bb, GDN_HEADS, GDN_HEAD_DIM, GDN_HEAD_DIM), F32),
        ],
        compiler_params=pltpu.CompilerParams(dimension_semantics=("arbitrary", "arbitrary"),
                                             vmem_limit_bytes=VMEM_LIMIT),
        name="gdn",
    )(qkv3, z3, ab3, cw, alog, dtb, nw, mask16, hmask16)


def _mixffn_body(x_ref, ya_ref, yb_ref, gate_ref, rp_ref, gp_ref, wo_ref, g_ref, up_ref, cw_ref, down_ref, fg_ref,
                 o_ref, prev_ref, *, col_block, apply_final):
    tm, d = x_ref.shape
    hidden = down_ref.shape[0]

    @pl.when(pl.program_id(1) == 0)
    def _():
        prev_ref[...] = jnp.zeros_like(prev_ref)

    ya = jnp.dot(ya_ref[...], rp_ref[...], preferred_element_type=F32)
    yb = jnp.dot(yb_ref[...], gp_ref[...], preferred_element_type=F32)
    mixed = _sigmoid(gate_ref[:, :d].astype(F32)) * ya + _sigmoid(gate_ref[:, d:].astype(F32)) * yb
    x = x_ref[...] + jnp.dot(mixed.astype(BF16), wo_ref[...], preferred_element_type=F32)

    ms = jnp.mean(x * x, axis=-1, keepdims=True)
    xn = (x * lax.rsqrt(ms + NORM_EPS) * g_ref[...]).astype(BF16)

    def up_cols(lo):
        return jnp.dot(xn, up_ref[:, lo:lo + col_block], preferred_element_type=F32)

    def conv_cols(h, lo):
        hp = jnp.concatenate([prev_ref[:, lo:lo + col_block], h], axis=0)
        out = h * cw_ref[FFN_CONV - 1:FFN_CONV, lo:lo + col_block]
        for i in range(FFN_CONV - 1):
            s = SUBLANES - (FFN_CONV - 1) + i
            out = out + hp[s:s + tm, :] * cw_ref[i:i + 1, lo:lo + col_block]
        prev_ref[:, lo:lo + col_block] = h[tm - SUBLANES:tm, :]
        return out

    nblk = hidden // col_block
    acc = x
    hs = (up_cols(0), up_cols(hidden))
    for j in range(nblk):
        lo = j * col_block
        nxt = (up_cols(lo + col_block), up_cols(hidden + lo + col_block)) if j + 1 < nblk else None
        hg = conv_cols(hs[0], lo)
        hu = conv_cols(hs[1], hidden + lo)
        act = (hg * _sigmoid(hg) * hu).astype(BF16)
        acc = acc + jnp.dot(act, down_ref[lo:lo + col_block, :], preferred_element_type=F32)
        hs = nxt
    if apply_final:
        ms2 = jnp.mean(acc * acc, axis=-1, keepdims=True)
        acc = acc * lax.rsqrt(ms2 + NORM_EPS) * fg_ref[...]
    o_ref[...] = acc


def _mixffn(x3, ya3, yb3, gate3, rp, gp, wo, gain, up, cw, down, final_g, tm, apply_final):
    b, t, d = x3.shape
    hidden = down.shape[0]
    full = lambda arr: pl.BlockSpec(arr.shape, lambda i, j: (0, 0))
    single = lambda arr: pl.BlockSpec(arr.shape, lambda i, j: (0, 0), pipeline_mode=pl.Buffered(1))
    tile = lambda arr: pl.BlockSpec((None, tm, arr.shape[-1]), lambda i, j: (i, j, 0))
    body = functools.partial(_mixffn_body, col_block=MXU_TILE, apply_final=apply_final)
    return pl.pallas_call(
        body,
        grid=(b, t // tm),
        in_specs=[tile(x3), tile(ya3), tile(yb3), tile(gate3), single(rp), single(gp), single(wo),
                  full(gain), single(up), full(cw), single(down), full(final_g)],
        out_specs=pl.BlockSpec((None, tm, d), lambda i, j: (i, j, 0)),
        out_shape=jax.ShapeDtypeStruct((b, t, d), F32),
        scratch_shapes=[pltpu.VMEM((SUBLANES, 2 * hidden), F32)],
        compiler_params=pltpu.CompilerParams(dimension_semantics=("arbitrary", "arbitrary"),
                                             vmem_limit_bytes=VMEM_LIMIT),
        name="mixffn",
    )(x3, ya3, yb3, gate3, rp, gp, wo, gain, up, cw, down, final_g)


def _row(vec):
    return vec.reshape(1, -1).astype(F32)


def _pad_lanes(vec, offset=0):
    out = jnp.zeros((1, LANES), F32)
    return lax.dynamic_update_slice(out, vec.reshape(1, -1).astype(F32), (0, offset))


def kernel(x, norm1_g, w_in, rwkv_mu, rwkv_w0, rwkv_w2, rwkv_a0, rwkv_a2, rwkv_g2, rwkv_k_k, rwkv_k_a,
           rwkv_r_k, rwkv_ln_w, rwkv_ln_b, rwkv_proj, gdn_conv_w, gdn_a_log, gdn_dt_bias, gdn_norm_w,
           gdn_proj, w_out, norm2_g, ffn_up, ffn_conv_w, ffn_down, final_g):
    b, t, d = x.shape
    n = b * t
    depth = norm1_g.shape[0]
    tm = 256
    bb = next(c for c in (4, 2, 1) if b % c == 0)

    blk = jnp.arange(MXU_TILE) // CHUNK
    maskf = (blk[:, None] == blk[None, :]).astype(F32)
    mask16 = maskf.astype(BF16)
    hmask16 = (jnp.arange(GDN_HEADS * CHUNK)[:, None] // CHUNK == jnp.arange(GDN_WIDTH)[None, :] // GDN_HEAD_DIM).astype(BF16)

    n_ab = 2 * GDN_HEADS
    c_ab = RWKV_SHIFT_WIDTH + 4 * GDN_WIDTH
    widths = (RWKV_SHIFT_WIDTH, 3 * GDN_WIDTH, GDN_WIDTH, AB_PAD, 2 * d)
    dtypes = (BF16, BF16, BF16, F32, BF16)

    for l in range(depth):
        wl = w_in[l]
        w_packed = jnp.concatenate(
            [wl[:, :c_ab + n_ab], jnp.zeros((d, AB_PAD - n_ab), wl.dtype), wl[:, c_ab + n_ab:]], axis=1).astype(BF16)
        p_rwkv, qkv, z, ab, gate = _inproj(x.reshape(n, d), _row(norm1_g[l]), w_packed, widths, dtypes, tm)

        rwkv_params = (_row(rwkv_mu[l]), _row(rwkv_w0[l]), rwkv_w2[l].astype(BF16), _row(rwkv_a0[l]),
                       rwkv_a2[l].astype(BF16), rwkv_g2[l].astype(BF16), _row(rwkv_k_k[l]), _row(rwkv_k_a[l]),
                       _row(rwkv_r_k[l]), _row(rwkv_ln_w[l]), _row(rwkv_ln_b[l]))
        y_a = _rwkv(p_rwkv.reshape(b, t, -1), rwkv_params, mask16, maskf, bb)

        y_b = _gdn(qkv.reshape(b, t, -1), z.reshape(b, t, -1), ab.reshape(b, t, -1), gdn_conv_w[l].astype(F32),
                   _pad_lanes(gdn_a_log[l]), _pad_lanes(gdn_dt_bias[l]), _row(gdn_norm_w[l]), mask16, hmask16, bb)

        x = _mixffn(x, y_a, y_b, gate.reshape(b, t, -1), rwkv_proj[l].astype(BF16), gdn_proj[l].astype(BF16),
                    w_out[l].astype(BF16), _row(norm2_g[l]), ffn_up[l].astype(BF16), ffn_conv_w[l].astype(F32),
                    ffn_down[l].astype(BF16), _row(final_g), tm, apply_final=(l == depth - 1))
    return x
```

```python
import functools

import jax
import jax.numpy as jnp
from jax import lax
from jax.experimental import pallas as pl
from jax.experimental.pallas import tpu as pltpu

F32 = jnp.float32
BF16 = jnp.bfloat16

CHUNK = 64
RWKV_HEADS = 8
RWKV_HEAD_DIM = 64
RWKV_WIDTH = RWKV_HEADS * RWKV_HEAD_DIM
DECAY_LORA = 64
ICLR_LORA = 64
GATE_LORA = 128
GDN_HEADS = 4
GDN_HEAD_DIM = 128
GDN_WIDTH = GDN_HEADS * GDN_HEAD_DIM
GDN_CONV = 4
FFN_CONV = 3
NORM_EPS = 1e-6
L2_EPS = 1e-6
RWKV_GN_EPS = 64e-5
RWKV_SHIFT_WIDTH = 3 * RWKV_WIDTH + DECAY_LORA + ICLR_LORA + GATE_LORA
LANES = 128
SUBLANES = 8
MXU_TILE = 256
AB_PAD = LANES
GROUP = MXU_TILE // CHUNK
RWKV_GROUPS = RWKV_WIDTH // MXU_TILE

VMEM_LIMIT = 52 * 1024 * 1024


def _sigmoid(x):
    return 1.0 / (1.0 + jnp.exp(-x))


def _softplus(x):
    return jnp.maximum(x, 0.0) + jnp.log(1.0 + jnp.exp(-jnp.abs(x)))


def _dot(a, b):
    return jnp.dot(a.astype(BF16), b.astype(BF16), preferred_element_type=F32)


def _dot_nt(a, b):
    return lax.dot_general(a.astype(BF16), b.astype(BF16), (((1,), (1,)), ((), ())), preferred_element_type=F32)


def _dot_tn(a, b):
    return lax.dot_general(a.astype(BF16), b.astype(BF16), (((0,), (0,)), ((), ())), preferred_element_type=F32)


def _bdiag(x, mask16):
    reps = MXU_TILE // x.shape[0]
    return jnp.concatenate([x.astype(BF16)] * reps, axis=0) * mask16


def _cat_masks():
    ri = lax.broadcasted_iota(jnp.int32, (CHUNK, MXU_TILE), 0)
    ci = lax.broadcasted_iota(jnp.int32, (CHUNK, MXU_TILE), 1) % CHUNK
    return ri > ci, ri >= ci, (ri == ci).astype(F32)


def _unit_lower_inverse_cat(ms, eye_cat, mask16):
    ts = [eye_cat + m for m in ms]
    mks = [_dot(m, _bdiag(m, mask16)) for m in ms]
    levels = CHUNK.bit_length() - 1
    for _ in range(levels - 2):
        boths = [_dot(jnp.concatenate([mk, t], axis=0), _bdiag(mk, mask16)) for mk, t in zip(mks, ts)]
        mks = [both[:CHUNK] for both in boths]
        ts = [t + both[CHUNK:] for t, both in zip(ts, boths)]
    return [t + _dot(t, _bdiag(mk, mask16)) for t, mk in zip(ts, mks)]


def _inproj_body(x_ref, g_ref, w_ref, o_rwkv, o_qkv, o_z, o_ab, o_gate, wg_ref):
    ab_start = o_rwkv.shape[-1] + o_qkv.shape[-1] + o_z.shape[-1]
    gate_start = w_ref.shape[-1] - o_gate.shape[-1]

    @pl.when(pl.program_id(0) == 0)
    def _():
        wg_ref[...] = w_ref[:, gate_start:]

    x = x_ref[...]
    ms = jnp.mean(x * x, axis=-1, keepdims=True)
    u = (x * lax.rsqrt(ms + NORM_EPS) * g_ref[...]).astype(BF16)
    off = 0
    for ref in (o_rwkv, o_qkv, o_z):
        width = ref.shape[-1]
        ref[...] = jnp.dot(u, w_ref[:, off:off + width], preferred_element_type=F32).astype(ref.dtype)
        off += width
    o_ab[...] = jnp.dot(u, w_ref[:, ab_start:ab_start + o_ab.shape[-1]], preferred_element_type=F32)
    o_gate[...] = jnp.dot(u, wg_ref[...], preferred_element_type=F32).astype(o_gate.dtype)


def _inproj(x2, gain, w16, widths, dtypes, tm):
    n, d = x2.shape
    return pl.pallas_call(
        _inproj_body,
        grid=(n // tm,),
        in_specs=[
            pl.BlockSpec((tm, d), lambda i: (i, 0)),
            pl.BlockSpec((1, d), lambda i: (0, 0)),
            pl.BlockSpec(w16.shape, lambda i: (0, 0), pipeline_mode=pl.Buffered(1)),
        ],
        out_specs=[pl.BlockSpec((tm, w), lambda i: (i, 0)) for w in widths],
        out_shape=[jax.ShapeDtypeStruct((n, w), dt) for w, dt in zip(widths, dtypes)],
        scratch_shapes=[pltpu.VMEM((d, widths[-1]), BF16)],
        compiler_params=pltpu.CompilerParams(dimension_semantics=("arbitrary",), vmem_limit_bytes=VMEM_LIMIT),
        name="inproj",
    )(x2, gain, w16)


def _rwkv_body(p_ref, mu_ref, w0_ref, w2_ref, a0_ref, a2_ref, g2_ref, kk_ref, ka_ref, rk_ref, lnw_ref,
               lnb_ref, m16_ref, mf_ref, y_ref, prev_ref, s_ref):
    bb, c, wd = p_ref.shape
    rows = bb * c
    hd = RWKV_HEAD_DIM

    @pl.when(pl.program_id(1) == 0)
    def _():
        prev_ref[...] = jnp.zeros_like(prev_ref)
        s_ref[...] = jnp.zeros_like(s_ref)

    row = lax.broadcasted_iota(jnp.int32, (c, 1), 0)
    ps, shifts = [], []
    for bi in range(bb):
        pb_ = p_ref[bi].astype(F32)
        shifts.append(jnp.where(row == 0, prev_ref[bi, 0:1, :], pltpu.roll(pb_, 1, 0)))
        prev_ref[bi, 0:1, :] = pb_[c - 1:c, :]
        ps.append(pb_)
    p = jnp.concatenate(ps, axis=0)
    pm = p + (jnp.concatenate(shifts, axis=0) - p) * mu_ref[...]

    w = RWKV_WIDTH
    r = pm[:, 0:w]
    k = pm[:, w:2 * w]
    v = pm[:, 2 * w:3 * w]
    o0 = 3 * w
    wl = pm[:, o0:o0 + DECAY_LORA]
    al = pm[:, o0 + DECAY_LORA:o0 + DECAY_LORA + ICLR_LORA]
    gl = pm[:, o0 + DECAY_LORA + ICLR_LORA:o0 + DECAY_LORA + ICLR_LORA + GATE_LORA]

    w_log = -_softplus(-(w0_ref[...] + _dot(jnp.tanh(wl), w2_ref[...]))) - 0.5
    lw = (-jnp.exp(w_log)).astype(BF16).astype(F32)
    a = _sigmoid(a0_ref[...] + _dot(al, a2_ref[...]))
    g = _dot(_sigmoid(gl), g2_ref[...])

    mask16 = m16_ref[...]
    maskf = mf_ref[...]

    def head_sums(*xs):
        stacked = jnp.concatenate([x[:, q * MXU_TILE:(q + 1) * MXU_TILE] for x in xs for q in range(RWKV_GROUPS)], axis=0)
        res = jnp.dot(stacked.astype(BF16), mask16, preferred_element_type=F32)
        outs = []
        for i in range(len(xs)):
            base = i * RWKV_GROUPS * rows
            outs.append(jnp.concatenate([res[base + q * rows:base + (q + 1) * rows] for q in range(RWKV_GROUPS)], axis=1))
        return outs

    kx = k * kk_ref[...]
    k2 = k * (1.0 + (a - 1.0) * ka_ref[...])
    kx_ss, rk_sum = head_sums(kx * kx, r * k2 * rk_ref[...])
    kk = kx * lax.rsqrt(kx_ss + L2_EPS)

    ri = lax.broadcasted_iota(jnp.int32, (rows, rows), 0)
    ci = lax.broadcasted_iota(jnp.int32, (rows, rows), 1)
    tril = ((ri >= ci) & (ri // c == ci // c)).astype(F32).astype(BF16)
    cum = jnp.dot(tril, lw.astype(BF16), preferred_element_type=F32)
    tot = jnp.concatenate([jnp.broadcast_to(cum[(bi + 1) * c - 1:(bi + 1) * c, :], (c, w)) for bi in range(bb)], axis=0)
    e_pos = jnp.exp(cum)
    e_neg = jnp.exp(-cum)
    e_prev = jnp.exp(cum - lw)
    e_rest = jnp.exp(tot - cum)
    w_tot = jnp.exp(tot)

    b_vec = kk * a
    a_t = (-kk * e_prev).astype(BF16)
    r_t = (r * e_pos).astype(BF16)
    b_t = (b_vec * e_neg).astype(BF16)
    k_t = (k2 * e_neg).astype(BF16)
    b_h = (b_vec * e_rest).astype(BF16)
    k_h = (k2 * e_rest).astype(BF16)
    v16 = v.astype(BF16)

    strict_cat, incl_cat, eye_cat = _cat_masks()
    mask2 = jnp.concatenate([strict_cat, incl_cat], axis=0)

    chains = [(bi, q) for bi in range(bb) for q in range(RWKV_GROUPS)]
    rs_of = lambda bi: slice(bi * c, (bi + 1) * c)
    gs_of = lambda q: slice(q * MXU_TILE, (q + 1) * MXU_TILE)
    s_bds = [s_ref[bi, q] for bi, q in chains]
    pxs = []
    for (bi, q), s_bd in zip(chains, s_bds):
        rs, gs = rs_of(bi), gs_of(q)
        l2 = jnp.concatenate([a_t[rs, gs], r_t[rs, gs]], axis=0)
        w3 = jnp.concatenate([_bdiag(b_t[rs, gs], mask16), _bdiag(k_t[rs, gs], mask16), s_bd.astype(BF16)], axis=0)
        pxs.append(_dot_nt(l2, w3))
    pbs = [jnp.where(mask2, px[:, 0:MXU_TILE], 0.0) for px in pxs]
    pks = [jnp.where(mask2, px[:, MXU_TILE:2 * MXU_TILE], 0.0) for px in pxs]
    xas = [px[:, 2 * MXU_TILE:] + _dot(pk, _bdiag(v16[rs_of(bi), gs_of(q)], mask16))
           for (bi, q), px, pk in zip(chains, pxs, pks)]
    tinvs = _unit_lower_inverse_cat([pb[:c] for pb in pbs], eye_cat, mask16)
    us = [_dot(tinv, _bdiag(xa[:c], mask16)) for tinv, xa in zip(tinvs, xas)]
    ys = [xa[c:] + _dot(pb[c:], _bdiag(u, mask16)) for xa, pb, u in zip(xas, pbs, us)]
    for (bi, q), s_bd, u in zip(chains, s_bds, us):
        rs, gs = rs_of(bi), gs_of(q)
        uv = jnp.concatenate([u.astype(BF16), v16[rs, gs]], axis=0)
        bk = jnp.concatenate([b_h[rs, gs], k_h[rs, gs]], axis=0)
        s_ref[bi, q] = s_bd * w_tot[bi * c:bi * c + 1, gs] + _dot_tn(uv, bk) * maskf
    y_rows = [jnp.concatenate(ys[bi * RWKV_GROUPS:(bi + 1) * RWKV_GROUPS], axis=1) for bi in range(bb)]

    y = jnp.concatenate(y_rows, axis=0)
    inv_hd = 1.0 / hd
    (mean,) = head_sums(y)
    yc = y - mean * inv_hd
    (var,) = head_sums(yc * yc)
    yn = yc * lax.rsqrt(var * inv_hd + RWKV_GN_EPS) * lnw_ref[...] + lnb_ref[...]
    out = (yn + rk_sum * v) * g
    for bi in range(bb):
        y_ref[bi] = out[bi * c:(bi + 1) * c, :].astype(y_ref.dtype)


def _rwkv(p3, params, mask16, maskf, bb):
    b, t, wd = p3.shape
    c = CHUNK
    full = lambda arr: pl.BlockSpec(arr.shape, lambda i, j: (0,) * arr.ndim)
    return pl.pallas_call(
        _rwkv_body,
        grid=(b // bb, t // c),
        in_specs=[pl.BlockSpec((bb, c, wd), lambda i, j: (i, j, 0))] + [full(a) for a in params]
        + [full(mask16), full(maskf)],
        out_specs=pl.BlockSpec((bb, c, RWKV_WIDTH), lambda i, j: (i, j, 0)),
        out_shape=jax.ShapeDtypeStruct((b, t, RWKV_WIDTH), BF16),
        scratch_shapes=[
            pltpu.VMEM((bb, SUBLANES, wd), F32),
            pltpu.VMEM((bb, RWKV_GROUPS, MXU_TILE, MXU_TILE), F32),
        ],
        compiler_params=pltpu.CompilerParams(dimension_semantics=("arbitrary", "arbitrary"),
                                             vmem_limit_bytes=VMEM_LIMIT),
        name="rwkv",
    )(p3, *params, mask16, maskf)


def _gdn_body(qkv_ref, z_ref, ab_ref, cw_ref, alog_ref, dtb_ref, nw_ref, m16_ref, hm16_ref, y_ref, prev_ref, s_ref):
    bb, c, _ = qkv_ref.shape
    rows = bb * c
    hd = GDN_HEAD_DIM
    nh = GDN_HEADS

    @pl.when(pl.program_id(1) == 0)
    def _():
        prev_ref[...] = jnp.zeros_like(prev_ref)
        s_ref[...] = jnp.zeros_like(s_ref)

    convs = []
    for bi in range(bb):
        x = qkv_ref[bi].astype(F32)
        xp = jnp.concatenate([prev_ref[bi], x], axis=0)
        conv = x * cw_ref[GDN_CONV - 1:GDN_CONV, :]
        for i in range(GDN_CONV - 1):
            lo = SUBLANES - (GDN_CONV - 1) + i
            conv = conv + xp[lo:lo + c, :] * cw_ref[i:i + 1, :]
        prev_ref[bi] = x[c - SUBLANES:c, :]
        convs.append(conv)
    conv = jnp.concatenate(convs, axis=0)
    qkv = conv * _sigmoid(conv)
    w = GDN_WIDTH

    ab = jnp.concatenate([ab_ref[bi] for bi in range(bb)], axis=0)
    g_all = -jnp.exp(alog_ref[...]) * _softplus(ab + dtb_ref[...])
    beta_all = _sigmoid(ab)

    ri = lax.broadcasted_iota(jnp.int32, (rows, rows), 0)
    ci = lax.broadcasted_iota(jnp.int32, (rows, rows), 1)
    tril16 = ((ri >= ci) & (ri // c == ci // c)).astype(F32).astype(BF16)
    g_hi = g_all.astype(BF16)
    rem = g_all - g_hi.astype(F32)
    g_mid = rem.astype(BF16)
    g_lo = (rem - g_mid.astype(F32)).astype(BF16)
    gcp = jnp.dot(tril16, jnp.concatenate([g_hi, g_mid, g_lo], axis=1), preferred_element_type=F32)
    gc_all = gcp[:, 0:LANES] + gcp[:, LANES:2 * LANES] + gcp[:, 2 * LANES:3 * LANES]
    eg_all = jnp.exp(gc_all)

    mask16 = m16_ref[...]
    hmask16 = hm16_ref[...]
    strict_cat, incl_cat, eye_cat = _cat_masks()
    lane_head = lax.broadcasted_iota(jnp.int32, (c, MXU_TILE), 1) // CHUNK

    qs, ks, kbs = [], [], []
    for h in range(nh):
        qh = qkv[:, h * hd:(h + 1) * hd]
        kh = qkv[:, w + h * hd:w + (h + 1) * hd]
        qs.append(qh * lax.rsqrt(jnp.sum(qh * qh, axis=-1, keepdims=True) + L2_EPS) * (hd ** -0.5))
        kh = kh * lax.rsqrt(jnp.sum(kh * kh, axis=-1, keepdims=True) + L2_EPS)
        ks.append(kh)
        kbs.append(kh * beta_all[:, nh + h:nh + h + 1])
    k_all = jnp.concatenate(ks, axis=1).astype(BF16)
    kb_all = jnp.concatenate(kbs, axis=1).astype(BF16)
    q_all = jnp.concatenate(qs, axis=1).astype(BF16)

    rs_of = lambda bi: slice(bi * c, (bi + 1) * c)
    gc_bs = [gc_all[rs_of(bi)] for bi in range(bb)]
    glasts = [gc_b[c - 1:c, :] for gc_b in gc_bs]
    erests = [jnp.exp(glast - gc_b) for glast, gc_b in zip(glasts, gc_bs)]
    elasts = [jnp.exp(glast) for glast in glasts]

    decays = []
    for gc_b in gc_bs:
        f_cat = jnp.zeros((c, MXU_TILE), F32)
        for h in range(nh):
            f_cat = jnp.where(lane_head == h, gc_b[:, h:h + 1], f_cat)
        r_cat = jnp.sum(f_cat * eye_cat, axis=0, keepdims=True)
        decays.append(jnp.where(incl_cat, jnp.exp(jnp.where(incl_cat, f_cat - r_cat, 0.0)), 0.0))

    kqs = [_dot_nt(jnp.concatenate([kb_all[rs_of(bi)], q_all[rs_of(bi)]], axis=0),
                   jnp.concatenate([k_all[rs_of(bi)]] * nh, axis=0) * hmask16) for bi in range(bb)]
    attns = [kq[c:] * decay for kq, decay in zip(kqs, decays)]
    tinvs = _unit_lower_inverse_cat([-jnp.where(strict_cat, kq[:c] * decay, 0.0) for kq, decay in zip(kqs, decays)],
                                    eye_cat, mask16)

    chains = [(bi, h) for bi in range(bb) for h in range(nh)]
    tl_of = lambda h: slice(h * c, (h + 1) * c)
    egs = [eg_all[rs_of(bi), h:h + 1] for bi, h in chains]
    uws = []
    for (bi, h), eg in zip(chains, egs):
        rs = rs_of(bi)
        vh = qkv[rs, 2 * w + h * hd:2 * w + (h + 1) * hd]
        beta = beta_all[rs, nh + h:nh + h + 1]
        uws.append(_dot(tinvs[bi][:, tl_of(h)], jnp.concatenate([vh * beta, kbs[h][rs] * eg], axis=1)))
    ss = [s_ref[bi, h] for bi, h in chains]
    wqs = [_dot(jnp.concatenate([uw[:, hd:], qs[h][rs_of(bi)] * eg], axis=0), s)
           for (bi, h), uw, eg, s in zip(chains, uws, egs, ss)]
    v_news = [uw[:, :hd] - wq[:c] for uw, wq in zip(uws, wqs)]
    os_ = [wq[c:] + _dot(attns[bi][:, tl_of(h)], v_new) for (bi, h), wq, v_new in zip(chains, wqs, v_news)]
    for (bi, h), s, v_new in zip(chains, ss, v_news):
        s_ref[bi, h] = s * elasts[bi][:, h:h + 1] + _dot_tn(ks[h][rs_of(bi)] * erests[bi][:, h:h + 1], v_new)
    for (bi, h), o in zip(chains, os_):
        sl = slice(h * hd, (h + 1) * hd)
        on = o * lax.rsqrt(jnp.mean(o * o, axis=-1, keepdims=True) + NORM_EPS) * nw_ref[...]
        zh = z_ref[bi, :, sl].astype(F32)
        y_ref[bi, :, sl] = (on * (zh * _sigmoid(zh))).astype(y_ref.dtype)


def _gdn(qkv3, z3, ab3, cw, alog, dtb, nw, mask16, hmask16, bb):
    b, t, wd = qkv3.shape
    c = CHUNK
    full = lambda arr: pl.BlockSpec(arr.shape, lambda i, j: (0,) * arr.ndim)
    return pl.pallas_call(
        _gdn_body,
        grid=(b // bb, t // c),
        in_specs=[
            pl.BlockSpec((bb, c, wd), lambda i, j: (i, j, 0)),
            pl.BlockSpec((bb, c, GDN_WIDTH), lambda i, j: (i, j, 0)),
            pl.BlockSpec((bb, c, AB_PAD), lambda i, j: (i, j, 0)),
            full(cw), full(alog), full(dtb), full(nw), full(mask16), full(hmask16),
        ],
        out_specs=pl.BlockSpec((bb, c, GDN_WIDTH), lambda i, j: (i, j, 0)),
        out_shape=jax.ShapeDtypeStruct((b, t, GDN_WIDTH), BF16),
        scratch_shapes=[
            pltpu.VMEM((bb, SUBLANES, wd), F32),
            pltpu.VMEM((bb, GDN_HEADS, GDN_HEAD_DIM, GDN_HEAD_DIM), F32),
        ],
        compiler_params=pltpu.CompilerParams(dimension_semantics=("arbitrary", "arbitrary"),
                                             vmem_limit_bytes=VMEM_LIMIT),
        name="gdn",
    )(qkv3, z3, ab3, cw, alog, dtb, nw, mask16, hmask16)


def _mixffn_body(x_ref, ya_ref, yb_ref, gate_ref, rp_ref, gp_ref, wo_ref, g_ref, up_ref, cw_ref, down_ref, fg_ref,
                 o_ref, prev_ref, *, col_block, apply_final):
    tm, d = x_ref.shape
    hidden = down_ref.shape[0]

    @pl.when(pl.program_id(1) == 0)
    def _():
        prev_ref[...] = jnp.zeros_like(prev_ref)

    ya = jnp.dot(ya_ref[...], rp_ref[...], preferred_element_type=F32)
    yb = jnp.dot(yb_ref[...], gp_ref[...], preferred_element_type=F32)
    mixed = _sigmoid(gate_ref[:, :d].astype(F32)) * ya + _sigmoid(gate_ref[:, d:].astype(F32)) * yb
    x = x_ref[...] + jnp.dot(mixed.astype(BF16), wo_ref[...], preferred_element_type=F32)

    ms = jnp.mean(x * x, axis=-1, keepdims=True)
    xn = (x * lax.rsqrt(ms + NORM_EPS) * g_ref[...]).astype(BF16)

    def up_cols(lo):
        return jnp.dot(xn, up_ref[:, lo:lo + col_block], preferred_element_type=F32)

    def conv_cols(h, lo):
        hp = jnp.concatenate([prev_ref[:, lo:lo + col_block], h], axis=0)
        out = h * cw_ref[FFN_CONV - 1:FFN_CONV, lo:lo + col_block]
        for i in range(FFN_CONV - 1):
            s = SUBLANES - (FFN_CONV - 1) + i
            out = out + hp[s:s + tm, :] * cw_ref[i:i + 1, lo:lo + col_block]
        prev_ref[:, lo:lo + col_block] = h[tm - SUBLANES:tm, :]
        return out

    nblk = hidden // col_block
    acc = x
    hs = (up_cols(0), up_cols(hidden))
    for j in range(nblk):
        lo = j * col_block
        nxt = (up_cols(lo + col_block), up_cols(hidden + lo + col_block)) if j + 1 < nblk else None
        hg = conv_cols(hs[0], lo)
        hu = conv_cols(hs[1], hidden + lo)
        act = (hg * _sigmoid(hg) * hu).astype(BF16)
        acc = acc + jnp.dot(act, down_ref[lo:lo + col_block, :], preferred_element_type=F32)
        hs = nxt
    if apply_final:
        ms2 = jnp.mean(acc * acc, axis=-1, keepdims=True)
        acc = acc * lax.rsqrt(ms2 + NORM_EPS) * fg_ref[...]
    o_ref[...] = acc


def _mixffn(x3, ya3, yb3, gate3, rp, gp, wo, gain, up, cw, down, final_g, tm, apply_final):
    b, t, d = x3.shape
    hidden = down.shape[0]
    full = lambda arr: pl.BlockSpec(arr.shape, lambda i, j: (0, 0))
    single = lambda arr: pl.BlockSpec(arr.shape, lambda i, j: (0, 0), pipeline_mode=pl.Buffered(1))
    tile = lambda arr: pl.BlockSpec((None, tm, arr.shape[-1]), lambda i, j: (i, j, 0))
    body = functools.partial(_mixffn_body, col_block=MXU_TILE, apply_final=apply_final)
    return pl.pallas_call(
        body,
        grid=(b, t // tm),
        in_specs=[tile(x3), tile(ya3), tile(yb3), tile(gate3), single(rp), single(gp), single(wo),
                  full(gain), single(up), full(cw), single(down), full(final_g)],
        out_specs=pl.BlockSpec((None, tm, d), lambda i, j: (i, j, 0)),
        out_shape=jax.ShapeDtypeStruct((b, t, d), F32),
        scratch_shapes=[pltpu.VMEM((SUBLANES, 2 * hidden), F32)],
        compiler_params=pltpu.CompilerParams(dimension_semantics=("arbitrary", "arbitrary"),
                                             vmem_limit_bytes=VMEM_LIMIT),
        name="mixffn",
    )(x3, ya3, yb3, gate3, rp, gp, wo, gain, up, cw, down, final_g)


def _row(vec):
    return vec.reshape(1, -1).astype(F32)


def _pad_lanes(vec, offset=0):
    out = jnp.zeros((1, LANES), F32)
    return lax.dynamic_update_slice(out, vec.reshape(1, -1).astype(F32), (0, offset))


def kernel(x, norm1_g, w_in, rwkv_mu, rwkv_w0, rwkv_w2, rwkv_a0, rwkv_a2, rwkv_g2, rwkv_k_k, rwkv_k_a,
           rwkv_r_k, rwkv_ln_w, rwkv_ln_b, rwkv_proj, gdn_conv_w, gdn_a_log, gdn_dt_bias, gdn_norm_w,
           gdn_proj, w_out, norm2_g, ffn_up, ffn_conv_w, ffn_down, final_g):
    b, t, d = x.shape
    n = b * t
    depth = norm1_g.shape[0]
    tm = 256
    tm_in = 512
    bb = next(c for c in (4, 2, 1) if b % c == 0)

    blk = jnp.arange(MXU_TILE) // CHUNK
    maskf = (blk[:, None] == blk[None, :]).astype(F32)
    mask16 = maskf.astype(BF16)
    hmask16 = (jnp.arange(GDN_HEADS * CHUNK)[:, None] // CHUNK == jnp.arange(GDN_WIDTH)[None, :] // GDN_HEAD_DIM).astype(BF16)

    widths = (RWKV_SHIFT_WIDTH, 3 * GDN_WIDTH, GDN_WIDTH, AB_PAD, 2 * d)
    dtypes = (BF16, BF16, BF16, F32, BF16)

    for l in range(depth):
        p_rwkv, qkv, z, ab, gate = _inproj(x.reshape(n, d), _row(norm1_g[l]), w_in[l].astype(BF16), widths, dtypes,
                                           tm_in)

        rwkv_params = (_row(rwkv_mu[l]), _row(rwkv_w0[l]), rwkv_w2[l].astype(BF16), _row(rwkv_a0[l]),
                       rwkv_a2[l].astype(BF16), rwkv_g2[l].astype(BF16), _row(rwkv_k_k[l]), _row(rwkv_k_a[l]),
                       _row(rwkv_r_k[l]), _row(rwkv_ln_w[l]), _row(rwkv_ln_b[l]))
        y_a = _rwkv(p_rwkv.reshape(b, t, -1), rwkv_params, mask16, maskf, bb)

        y_b = _gdn(qkv.reshape(b, t, -1), z.reshape(b, t, -1), ab.reshape(b, t, -1), gdn_conv_w[l].astype(F32),
                   _pad_lanes(gdn_a_log[l]), _pad_lanes(gdn_dt_bias[l]), _row(gdn_norm_w[l]), mask16, hmask16, bb)

        x = _mixffn(x, y_a, y_b, gate.reshape(b, t, -1), rwkv_proj[l].astype(BF16), gdn_proj[l].astype(BF16),
                    w_out[l].astype(BF16), _row(norm2_g[l]), ffn_up[l].astype(BF16), ffn_conv_w[l].astype(F32),
                    ffn_down[l].astype(BF16), _row(final_g), tm, apply_final=(l == depth - 1))
    return x
```

```python
import functools

import jax
import jax.numpy as jnp
from jax import lax
from jax.experimental import pallas as pl
from jax.experimental.pallas import tpu as pltpu

F32 = jnp.float32
BF16 = jnp.bfloat16

CHUNK = 64
RWKV_HEADS = 8
RWKV_HEAD_DIM = 64
RWKV_WIDTH = RWKV_HEADS * RWKV_HEAD_DIM
DECAY_LORA = 64
ICLR_LORA = 64
GATE_LORA = 128
GDN_HEADS = 4
GDN_HEAD_DIM = 128
GDN_WIDTH = GDN_HEADS * GDN_HEAD_DIM
GDN_CONV = 4
FFN_CONV = 3
NORM_EPS = 1e-6
L2_EPS = 1e-6
RWKV_GN_EPS = 64e-5
RWKV_SHIFT_WIDTH = 3 * RWKV_WIDTH + DECAY_LORA + ICLR_LORA + GATE_LORA
LANES = 128
SUBLANES = 8
MXU_TILE = 256
AB_PAD = LANES
GROUP = MXU_TILE // CHUNK
RWKV_GROUPS = RWKV_WIDTH // MXU_TILE

VMEM_LIMIT = 52 * 1024 * 1024


def _sigmoid(x):
    return 1.0 / (1.0 + jnp.exp(-x))


def _softplus(x):
    return jnp.maximum(x, 0.0) + jnp.log(1.0 + jnp.exp(-jnp.abs(x)))


def _dot(a, b):
    return jnp.dot(a.astype(BF16), b.astype(BF16), preferred_element_type=F32)


def _dot_nt(a, b):
    return lax.dot_general(a.astype(BF16), b.astype(BF16), (((1,), (1,)), ((), ())), preferred_element_type=F32)


def _dot_tn(a, b):
    return lax.dot_general(a.astype(BF16), b.astype(BF16), (((0,), (0,)), ((), ())), preferred_element_type=F32)


def _bdiag(x, mask16):
    reps = MXU_TILE // x.shape[0]
    return jnp.concatenate([x.astype(BF16)] * reps, axis=0) * mask16


def _cat_masks():
    ri = lax.broadcasted_iota(jnp.int32, (CHUNK, MXU_TILE), 0)
    ci = lax.broadcasted_iota(jnp.int32, (CHUNK, MXU_TILE), 1) % CHUNK
    return ri > ci, ri >= ci, (ri == ci).astype(F32)


def _unit_lower_inverse_cat(ms, eye_cat, mask16):
    ts = [eye_cat + m for m in ms]
    mks = [_dot(m, _bdiag(m, mask16)) for m in ms]
    levels = CHUNK.bit_length() - 1
    for _ in range(levels - 2):
        boths = [_dot(jnp.concatenate([mk, t], axis=0), _bdiag(mk, mask16)) for mk, t in zip(mks, ts)]
        mks = [both[:CHUNK] for both in boths]
        ts = [t + both[CHUNK:] for t, both in zip(ts, boths)]
    return [t + _dot(t, _bdiag(mk, mask16)) for t, mk in zip(ts, mks)]


def _inproj_body(x_ref, g_ref, w_ref, o_rwkv, o_qkv, o_z, o_ab, o_gate, wg_ref):
    ab_start = o_rwkv.shape[-1] + o_qkv.shape[-1] + o_z.shape[-1]
    gate_start = w_ref.shape[-1] - o_gate.shape[-1]

    @pl.when(pl.program_id(0) == 0)
    def _():
        wg_ref[...] = w_ref[:, gate_start:]

    x = x_ref[...]
    ms = jnp.mean(x * x, axis=-1, keepdims=True)
    u = (x * lax.rsqrt(ms + NORM_EPS) * g_ref[...]).astype(BF16)
    off = 0
    for ref in (o_rwkv, o_qkv, o_z):
        width = ref.shape[-1]
        ref[...] = jnp.dot(u, w_ref[:, off:off + width], preferred_element_type=F32).astype(ref.dtype)
        off += width
    o_ab[...] = jnp.dot(u, w_ref[:, ab_start:ab_start + o_ab.shape[-1]], preferred_element_type=F32)
    o_gate[...] = jnp.dot(u, wg_ref[...], preferred_element_type=F32).astype(o_gate.dtype)


def _inproj(x2, gain, w16, widths, dtypes, tm):
    n, d = x2.shape
    return pl.pallas_call(
        _inproj_body,
        grid=(n // tm,),
        in_specs=[
            pl.BlockSpec((tm, d), lambda i: (i, 0)),
            pl.BlockSpec((1, d), lambda i: (0, 0)),
            pl.BlockSpec(w16.shape, lambda i: (0, 0), pipeline_mode=pl.Buffered(1)),
        ],
        out_specs=[pl.BlockSpec((tm, w), lambda i: (i, 0)) for w in widths],
        out_shape=[jax.ShapeDtypeStruct((n, w), dt) for w, dt in zip(widths, dtypes)],
        scratch_shapes=[pltpu.VMEM((d, widths[-1]), BF16)],
        compiler_params=pltpu.CompilerParams(dimension_semantics=("arbitrary",), vmem_limit_bytes=VMEM_LIMIT),
        name="inproj",
    )(x2, gain, w16)


def _rwkv_body(p_ref, mu_ref, w0_ref, w2_ref, a0_ref, a2_ref, g2_ref, kk_ref, ka_ref, rk_ref, lnw_ref,
               lnb_ref, m16_ref, mf_ref, y_ref, prev_ref, s_ref):
    bb, c, wd = p_ref.shape
    rows = bb * c
    hd = RWKV_HEAD_DIM

    @pl.when(pl.program_id(1) == 0)
    def _():
        prev_ref[...] = jnp.zeros_like(prev_ref)
        s_ref[...] = jnp.zeros_like(s_ref)

    row = lax.broadcasted_iota(jnp.int32, (c, 1), 0)
    ps, shifts = [], []
    for bi in range(bb):
        pb_ = p_ref[bi].astype(F32)
        shifts.append(jnp.where(row == 0, prev_ref[bi, 0:1, :], pltpu.roll(pb_, 1, 0)))
        prev_ref[bi, 0:1, :] = pb_[c - 1:c, :]
        ps.append(pb_)
    p = jnp.concatenate(ps, axis=0)
    pm = p + (jnp.concatenate(shifts, axis=0) - p) * mu_ref[...]

    w = RWKV_WIDTH
    r = pm[:, 0:w]
    k = pm[:, w:2 * w]
    v = pm[:, 2 * w:3 * w]
    o0 = 3 * w
    wl = pm[:, o0:o0 + DECAY_LORA]
    al = pm[:, o0 + DECAY_LORA:o0 + DECAY_LORA + ICLR_LORA]
    gl = pm[:, o0 + DECAY_LORA + ICLR_LORA:o0 + DECAY_LORA + ICLR_LORA + GATE_LORA]

    w_log = -_softplus(-(w0_ref[...] + _dot(jnp.tanh(wl), w2_ref[...]))) - 0.5
    lw = (-jnp.exp(w_log)).astype(BF16).astype(F32)
    a = _sigmoid(a0_ref[...] + _dot(al, a2_ref[...]))
    g = _dot(_sigmoid(gl), g2_ref[...])

    mask16 = m16_ref[...]
    maskf = mf_ref[...]

    def head_sums(*xs):
        stacked = jnp.concatenate([x[:, q * MXU_TILE:(q + 1) * MXU_TILE] for x in xs for q in range(RWKV_GROUPS)], axis=0)
        res = jnp.dot(stacked.astype(BF16), mask16, preferred_element_type=F32)
        outs = []
        for i in range(len(xs)):
            base = i * RWKV_GROUPS * rows
            outs.append(jnp.concatenate([res[base + q * rows:base + (q + 1) * rows] for q in range(RWKV_GROUPS)], axis=1))
        return outs

    kx = k * kk_ref[...]
    k2 = k * (1.0 + (a - 1.0) * ka_ref[...])
    kx_ss, rk_sum = head_sums(kx * kx, r * k2 * rk_ref[...])
    kk = kx * lax.rsqrt(kx_ss + L2_EPS)

    ri = lax.broadcasted_iota(jnp.int32, (rows, rows), 0)
    ci = lax.broadcasted_iota(jnp.int32, (rows, rows), 1)
    tril = ((ri >= ci) & (ri // c == ci // c)).astype(F32).astype(BF16)
    cum = jnp.dot(tril, lw.astype(BF16), preferred_element_type=F32)
    tot = jnp.concatenate([jnp.broadcast_to(cum[(bi + 1) * c - 1:(bi + 1) * c, :], (c, w)) for bi in range(bb)], axis=0)
    e_pos = jnp.exp(cum)
    e_neg = jnp.exp(-cum)
    e_prev = jnp.exp(cum - lw)
    e_rest = jnp.exp(tot - cum)
    w_tot = jnp.exp(tot)

    b_vec = kk * a
    a_t = (-kk * e_prev).astype(BF16)
    r_t = (r * e_pos).astype(BF16)
    b_t = (b_vec * e_neg).astype(BF16)
    k_t = (k2 * e_neg).astype(BF16)
    b_h = (b_vec * e_rest).astype(BF16)
    k_h = (k2 * e_rest).astype(BF16)
    v16 = v.astype(BF16)

    strict_cat, incl_cat, eye_cat = _cat_masks()
    mask2 = jnp.concatenate([strict_cat, incl_cat], axis=0)

    chains = [(bi, q) for bi in range(bb) for q in range(RWKV_GROUPS)]
    rs_of = lambda bi: slice(bi * c, (bi + 1) * c)
    gs_of = lambda q: slice(q * MXU_TILE, (q + 1) * MXU_TILE)
    s_bds = [s_ref[bi, q] for bi, q in chains]
    pxs = []
    for (bi, q), s_bd in zip(chains, s_bds):
        rs, gs = rs_of(bi), gs_of(q)
        l2 = jnp.concatenate([a_t[rs, gs], r_t[rs, gs]], axis=0)
        w3 = jnp.concatenate([_bdiag(b_t[rs, gs], mask16), _bdiag(k_t[rs, gs], mask16), s_bd.astype(BF16)], axis=0)
        pxs.append(_dot_nt(l2, w3))
    pbs = [jnp.where(mask2, px[:, 0:MXU_TILE], 0.0) for px in pxs]
    pks = [jnp.where(mask2, px[:, MXU_TILE:2 * MXU_TILE], 0.0) for px in pxs]
    xas = [px[:, 2 * MXU_TILE:] + _dot(pk, _bdiag(v16[rs_of(bi), gs_of(q)], mask16))
           for (bi, q), px, pk in zip(chains, pxs, pks)]
    tinvs = _unit_lower_inverse_cat([pb[:c] for pb in pbs], eye_cat, mask16)
    us = [_dot(tinv, _bdiag(xa[:c], mask16)) for tinv, xa in zip(tinvs, xas)]
    ys = [xa[c:] + _dot(pb[c:], _bdiag(u, mask16)) for xa, pb, u in zip(xas, pbs, us)]
    for (bi, q), s_bd, u in zip(chains, s_bds, us):
        rs, gs = rs_of(bi), gs_of(q)
        uv = jnp.concatenate([u.astype(BF16), v16[rs, gs]], axis=0)
        bk = jnp.concatenate([b_h[rs, gs], k_h[rs, gs]], axis=0)
        s_ref[bi, q] = s_bd * w_tot[bi * c:bi * c + 1, gs] + _dot_tn(uv, bk) * maskf
    y_rows = [jnp.concatenate(ys[bi * RWKV_GROUPS:(bi + 1) * RWKV_GROUPS], axis=1) for bi in range(bb)]

    y = jnp.concatenate(y_rows, axis=0)
    inv_hd = 1.0 / hd
    (mean,) = head_sums(y)
    yc = y - mean * inv_hd
    (var,) = head_sums(yc * yc)
    yn = yc * lax.rsqrt(var * inv_hd + RWKV_GN_EPS) * lnw_ref[...] + lnb_ref[...]
    out = (yn + rk_sum * v) * g
    for bi in range(bb):
        y_ref[bi] = out[bi * c:(bi + 1) * c, :].astype(y_ref.dtype)


def _rwkv(p3, params, mask16, maskf, bb):
    b, t, wd = p3.shape
    c = CHUNK
    full = lambda arr: pl.BlockSpec(arr.shape, lambda i, j: (0,) * arr.ndim)
    return pl.pallas_call(
        _rwkv_body,
        grid=(b // bb, t // c),
        in_specs=[pl.BlockSpec((bb, c, wd), lambda i, j: (i, j, 0))] + [full(a) for a in params]
        + [full(mask16), full(maskf)],
        out_specs=pl.BlockSpec((bb, c, RWKV_WIDTH), lambda i, j: (i, j, 0)),
        out_shape=jax.ShapeDtypeStruct((b, t, RWKV_WIDTH), BF16),
        scratch_shapes=[
            pltpu.VMEM((bb, SUBLANES, wd), F32),
            pltpu.VMEM((bb, RWKV_GROUPS, MXU_TILE, MXU_TILE), F32),
        ],
        compiler_params=pltpu.CompilerParams(dimension_semantics=("arbitrary", "arbitrary"),
                                             vmem_limit_bytes=VMEM_LIMIT),
        name="rwkv",
    )(p3, *params, mask16, maskf)


def _gdn_body(qkv_ref, z_ref, ab_ref, cw_ref, alog_ref, dtb_ref, nw_ref, m16_ref, hm16_ref, y_ref, prev_ref, s_ref):
    bb, c, _ = qkv_ref.shape
    rows = bb * c
    hd = GDN_HEAD_DIM
    nh = GDN_HEADS

    @pl.when(pl.program_id(1) == 0)
    def _():
        prev_ref[...] = jnp.zeros_like(prev_ref)
        s_ref[...] = jnp.zeros_like(s_ref)

    convs = []
    for bi in range(bb):
        x = qkv_ref[bi].astype(F32)
        xp = jnp.concatenate([prev_ref[bi], x], axis=0)
        conv = x * cw_ref[GDN_CONV - 1:GDN_CONV, :]
        for i in range(GDN_CONV - 1):
            lo = SUBLANES - (GDN_CONV - 1) + i
            conv = conv + xp[lo:lo + c, :] * cw_ref[i:i + 1, :]
        prev_ref[bi] = x[c - SUBLANES:c, :]
        convs.append(conv)
    conv = jnp.concatenate(convs, axis=0)
    qkv = conv * _sigmoid(conv)
    w = GDN_WIDTH

    ab = jnp.concatenate([ab_ref[bi] for bi in range(bb)], axis=0)
    g_all = -jnp.exp(alog_ref[...]) * _softplus(ab + dtb_ref[...])
    beta_all = _sigmoid(ab)

    ri = lax.broadcasted_iota(jnp.int32, (rows, rows), 0)
    ci = lax.broadcasted_iota(jnp.int32, (rows, rows), 1)
    tril16 = ((ri >= ci) & (ri // c == ci // c)).astype(F32).astype(BF16)
    g_hi = g_all.astype(BF16)
    rem = g_all - g_hi.astype(F32)
    g_mid = rem.astype(BF16)
    g_lo = (rem - g_mid.astype(F32)).astype(BF16)
    gcp = jnp.dot(tril16, jnp.concatenate([g_hi, g_mid, g_lo], axis=1), preferred_element_type=F32)
    gc_all = gcp[:, 0:LANES] + gcp[:, LANES:2 * LANES] + gcp[:, 2 * LANES:3 * LANES]
    eg_all = jnp.exp(gc_all)

    mask16 = m16_ref[...]
    hmask16 = hm16_ref[...]
    strict_cat, incl_cat, eye_cat = _cat_masks()
    lane_head = lax.broadcasted_iota(jnp.int32, (c, MXU_TILE), 1) // CHUNK

    qs, ks, kbs = [], [], []
    for h in range(nh):
        qh = qkv[:, h * hd:(h + 1) * hd]
        kh = qkv[:, w + h * hd:w + (h + 1) * hd]
        qs.append(qh * lax.rsqrt(jnp.sum(qh * qh, axis=-1, keepdims=True) + L2_EPS) * (hd ** -0.5))
        kh = kh * lax.rsqrt(jnp.sum(kh * kh, axis=-1, keepdims=True) + L2_EPS)
        ks.append(kh)
        kbs.append(kh * beta_all[:, nh + h:nh + h + 1])
    k_all = jnp.concatenate(ks, axis=1).astype(BF16)
    kb_all = jnp.concatenate(kbs, axis=1).astype(BF16)
    q_all = jnp.concatenate(qs, axis=1).astype(BF16)

    rs_of = lambda bi: slice(bi * c, (bi + 1) * c)
    gc_bs = [gc_all[rs_of(bi)] for bi in range(bb)]
    glasts = [gc_b[c - 1:c, :] for gc_b in gc_bs]
    erests = [jnp.exp(glast - gc_b) for glast, gc_b in zip(glasts, gc_bs)]
    elasts = [jnp.exp(glast) for glast in glasts]

    decays = []
    for gc_b in gc_bs:
        f_cat = jnp.zeros((c, MXU_TILE), F32)
        for h in range(nh):
            f_cat = jnp.where(lane_head == h, gc_b[:, h:h + 1], f_cat)
        r_cat = jnp.sum(f_cat * eye_cat, axis=0, keepdims=True)
        decays.append(jnp.where(incl_cat, jnp.exp(jnp.where(incl_cat, f_cat - r_cat, 0.0)), 0.0))

    kqs = [_dot_nt(jnp.concatenate([kb_all[rs_of(bi)], q_all[rs_of(bi)]], axis=0),
                   jnp.concatenate([k_all[rs_of(bi)]] * nh, axis=0) * hmask16) for bi in range(bb)]
    attns = [kq[c:] * decay for kq, decay in zip(kqs, decays)]
    tinvs = _unit_lower_inverse_cat([-jnp.where(strict_cat, kq[:c] * decay, 0.0) for kq, decay in zip(kqs, decays)],
                                    eye_cat, mask16)

    chains = [(bi, h) for bi in range(bb) for h in range(nh)]
    tl_of = lambda h: slice(h * c, (h + 1) * c)
    egs = [eg_all[rs_of(bi), h:h + 1] for bi, h in chains]
    uws = []
    for (bi, h), eg in zip(chains, egs):
        rs = rs_of(bi)
        vh = qkv[rs, 2 * w + h * hd:2 * w + (h + 1) * hd]
        beta = beta_all[rs, nh + h:nh + h + 1]
        uws.append(_dot(tinvs[bi][:, tl_of(h)], jnp.concatenate([vh * beta, kbs[h][rs] * eg], axis=1)))
    ss = [s_ref[bi, h] for bi, h in chains]
    wqs = [_dot(jnp.concatenate([uw[:, hd:], qs[h][rs_of(bi)] * eg], axis=0), s)
           for (bi, h), uw, eg, s in zip(chains, uws, egs, ss)]
    v_news = [uw[:, :hd] - wq[:c] for uw, wq in zip(uws, wqs)]
    os_ = [wq[c:] + _dot(attns[bi][:, tl_of(h)], v_new) for (bi, h), wq, v_new in zip(chains, wqs, v_news)]
    for (bi, h), s, v_new in zip(chains, ss, v_news):
        s_ref[bi, h] = s * elasts[bi][:, h:h + 1] + _dot_tn(ks[h][rs_of(bi)] * erests[bi][:, h:h + 1], v_new)
    for (bi, h), o in zip(chains, os_):
        sl = slice(h * hd, (h + 1) * hd)
        on = o * lax.rsqrt(jnp.mean(o * o, axis=-1, keepdims=True) + NORM_EPS) * nw_ref[...]
        zh = z_ref[bi, :, sl].astype(F32)
        y_ref[bi, :, sl] = (on * (zh * _sigmoid(zh))).astype(y_ref.dtype)


def _gdn(qkv3, z3, ab3, cw, alog, dtb, nw, mask16, hmask16, bb):
    b, t, wd = qkv3.shape
    c = CHUNK
    full = lambda arr: pl.BlockSpec(arr.shape, lambda i, j: (0,) * arr.ndim)
    return pl.pallas_call(
        _gdn_body,
        grid=(b // bb, t // c),
        in_specs=[
            pl.BlockSpec((bb, c, wd), lambda i, j: (i, j, 0)),
            pl.BlockSpec((bb, c, GDN_WIDTH), lambda i, j: (i, j, 0)),
            pl.BlockSpec((bb, c, AB_PAD), lambda i, j: (i, j, 0)),
            full(cw), full(alog), full(dtb), full(nw), full(mask16), full(hmask16),
        ],
        out_specs=pl.BlockSpec((bb, c, GDN_WIDTH), lambda i, j: (i, j, 0)),
        out_shape=jax.ShapeDtypeStruct((b, t, GDN_WIDTH), BF16),
        scratch_shapes=[
            pltpu.VMEM((bb, SUBLANES, wd), F32),
            pltpu.VMEM((bb, GDN_HEADS, GDN_HEAD_DIM, GDN_HEAD_DIM), F32),
        ],
        compiler_params=pltpu.CompilerParams(dimension_semantics=("arbitrary", "arbitrary"),
                                             vmem_limit_bytes=VMEM_LIMIT),
        name="gdn",
    )(qkv3, z3, ab3, cw, alog, dtb, nw, mask16, hmask16)


def _mixffn_body(x_ref, ya_ref, yb_ref, gate_ref, rp_ref, gp_ref, wo_ref, g_ref, up_ref, cw_ref, down_ref, fg_ref,
                 o_ref, prev_ref, *, col_block, apply_final):
    tm, d = x_ref.shape
    hidden = down_ref.shape[0]

    @pl.when(pl.program_id(1) == 0)
    def _():
        prev_ref[...] = jnp.zeros_like(prev_ref)

    ya = jnp.dot(ya_ref[...], rp_ref[...], preferred_element_type=F32)
    yb = jnp.dot(yb_ref[...], gp_ref[...], preferred_element_type=F32)
    mixed = _sigmoid(gate_ref[:, :d].astype(F32)) * ya + _sigmoid(gate_ref[:, d:].astype(F32)) * yb
    x = x_ref[...] + jnp.dot(mixed.astype(BF16), wo_ref[...], preferred_element_type=F32)

    ms = jnp.mean(x * x, axis=-1, keepdims=True)
    xn = (x * lax.rsqrt(ms + NORM_EPS) * g_ref[...]).astype(BF16)

    def up_cols(lo):
        return jnp.dot(xn, up_ref[:, lo:lo + col_block], preferred_element_type=F32)

    def conv_cols(h, lo):
        hp = jnp.concatenate([prev_ref[:, lo:lo + col_block], h], axis=0)
        out = h * cw_ref[FFN_CONV - 1:FFN_CONV, lo:lo + col_block]
        for i in range(FFN_CONV - 1):
            s = SUBLANES - (FFN_CONV - 1) + i
            out = out + hp[s:s + tm, :] * cw_ref[i:i + 1, lo:lo + col_block]
        prev_ref[:, lo:lo + col_block] = h[tm - SUBLANES:tm, :]
        return out

    nblk = hidden // col_block
    acc = x
    hs = (up_cols(0), up_cols(hidden))
    for j in range(nblk):
        lo = j * col_block
        nxt = (up_cols(lo + col_block), up_cols(hidden + lo + col_block)) if j + 1 < nblk else None
        hg = conv_cols(hs[0], lo)
        hu = conv_cols(hs[1], hidden + lo)
        act = (hg * _sigmoid(hg) * hu).astype(BF16)
        acc = acc + jnp.dot(act, down_ref[lo:lo + col_block, :], preferred_element_type=F32)
        hs = nxt
    if apply_final:
        ms2 = jnp.mean(acc * acc, axis=-1, keepdims=True)
        acc = acc * lax.rsqrt(ms2 + NORM_EPS) * fg_ref[...]
    o_ref[...] = acc


def _mixffn(x3, ya3, yb3, gate3, rp, gp, wo, gain, up, cw, down, final_g, tm, apply_final):
    b, t, d = x3.shape
    hidden = down.shape[0]
    full = lambda arr: pl.BlockSpec(arr.shape, lambda i, j: (0, 0))
    single = lambda arr: pl.BlockSpec(arr.shape, lambda i, j: (0, 0), pipeline_mode=pl.Buffered(1))
    tile = lambda arr: pl.BlockSpec((None, tm, arr.shape[-1]), lambda i, j: (i, j, 0))
    body = functools.partial(_mixffn_body, col_block=MXU_TILE, apply_final=apply_final)
    return pl.pallas_call(
        body,
        grid=(b, t // tm),
        in_specs=[tile(x3), tile(ya3), tile(yb3), tile(gate3), single(rp), single(gp), single(wo),
                  full(gain), single(up), full(cw), single(down), full(final_g)],
        out_specs=pl.BlockSpec((None, tm, d), lambda i, j: (i, j, 0)),
        out_shape=jax.ShapeDtypeStruct((b, t, d), F32),
        scratch_shapes=[pltpu.VMEM((SUBLANES, 2 * hidden), F32)],
        compiler_params=pltpu.CompilerParams(dimension_semantics=("arbitrary", "arbitrary"),
                                             vmem_limit_bytes=VMEM_LIMIT),
        name="mixffn",
    )(x3, ya3, yb3, gate3, rp, gp, wo, gain, up, cw, down, final_g)


def _row(vec):
    return vec.reshape(1, -1).astype(F32)


def _pad_lanes(vec, offset=0):
    out = jnp.zeros((1, LANES), F32)
    return lax.dynamic_update_slice(out, vec.reshape(1, -1).astype(F32), (0, offset))


def kernel(x, norm1_g, w_in, rwkv_mu, rwkv_w0, rwkv_w2, rwkv_a0, rwkv_a2, rwkv_g2, rwkv_k_k, rwkv_k_a,
           rwkv_r_k, rwkv_ln_w, rwkv_ln_b, rwkv_proj, gdn_conv_w, gdn_a_log, gdn_dt_bias, gdn_norm_w,
           gdn_proj, w_out, norm2_g, ffn_up, ffn_conv_w, ffn_down, final_g):
    b, t, d = x.shape
    n = b * t
    depth = norm1_g.shape[0]
    tm = 256
    tm_in = 512
    bb = next(c for c in (8, 4, 2, 1) if b % c == 0)

    blk = jnp.arange(MXU_TILE) // CHUNK
    maskf = (blk[:, None] == blk[None, :]).astype(F32)
    mask16 = maskf.astype(BF16)
    hmask16 = (jnp.arange(GDN_HEADS * CHUNK)[:, None] // CHUNK == jnp.arange(GDN_WIDTH)[None, :] // GDN_HEAD_DIM).astype(BF16)

    widths = (RWKV_SHIFT_WIDTH, 3 * GDN_WIDTH, GDN_WIDTH, AB_PAD, 2 * d)
    dtypes = (BF16, BF16, BF16, F32, BF16)

    for l in range(depth):
        p_rwkv, qkv, z, ab, gate = _inproj(x.reshape(n, d), _row(norm1_g[l]), w_in[l].astype(BF16), widths, dtypes,
                                           tm_in)

        rwkv_params = (_row(rwkv_mu[l]), _row(rwkv_w0[l]), rwkv_w2[l].astype(BF16), _row(rwkv_a0[l]),
                       rwkv_a2[l].astype(BF16), rwkv_g2[l].astype(BF16), _row(rwkv_k_k[l]), _row(rwkv_k_a[l]),
                       _row(rwkv_r_k[l]), _row(rwkv_ln_w[l]), _row(rwkv_ln_b[l]))
        y_a = _rwkv(p_rwkv.reshape(b, t, -1), rwkv_params, mask16, maskf, bb)

        y_b = _gdn(qkv.reshape(b, t, -1), z.reshape(b, t, -1), ab.reshape(b, t, -1), gdn_conv_w[l].astype(F32),
                   _pad_lanes(gdn_a_log[l]), _pad_lanes(gdn_dt_bias[l]), _row(gdn_norm_w[l]), mask16, hmask16, bb)

        x = _mixffn(x, y_a, y_b, gate.reshape(b, t, -1), rwkv_proj[l].astype(BF16), gdn_proj[l].astype(BF16),
                    w_out[l].astype(BF16), _row(norm2_g[l]), ffn_up[l].astype(BF16), ffn_conv_w[l].astype(F32),
                    ffn_down[l].astype(BF16), _row(final_g), tm, apply_final=(l == depth - 1))
    return x
```

```python
import functools

import jax
import jax.numpy as jnp
from jax import lax
from jax.experimental import pallas as pl
from jax.experimental.pallas import tpu as pltpu

F32 = jnp.float32
BF16 = jnp.bfloat16

CHUNK = 64
RWKV_HEADS = 8
RWKV_HEAD_DIM = 64
RWKV_WIDTH = RWKV_HEADS * RWKV_HEAD_DIM
DECAY_LORA = 64
ICLR_LORA = 64
GATE_LORA = 128
GDN_HEADS = 4
GDN_HEAD_DIM = 128
GDN_WIDTH = GDN_HEADS * GDN_HEAD_DIM
GDN_CONV = 4
FFN_CONV = 3
NORM_EPS = 1e-6
L2_EPS = 1e-6
RWKV_GN_EPS = 64e-5
RWKV_SHIFT_WIDTH = 3 * RWKV_WIDTH + DECAY_LORA + ICLR_LORA + GATE_LORA
LANES = 128
SUBLANES = 8
MXU_TILE = 256
AB_PAD = LANES
GROUP = MXU_TILE // CHUNK
RWKV_GROUPS = RWKV_WIDTH // MXU_TILE

VMEM_LIMIT = 52 * 1024 * 1024


def _sigmoid(x):
    return 1.0 / (1.0 + jnp.exp(-x))


def _softplus(x):
    return jnp.maximum(x, 0.0) + jnp.log(1.0 + jnp.exp(-jnp.abs(x)))


def _dot(a, b):
    return jnp.dot(a.astype(BF16), b.astype(BF16), preferred_element_type=F32)


def _dot_nt(a, b):
    return lax.dot_general(a.astype(BF16), b.astype(BF16), (((1,), (1,)), ((), ())), preferred_element_type=F32)


def _dot_tn(a, b):
    return lax.dot_general(a.astype(BF16), b.astype(BF16), (((0,), (0,)), ((), ())), preferred_element_type=F32)


def _bdiag(x, mask16):
    reps = MXU_TILE // x.shape[0]
    return jnp.concatenate([x.astype(BF16)] * reps, axis=0) * mask16


def _cat_masks():
    ri = lax.broadcasted_iota(jnp.int32, (CHUNK, MXU_TILE), 0)
    ci = lax.broadcasted_iota(jnp.int32, (CHUNK, MXU_TILE), 1) % CHUNK
    return ri > ci, ri >= ci, (ri == ci).astype(F32)


def _unit_lower_inverse_cat(ms, eye_cat, mask16):
    ts = [eye_cat + m for m in ms]
    mks = [_dot(m, _bdiag(m, mask16)) for m in ms]
    levels = CHUNK.bit_length() - 1
    for _ in range(levels - 2):
        boths = [_dot(jnp.concatenate([mk, t], axis=0), _bdiag(mk, mask16)) for mk, t in zip(mks, ts)]
        mks = [both[:CHUNK] for both in boths]
        ts = [t + both[CHUNK:] for t, both in zip(ts, boths)]
    return [t + _dot(t, _bdiag(mk, mask16)) for t, mk in zip(ts, mks)]


def _inproj_body(x_ref, g_ref, w_ref, cw_ref, o_rwkv, o_qkv, o_z, o_ab, o_gate, wg_ref, tail_ref, *, tiles_per_seq):
    ab_start = o_rwkv.shape[-1] + o_qkv.shape[-1] + o_z.shape[-1]
    gate_start = w_ref.shape[-1] - o_gate.shape[-1]

    @pl.when(pl.program_id(0) == 0)
    def _():
        wg_ref[...] = w_ref[:, gate_start:]

    x = x_ref[...]
    ms = jnp.mean(x * x, axis=-1, keepdims=True)
    u = (x * lax.rsqrt(ms + NORM_EPS) * g_ref[...]).astype(BF16)
    tm = x_ref.shape[0]
    w_r, w_q, w_z = o_rwkv.shape[-1], o_qkv.shape[-1], o_z.shape[-1]

    @pl.when(pl.program_id(0) % tiles_per_seq == 0)
    def _():
        tail_ref[...] = jnp.zeros_like(tail_ref)

    def conv_silu(h, lo):
        cols = slice(lo, lo + MXU_TILE)
        hp = jnp.concatenate([tail_ref[:, cols], h], axis=0)
        conv = h * cw_ref[GDN_CONV - 1:GDN_CONV, cols]
        for i in range(GDN_CONV - 1):
            s = SUBLANES - (GDN_CONV - 1) + i
            conv = conv + hp[s:s + tm, :] * cw_ref[i:i + 1, cols]
        tail_ref[:, cols] = h[tm - SUBLANES:tm, :]
        o_qkv[:, cols] = (conv * _sigmoid(conv)).astype(o_qkv.dtype)

    def project(ref, wref, out_lo, w_lo):
        ref[:, out_lo:out_lo + MXU_TILE] = jnp.dot(
            u, wref[:, w_lo:w_lo + MXU_TILE], preferred_element_type=F32).astype(ref.dtype)

    plain = [functools.partial(project, o_rwkv, w_ref, k, k) for k in range(0, w_r, MXU_TILE)]
    plain += [functools.partial(project, o_z, w_ref, k, w_r + w_q + k) for k in range(0, w_z, MXU_TILE)]
    plain += [functools.partial(project, o_gate, wg_ref, k, k) for k in range(0, o_gate.shape[-1], MXU_TILE)]
    qcol = lambda lo: jnp.dot(u, w_ref[:, w_r + lo:w_r + lo + MXU_TILE], preferred_element_type=F32)
    hq = qcol(0)
    for lo in range(0, w_q, MXU_TILE):
        nxt = qcol(lo + MXU_TILE) if lo + MXU_TILE < w_q else None
        for _ in range(2):
            plain.pop(0)()
        conv_silu(hq, lo)
        hq = nxt
    for emit in plain:
        emit()
    o_ab[...] = jnp.dot(u, w_ref[:, ab_start:ab_start + o_ab.shape[-1]], preferred_element_type=F32)


def _inproj(x2, gain, w16, cw, widths, dtypes, tm, tiles_per_seq):
    n, d = x2.shape
    return pl.pallas_call(
        functools.partial(_inproj_body, tiles_per_seq=tiles_per_seq),
        grid=(n // tm,),
        in_specs=[
            pl.BlockSpec((tm, d), lambda i: (i, 0)),
            pl.BlockSpec((1, d), lambda i: (0, 0)),
            pl.BlockSpec(w16.shape, lambda i: (0, 0), pipeline_mode=pl.Buffered(1)),
            pl.BlockSpec(cw.shape, lambda i: (0, 0)),
        ],
        out_specs=[pl.BlockSpec((tm, w), lambda i: (i, 0)) for w in widths],
        out_shape=[jax.ShapeDtypeStruct((n, w), dt) for w, dt in zip(widths, dtypes)],
        scratch_shapes=[pltpu.VMEM((d, widths[-1]), BF16), pltpu.VMEM((SUBLANES, widths[1]), F32)],
        compiler_params=pltpu.CompilerParams(dimension_semantics=("arbitrary",), vmem_limit_bytes=VMEM_LIMIT),
        name="inproj",
    )(x2, gain, w16, cw)


def _rwkv_body(p_ref, mu_ref, w0_ref, w2_ref, a0_ref, a2_ref, g2_ref, kk_ref, ka_ref, rk_ref, lnw_ref,
               lnb_ref, m16_ref, mf_ref, y_ref, prev_ref, s_ref):
    bb, c, wd = p_ref.shape
    rows = bb * c
    hd = RWKV_HEAD_DIM

    @pl.when(pl.program_id(1) == 0)
    def _():
        prev_ref[...] = jnp.zeros_like(prev_ref)
        s_ref[...] = jnp.zeros_like(s_ref)

    row = lax.broadcasted_iota(jnp.int32, (c, 1), 0)
    ps, shifts = [], []
    for bi in range(bb):
        pb_ = p_ref[bi].astype(F32)
        shifts.append(jnp.where(row == 0, prev_ref[bi, 0:1, :], pltpu.roll(pb_, 1, 0)))
        prev_ref[bi, 0:1, :] = pb_[c - 1:c, :]
        ps.append(pb_)
    p = jnp.concatenate(ps, axis=0)
    pm = p + (jnp.concatenate(shifts, axis=0) - p) * mu_ref[...]

    w = RWKV_WIDTH
    r = pm[:, 0:w]
    k = pm[:, w:2 * w]
    v = pm[:, 2 * w:3 * w]
    o0 = 3 * w
    wl = pm[:, o0:o0 + DECAY_LORA]
    al = pm[:, o0 + DECAY_LORA:o0 + DECAY_LORA + ICLR_LORA]
    gl = pm[:, o0 + DECAY_LORA + ICLR_LORA:o0 + DECAY_LORA + ICLR_LORA + GATE_LORA]

    w_log = -_softplus(-(w0_ref[...] + _dot(jnp.tanh(wl), w2_ref[...]))) - 0.5
    lw = (-jnp.exp(w_log)).astype(BF16).astype(F32)
    a = _sigmoid(a0_ref[...] + _dot(al, a2_ref[...]))
    g = _dot(_sigmoid(gl), g2_ref[...])

    mask16 = m16_ref[...]
    maskf = mf_ref[...]

    def head_sums(*xs):
        stacked = jnp.concatenate([x[:, q * MXU_TILE:(q + 1) * MXU_TILE] for x in xs for q in range(RWKV_GROUPS)], axis=0)
        res = jnp.dot(stacked.astype(BF16), mask16, preferred_element_type=F32)
        outs = []
        for i in range(len(xs)):
            base = i * RWKV_GROUPS * rows
            outs.append(jnp.concatenate([res[base + q * rows:base + (q + 1) * rows] for q in range(RWKV_GROUPS)], axis=1))
        return outs

    kx = k * kk_ref[...]
    k2 = k * (1.0 + (a - 1.0) * ka_ref[...])
    kx_ss, rk_sum = head_sums(kx * kx, r * k2 * rk_ref[...])
    kk = kx * lax.rsqrt(kx_ss + L2_EPS)

    ri = lax.broadcasted_iota(jnp.int32, (rows, rows), 0)
    ci = lax.broadcasted_iota(jnp.int32, (rows, rows), 1)
    tril = ((ri >= ci) & (ri // c == ci // c)).astype(F32).astype(BF16)
    cum = jnp.dot(tril, lw.astype(BF16), preferred_element_type=F32)
    tot = jnp.concatenate([jnp.broadcast_to(cum[(bi + 1) * c - 1:(bi + 1) * c, :], (c, w)) for bi in range(bb)], axis=0)
    e_pos = jnp.exp(cum)
    e_neg = jnp.exp(-cum)
    e_prev = jnp.exp(cum - lw)
    e_rest = jnp.exp(tot - cum)
    w_tot = jnp.exp(tot)

    b_vec = kk * a
    a_t = (-kk * e_prev).astype(BF16)
    r_t = (r * e_pos).astype(BF16)
    b_t = (b_vec * e_neg).astype(BF16)
    k_t = (k2 * e_neg).astype(BF16)
    b_h = (b_vec * e_rest).astype(BF16)
    k_h = (k2 * e_rest).astype(BF16)
    v16 = v.astype(BF16)

    strict_cat, incl_cat, eye_cat = _cat_masks()
    mask2 = jnp.concatenate([strict_cat, incl_cat], axis=0)

    chains = [(bi, q) for bi in range(bb) for q in range(RWKV_GROUPS)]
    rs_of = lambda bi: slice(bi * c, (bi + 1) * c)
    gs_of = lambda q: slice(q * MXU_TILE, (q + 1) * MXU_TILE)
    s_bds = [s_ref[bi, q] for bi, q in chains]
    pxs = []
    for (bi, q), s_bd in zip(chains, s_bds):
        rs, gs = rs_of(bi), gs_of(q)
        l2 = jnp.concatenate([a_t[rs, gs], r_t[rs, gs]], axis=0)
        w3 = jnp.concatenate([_bdiag(b_t[rs, gs], mask16), _bdiag(k_t[rs, gs], mask16), s_bd.astype(BF16)], axis=0)
        pxs.append(_dot_nt(l2, w3))
    pbs = [jnp.where(mask2, px[:, 0:MXU_TILE], 0.0) for px in pxs]
    pks = [jnp.where(mask2, px[:, MXU_TILE:2 * MXU_TILE], 0.0) for px in pxs]
    xas = [px[:, 2 * MXU_TILE:] + _dot(pk, _bdiag(v16[rs_of(bi), gs_of(q)], mask16))
           for (bi, q), px, pk in zip(chains, pxs, pks)]
    tinvs = _unit_lower_inverse_cat([pb[:c] for pb in pbs], eye_cat, mask16)
    us = [_dot(tinv, _bdiag(xa[:c], mask16)) for tinv, xa in zip(tinvs, xas)]
    ys = [xa[c:] + _dot(pb[c:], _bdiag(u, mask16)) for xa, pb, u in zip(xas, pbs, us)]
    for (bi, q), s_bd, u in zip(chains, s_bds, us):
        rs, gs = rs_of(bi), gs_of(q)
        uv = jnp.concatenate([u.astype(BF16), v16[rs, gs]], axis=0)
        bk = jnp.concatenate([b_h[rs, gs], k_h[rs, gs]], axis=0)
        s_ref[bi, q] = s_bd * w_tot[bi * c:bi * c + 1, gs] + _dot_tn(uv, bk) * maskf
    y_rows = [jnp.concatenate(ys[bi * RWKV_GROUPS:(bi + 1) * RWKV_GROUPS], axis=1) for bi in range(bb)]

    y = jnp.concatenate(y_rows, axis=0)
    inv_hd = 1.0 / hd
    (mean,) = head_sums(y)
    yc = y - mean * inv_hd
    (var,) = head_sums(yc * yc)
    yn = yc * lax.rsqrt(var * inv_hd + RWKV_GN_EPS) * lnw_ref[...] + lnb_ref[...]
    out = (yn + rk_sum * v) * g
    for bi in range(bb):
        y_ref[bi] = out[bi * c:(bi + 1) * c, :].astype(y_ref.dtype)


def _rwkv(p3, params, mask16, maskf, bb):
    b, t, wd = p3.shape
    c = CHUNK
    full = lambda arr: pl.BlockSpec(arr.shape, lambda i, j: (0,) * arr.ndim)
    return pl.pallas_call(
        _rwkv_body,
        grid=(b // bb, t // c),
        in_specs=[pl.BlockSpec((bb, c, wd), lambda i, j: (i, j, 0))] + [full(a) for a in params]
        + [full(mask16), full(maskf)],
        out_specs=pl.BlockSpec((bb, c, RWKV_WIDTH), lambda i, j: (i, j, 0)),
        out_shape=jax.ShapeDtypeStruct((b, t, RWKV_WIDTH), BF16),
        scratch_shapes=[
            pltpu.VMEM((bb, SUBLANES, wd), F32),
            pltpu.VMEM((bb, RWKV_GROUPS, MXU_TILE, MXU_TILE), F32),
        ],
        compiler_params=pltpu.CompilerParams(dimension_semantics=("arbitrary", "arbitrary"),
                                             vmem_limit_bytes=VMEM_LIMIT),
        name="rwkv",
    )(p3, *params, mask16, maskf)


def _gdn_body(qkv_ref, z_ref, ab_ref, alog_ref, dtb_ref, nw_ref, m16_ref, hm16_ref, y_ref, s_ref):
    bb, c, _ = qkv_ref.shape
    rows = bb * c
    hd = GDN_HEAD_DIM
    nh = GDN_HEADS

    @pl.when(pl.program_id(1) == 0)
    def _():
        s_ref[...] = jnp.zeros_like(s_ref)

    qkv = jnp.concatenate([qkv_ref[bi] for bi in range(bb)], axis=0).astype(F32)
    w = GDN_WIDTH

    ab = jnp.concatenate([ab_ref[bi] for bi in range(bb)], axis=0)
    g_all = -jnp.exp(alog_ref[...]) * _softplus(ab + dtb_ref[...])
    beta_all = _sigmoid(ab)

    ri = lax.broadcasted_iota(jnp.int32, (rows, rows), 0)
    ci = lax.broadcasted_iota(jnp.int32, (rows, rows), 1)
    tril16 = ((ri >= ci) & (ri // c == ci // c)).astype(F32).astype(BF16)
    g_hi = g_all.astype(BF16)
    rem = g_all - g_hi.astype(F32)
    g_mid = rem.astype(BF16)
    g_lo = (rem - g_mid.astype(F32)).astype(BF16)
    gcp = jnp.dot(tril16, jnp.concatenate([g_hi, g_mid, g_lo], axis=1), preferred_element_type=F32)
    gc_all = gcp[:, 0:LANES] + gcp[:, LANES:2 * LANES] + gcp[:, 2 * LANES:3 * LANES]
    eg_all = jnp.exp(gc_all)

    mask16 = m16_ref[...]
    hmask16 = hm16_ref[...]
    strict_cat, incl_cat, eye_cat = _cat_masks()
    lane_head = lax.broadcasted_iota(jnp.int32, (c, MXU_TILE), 1) // CHUNK

    qs, ks, kbs = [], [], []
    for h in range(nh):
        qh = qkv[:, h * hd:(h + 1) * hd]
        kh = qkv[:, w + h * hd:w + (h + 1) * hd]
        qs.append(qh * lax.rsqrt(jnp.sum(qh * qh, axis=-1, keepdims=True) + L2_EPS) * (hd ** -0.5))
        kh = kh * lax.rsqrt(jnp.sum(kh * kh, axis=-1, keepdims=True) + L2_EPS)
        ks.append(kh)
        kbs.append(kh * beta_all[:, nh + h:nh + h + 1])
    k_all = jnp.concatenate(ks, axis=1).astype(BF16)
    kb_all = jnp.concatenate(kbs, axis=1).astype(BF16)
    q_all = jnp.concatenate(qs, axis=1).astype(BF16)

    rs_of = lambda bi: slice(bi * c, (bi + 1) * c)
    gc_bs = [gc_all[rs_of(bi)] for bi in range(bb)]
    glasts = [gc_b[c - 1:c, :] for gc_b in gc_bs]
    erests = [jnp.exp(glast - gc_b) for glast, gc_b in zip(glasts, gc_bs)]
    elasts = [jnp.exp(glast) for glast in glasts]

    decays = []
    for gc_b in gc_bs:
        f_cat = jnp.zeros((c, MXU_TILE), F32)
        for h in range(nh):
            f_cat = jnp.where(lane_head == h, gc_b[:, h:h + 1], f_cat)
        r_cat = jnp.sum(f_cat * eye_cat, axis=0, keepdims=True)
        decays.append(jnp.where(incl_cat, jnp.exp(jnp.where(incl_cat, f_cat - r_cat, 0.0)), 0.0))

    kqs = [_dot_nt(jnp.concatenate([kb_all[rs_of(bi)], q_all[rs_of(bi)]], axis=0),
                   jnp.concatenate([k_all[rs_of(bi)]] * nh, axis=0) * hmask16) for bi in range(bb)]
    attns = [kq[c:] * decay for kq, decay in zip(kqs, decays)]
    tinvs = _unit_lower_inverse_cat([-jnp.where(strict_cat, kq[:c] * decay, 0.0) for kq, decay in zip(kqs, decays)],
                                    eye_cat, mask16)

    chains = [(bi, h) for bi in range(bb) for h in range(nh)]
    tl_of = lambda h: slice(h * c, (h + 1) * c)
    egs = [eg_all[rs_of(bi), h:h + 1] for bi, h in chains]
    uws = []
    for (bi, h), eg in zip(chains, egs):
        rs = rs_of(bi)
        vh = qkv[rs, 2 * w + h * hd:2 * w + (h + 1) * hd]
        beta = beta_all[rs, nh + h:nh + h + 1]
        uws.append(_dot(tinvs[bi][:, tl_of(h)], jnp.concatenate([vh * beta, kbs[h][rs] * eg], axis=1)))
    ss = [s_ref[bi, h] for bi, h in chains]
    wqs = [_dot(jnp.concatenate([uw[:, hd:], qs[h][rs_of(bi)] * eg], axis=0), s)
           for (bi, h), uw, eg, s in zip(chains, uws, egs, ss)]
    v_news = [uw[:, :hd] - wq[:c] for uw, wq in zip(uws, wqs)]
    os_ = [wq[c:] + _dot(attns[bi][:, tl_of(h)], v_new) for (bi, h), wq, v_new in zip(chains, wqs, v_news)]
    for (bi, h), s, v_new in zip(chains, ss, v_news):
        s_ref[bi, h] = s * elasts[bi][:, h:h + 1] + _dot_tn(ks[h][rs_of(bi)] * erests[bi][:, h:h + 1], v_new)
    for (bi, h), o in zip(chains, os_):
        sl = slice(h * hd, (h + 1) * hd)
        on = o * lax.rsqrt(jnp.mean(o * o, axis=-1, keepdims=True) + NORM_EPS) * nw_ref[...]
        zh = z_ref[bi, :, sl].astype(F32)
        y_ref[bi, :, sl] = (on * (zh * _sigmoid(zh))).astype(y_ref.dtype)


def _gdn(qkv3, z3, ab3, alog, dtb, nw, mask16, hmask16, bb):
    b, t, wd = qkv3.shape
    c = CHUNK
    full = lambda arr: pl.BlockSpec(arr.shape, lambda i, j: (0,) * arr.ndim)
    return pl.pallas_call(
        _gdn_body,
        grid=(b // bb, t // c),
        in_specs=[
            pl.BlockSpec((bb, c, wd), lambda i, j: (i, j, 0)),
            pl.BlockSpec((bb, c, GDN_WIDTH), lambda i, j: (i, j, 0)),
            pl.BlockSpec((bb, c, AB_PAD), lambda i, j: (i, j, 0)),
            full(alog), full(dtb), full(nw), full(mask16), full(hmask16),
        ],
        out_specs=pl.BlockSpec((bb, c, GDN_WIDTH), lambda i, j: (i, j, 0)),
        out_shape=jax.ShapeDtypeStruct((b, t, GDN_WIDTH), BF16),
        scratch_shapes=[pltpu.VMEM((bb, GDN_HEADS, GDN_HEAD_DIM, GDN_HEAD_DIM), F32)],
        compiler_params=pltpu.CompilerParams(dimension_semantics=("arbitrary", "arbitrary"),
                                             vmem_limit_bytes=VMEM_LIMIT),
        name="gdn",
    )(qkv3, z3, ab3, alog, dtb, nw, mask16, hmask16)


def _mixffn_body(x_ref, ya_ref, yb_ref, gate_ref, rp_ref, gp_ref, wo_ref, g_ref, up_ref, cw_ref, down_ref, fg_ref,
                 o_ref, prev_ref, *, col_block, apply_final):
    tm, d = x_ref.shape
    hidden = down_ref.shape[0]

    @pl.when(pl.program_id(1) == 0)
    def _():
        prev_ref[...] = jnp.zeros_like(prev_ref)

    ya = jnp.dot(ya_ref[...], rp_ref[...], preferred_element_type=F32)
    yb = jnp.dot(yb_ref[...], gp_ref[...], preferred_element_type=F32)
    mixed = _sigmoid(gate_ref[:, :d].astype(F32)) * ya + _sigmoid(gate_ref[:, d:].astype(F32)) * yb
    x = x_ref[...] + jnp.dot(mixed.astype(BF16), wo_ref[...], preferred_element_type=F32)

    ms = jnp.mean(x * x, axis=-1, keepdims=True)
    xn = (x * lax.rsqrt(ms + NORM_EPS) * g_ref[...]).astype(BF16)

    def up_cols(lo):
        return jnp.dot(xn, up_ref[:, lo:lo + col_block], preferred_element_type=F32)

    def conv_cols(h, lo):
        hp = jnp.concatenate([prev_ref[:, lo:lo + col_block], h], axis=0)
        out = h * cw_ref[FFN_CONV - 1:FFN_CONV, lo:lo + col_block]
        for i in range(FFN_CONV - 1):
            s = SUBLANES - (FFN_CONV - 1) + i
            out = out + hp[s:s + tm, :] * cw_ref[i:i + 1, lo:lo + col_block]
        prev_ref[:, lo:lo + col_block] = h[tm - SUBLANES:tm, :]
        return out

    nblk = hidden // col_block
    acc = x
    hs = (up_cols(0), up_cols(hidden))
    for j in range(nblk):
        lo = j * col_block
        nxt = (up_cols(lo + col_block), up_cols(hidden + lo + col_block)) if j + 1 < nblk else None
        hg = conv_cols(hs[0], lo)
        hu = conv_cols(hs[1], hidden + lo)
        act = (hg * _sigmoid(hg) * hu).astype(BF16)
        acc = acc + jnp.dot(act, down_ref[lo:lo + col_block, :], preferred_element_type=F32)
        hs = nxt
    if apply_final:
        ms2 = jnp.mean(acc * acc, axis=-1, keepdims=True)
        acc = acc * lax.rsqrt(ms2 + NORM_EPS) * fg_ref[...]
    o_ref[...] = acc


def _mixffn(x3, ya3, yb3, gate3, rp, gp, wo, gain, up, cw, down, final_g, tm, apply_final):
    b, t, d = x3.shape
    hidden = down.shape[0]
    full = lambda arr: pl.BlockSpec(arr.shape, lambda i, j: (0, 0))
    single = lambda arr: pl.BlockSpec(arr.shape, lambda i, j: (0, 0), pipeline_mode=pl.Buffered(1))
    tile = lambda arr: pl.BlockSpec((None, tm, arr.shape[-1]), lambda i, j: (i, j, 0))
    body = functools.partial(_mixffn_body, col_block=MXU_TILE, apply_final=apply_final)
    return pl.pallas_call(
        body,
        grid=(b, t // tm),
        in_specs=[tile(x3), tile(ya3), tile(yb3), tile(gate3), single(rp), single(gp), single(wo),
                  full(gain), single(up), full(cw), single(down), full(final_g)],
        out_specs=pl.BlockSpec((None, tm, d), lambda i, j: (i, j, 0)),
        out_shape=jax.ShapeDtypeStruct((b, t, d), F32),
        scratch_shapes=[pltpu.VMEM((SUBLANES, 2 * hidden), F32)],
        compiler_params=pltpu.CompilerParams(dimension_semantics=("arbitrary", "arbitrary"),
                                             vmem_limit_bytes=VMEM_LIMIT),
        name="mixffn",
    )(x3, ya3, yb3, gate3, rp, gp, wo, gain, up, cw, down, final_g)


def _row(vec):
    return vec.reshape(1, -1).astype(F32)


def _pad_lanes(vec, offset=0):
    out = jnp.zeros((1, LANES), F32)
    return lax.dynamic_update_slice(out, vec.reshape(1, -1).astype(F32), (0, offset))


def kernel(x, norm1_g, w_in, rwkv_mu, rwkv_w0, rwkv_w2, rwkv_a0, rwkv_a2, rwkv_g2, rwkv_k_k, rwkv_k_a,
           rwkv_r_k, rwkv_ln_w, rwkv_ln_b, rwkv_proj, gdn_conv_w, gdn_a_log, gdn_dt_bias, gdn_norm_w,
           gdn_proj, w_out, norm2_g, ffn_up, ffn_conv_w, ffn_down, final_g):
    b, t, d = x.shape
    n = b * t
    depth = norm1_g.shape[0]
    tm = 256
    tm_in = 512
    bb = next(c for c in (8, 4, 2, 1) if b % c == 0)

    blk = jnp.arange(MXU_TILE) // CHUNK
    maskf = (blk[:, None] == blk[None, :]).astype(F32)
    mask16 = maskf.astype(BF16)
    hmask16 = (jnp.arange(GDN_HEADS * CHUNK)[:, None] // CHUNK == jnp.arange(GDN_WIDTH)[None, :] // GDN_HEAD_DIM).astype(BF16)

    widths = (RWKV_SHIFT_WIDTH, 3 * GDN_WIDTH, GDN_WIDTH, AB_PAD, 2 * d)
    dtypes = (BF16, BF16, BF16, F32, BF16)

    for l in range(depth):
        p_rwkv, qkv, z, ab, gate = _inproj(x.reshape(n, d), _row(norm1_g[l]), w_in[l].astype(BF16),
                                           gdn_conv_w[l].astype(F32), widths, dtypes, tm_in, t // tm_in)

        rwkv_params = (_row(rwkv_mu[l]), _row(rwkv_w0[l]), rwkv_w2[l].astype(BF16), _row(rwkv_a0[l]),
                       rwkv_a2[l].astype(BF16), rwkv_g2[l].astype(BF16), _row(rwkv_k_k[l]), _row(rwkv_k_a[l]),
                       _row(rwkv_r_k[l]), _row(rwkv_ln_w[l]), _row(rwkv_ln_b[l]))
        y_a = _rwkv(p_rwkv.reshape(b, t, -1), rwkv_params, mask16, maskf, bb)

        y_b = _gdn(qkv.reshape(b, t, -1), z.reshape(b, t, -1), ab.reshape(b, t, -1),
                   _pad_lanes(gdn_a_log[l]), _pad_lanes(gdn_dt_bias[l]), _row(gdn_norm_w[l]), mask16, hmask16, bb)

        x = _mixffn(x, y_a, y_b, gate.reshape(b, t, -1), rwkv_proj[l].astype(BF16), gdn_proj[l].astype(BF16),
                    w_out[l].astype(BF16), _row(norm2_g[l]), ffn_up[l].astype(BF16), ffn_conv_w[l].astype(F32),
                    ffn_down[l].astype(BF16), _row(final_g), tm, apply_final=(l == depth - 1))
    return x
```

```python
import functools
import math

import jax
import jax.numpy as jnp
from jax import lax
from jax.experimental import pallas as pl
from jax.experimental.pallas import tpu as pltpu

F32 = jnp.float32
BF16 = jnp.bfloat16

CHUNK = 64
RWKV_HEADS = 8
RWKV_HEAD_DIM = 64
RWKV_WIDTH = RWKV_HEADS * RWKV_HEAD_DIM
DECAY_LORA = 64
ICLR_LORA = 64
GATE_LORA = 128
GDN_HEADS = 4
GDN_HEAD_DIM = 128
GDN_WIDTH = GDN_HEADS * GDN_HEAD_DIM
GDN_CONV = 4
FFN_CONV = 3
NORM_EPS = 1e-6
L2_EPS = 1e-6
RWKV_GN_EPS = 64e-5
RWKV_SHIFT_WIDTH = 3 * RWKV_WIDTH + DECAY_LORA + ICLR_LORA + GATE_LORA
LANES = 128
SUBLANES = 8
MXU_TILE = 256
AB_PAD = LANES
GROUP = MXU_TILE // CHUNK
RWKV_GROUPS = RWKV_WIDTH // MXU_TILE

VMEM_LIMIT = 52 * 1024 * 1024


def _sigmoid(x):
    return 1.0 / (1.0 + jnp.exp(-x))


def _softplus(x):
    return jnp.maximum(x, 0.0) + jnp.log(1.0 + jnp.exp(-jnp.abs(x)))


def _dot(a, b):
    return jnp.dot(a.astype(BF16), b.astype(BF16), preferred_element_type=F32)


def _dot_nt(a, b):
    return lax.dot_general(a.astype(BF16), b.astype(BF16), (((1,), (1,)), ((), ())), preferred_element_type=F32)


def _dot_tn(a, b):
    return lax.dot_general(a.astype(BF16), b.astype(BF16), (((0,), (0,)), ((), ())), preferred_element_type=F32)


def _bdiag(x, mask16):
    reps = MXU_TILE // x.shape[0]
    return jnp.concatenate([x.astype(BF16)] * reps, axis=0) * mask16


def _cat_masks():
    ri = lax.broadcasted_iota(jnp.int32, (CHUNK, MXU_TILE), 0)
    ci = lax.broadcasted_iota(jnp.int32, (CHUNK, MXU_TILE), 1) % CHUNK
    return ri > ci, ri >= ci, (ri == ci).astype(F32)


def _unit_lower_inverse_cat(ms, eye_cat, mask16):
    ts = [eye_cat + m for m in ms]
    mks = [_dot(m, _bdiag(m, mask16)) for m in ms]
    levels = CHUNK.bit_length() - 1
    for _ in range(levels - 2):
        boths = [_dot(jnp.concatenate([mk, t], axis=0), _bdiag(mk, mask16)) for mk, t in zip(mks, ts)]
        mks = [both[:CHUNK] for both in boths]
        ts = [t + both[CHUNK:] for t, both in zip(ts, boths)]
    return [t + _dot(t, _bdiag(mk, mask16)) for t, mk in zip(ts, mks)]


def _inproj_body(x_ref, g_ref, w_ref, cw_ref, o_rwkv, o_qkv, o_z, o_ab, o_gate, wg_ref, tail_ref, *, tiles_per_seq):
    ab_start = o_rwkv.shape[-1] + o_qkv.shape[-1] + o_z.shape[-1]
    gate_start = w_ref.shape[-1] - o_gate.shape[-1]

    @pl.when(pl.program_id(0) == 0)
    def _():
        wg_ref[...] = w_ref[:, gate_start:]

    x = x_ref[...]
    ms = jnp.mean(x * x, axis=-1, keepdims=True)
    u = (x * lax.rsqrt(ms + NORM_EPS) * g_ref[...]).astype(BF16)
    tm = x_ref.shape[0]
    w_r, w_q, w_z = o_rwkv.shape[-1], o_qkv.shape[-1], o_z.shape[-1]

    @pl.when(pl.program_id(0) % tiles_per_seq == 0)
    def _():
        tail_ref[...] = jnp.zeros_like(tail_ref)

    def conv_silu(h, lo):
        cols = slice(lo, lo + MXU_TILE)
        hp = jnp.concatenate([tail_ref[:, cols], h], axis=0)
        conv = h * cw_ref[GDN_CONV - 1:GDN_CONV, cols]
        for i in range(GDN_CONV - 1):
            s = SUBLANES - (GDN_CONV - 1) + i
            conv = conv + hp[s:s + tm, :] * cw_ref[i:i + 1, cols]
        tail_ref[:, cols] = h[tm - SUBLANES:tm, :]
        o_qkv[:, cols] = (conv * _sigmoid(conv)).astype(o_qkv.dtype)

    def project(ref, wref, out_lo, w_lo):
        ref[:, out_lo:out_lo + MXU_TILE] = jnp.dot(
            u, wref[:, w_lo:w_lo + MXU_TILE], preferred_element_type=F32).astype(ref.dtype)

    plain = [functools.partial(project, o_rwkv, w_ref, k, k) for k in range(0, w_r, MXU_TILE)]
    plain += [functools.partial(project, o_z, w_ref, k, w_r + w_q + k) for k in range(0, w_z, MXU_TILE)]
    plain += [functools.partial(project, o_gate, wg_ref, k, k) for k in range(0, o_gate.shape[-1], MXU_TILE)]
    qcol = lambda lo: jnp.dot(u, w_ref[:, w_r + lo:w_r + lo + MXU_TILE], preferred_element_type=F32)
    hq = qcol(0)
    for lo in range(0, w_q, MXU_TILE):
        nxt = qcol(lo + MXU_TILE) if lo + MXU_TILE < w_q else None
        for _ in range(2):
            plain.pop(0)()
        conv_silu(hq, lo)
        hq = nxt
    for emit in plain:
        emit()
    o_ab[...] = jnp.dot(u, w_ref[:, ab_start:ab_start + o_ab.shape[-1]], preferred_element_type=F32)


def _inproj(x2, gain, w16, cw, widths, dtypes, tm, tiles_per_seq):
    n, d = x2.shape
    return pl.pallas_call(
        functools.partial(_inproj_body, tiles_per_seq=tiles_per_seq),
        grid=(n // tm,),
        in_specs=[
            pl.BlockSpec((tm, d), lambda i: (i, 0)),
            pl.BlockSpec((1, d), lambda i: (0, 0)),
            pl.BlockSpec(w16.shape, lambda i: (0, 0), pipeline_mode=pl.Buffered(1)),
            pl.BlockSpec(cw.shape, lambda i: (0, 0)),
        ],
        out_specs=[pl.BlockSpec((tm, w), lambda i: (i, 0)) for w in widths],
        out_shape=[jax.ShapeDtypeStruct((n, w), dt) for w, dt in zip(widths, dtypes)],
        scratch_shapes=[pltpu.VMEM((d, widths[-1]), BF16), pltpu.VMEM((SUBLANES, widths[1]), F32)],
        compiler_params=pltpu.CompilerParams(dimension_semantics=("arbitrary",), vmem_limit_bytes=VMEM_LIMIT),
        name="inproj",
    )(x2, gain, w16, cw)


def _rwkv_body(p_ref, mu_ref, w0_ref, w2_ref, a0_ref, a2_ref, g2_ref, kk_ref, ka_ref, rk_ref, lnw_ref,
               lnb_ref, m16_ref, mf_ref, tril_ref, y_ref, prev_ref, s_ref):
    bb, c, wd = p_ref.shape
    rows = bb * c
    hd = RWKV_HEAD_DIM

    @pl.when(pl.program_id(1) == 0)
    def _():
        prev_ref[...] = jnp.zeros_like(prev_ref)
        s_ref[...] = jnp.zeros_like(s_ref)

    row8 = lax.broadcasted_iota(jnp.int32, (SUBLANES, 1), 0)
    ps, shifts = [], []
    for bi in range(bb):
        pb_ = p_ref[bi].astype(F32)
        rolled = pltpu.roll(pb_, 1, 0)
        shifts += [jnp.where(row8 == 0, prev_ref[bi, 0:1, :], rolled[:SUBLANES]), rolled[SUBLANES:]]
        prev_ref[bi, 0:1, :] = pb_[c - 1:c, :]
        ps.append(pb_)
    p = jnp.concatenate(ps, axis=0)
    pm = p + (jnp.concatenate(shifts, axis=0) - p) * mu_ref[...]

    w = RWKV_WIDTH
    r = pm[:, 0:w]
    k = pm[:, w:2 * w]
    v = pm[:, 2 * w:3 * w]
    o0 = 3 * w
    wl = pm[:, o0:o0 + DECAY_LORA]
    al = pm[:, o0 + DECAY_LORA:o0 + DECAY_LORA + ICLR_LORA]
    gl = pm[:, o0 + DECAY_LORA + ICLR_LORA:o0 + DECAY_LORA + ICLR_LORA + GATE_LORA]

    z = w0_ref[...] + _dot(jnp.tanh(wl), w2_ref[...])
    lw = (-math.exp(-0.5) * _sigmoid(z)).astype(BF16).astype(F32)
    a = _sigmoid(a0_ref[...] + _dot(al, a2_ref[...]))
    g = _dot(_sigmoid(gl), g2_ref[...])

    mask16 = m16_ref[...]
    maskf = mf_ref[...]

    def head_sums(*xs):
        stacked = jnp.concatenate([x[:, q * MXU_TILE:(q + 1) * MXU_TILE] for x in xs for q in range(RWKV_GROUPS)], axis=0)
        res = jnp.dot(stacked.astype(BF16), mask16, preferred_element_type=F32)
        outs = []
        for i in range(len(xs)):
            base = i * RWKV_GROUPS * rows
            outs.append(jnp.concatenate([res[base + q * rows:base + (q + 1) * rows] for q in range(RWKV_GROUPS)], axis=1))
        return outs

    kx = k * kk_ref[...]
    k2 = k * (1.0 + (a - 1.0) * ka_ref[...])
    kx_ss, rk_sum = head_sums(kx * kx, r * k2 * rk_ref[...])
    kk = kx * lax.rsqrt(kx_ss + L2_EPS)

    cum = jnp.dot(tril_ref[...], lw.astype(BF16), preferred_element_type=F32)
    e_pos = jnp.exp(cum)
    e_neg = 1.0 / e_pos
    e_prev = jnp.exp(cum - lw)
    w_tot = [e_pos[(bi + 1) * c - 1:(bi + 1) * c, :] for bi in range(bb)]
    e_rest = jnp.concatenate([e_neg[bi * c:(bi + 1) * c] * w_tot[bi] for bi in range(bb)], axis=0)

    b_vec = kk * a
    a_t = (-kk * e_prev).astype(BF16)
    r_t = (r * e_pos).astype(BF16)
    b_t = (b_vec * e_neg).astype(BF16)
    k_t = (k2 * e_neg).astype(BF16)
    b_h = (b_vec * e_rest).astype(BF16)
    k_h = (k2 * e_rest).astype(BF16)
    v16 = v.astype(BF16)

    strict_cat, incl_cat, eye_cat = _cat_masks()
    mask2 = jnp.concatenate([strict_cat, incl_cat], axis=0)

    chains = [(bi, q) for bi in range(bb) for q in range(RWKV_GROUPS)]
    rs_of = lambda bi: slice(bi * c, (bi + 1) * c)
    gs_of = lambda q: slice(q * MXU_TILE, (q + 1) * MXU_TILE)
    s_bds = [s_ref[bi, q] for bi, q in chains]
    pxs = []
    for (bi, q), s_bd in zip(chains, s_bds):
        rs, gs = rs_of(bi), gs_of(q)
        l2 = jnp.concatenate([a_t[rs, gs], r_t[rs, gs]], axis=0)
        w3 = jnp.concatenate([_bdiag(b_t[rs, gs], mask16), _bdiag(k_t[rs, gs], mask16), s_bd.astype(BF16)], axis=0)
        pxs.append(_dot_nt(l2, w3))
    pbs = [jnp.where(mask2, px[:, 0:MXU_TILE], 0.0) for px in pxs]
    pks = [jnp.where(mask2, px[:, MXU_TILE:2 * MXU_TILE], 0.0) for px in pxs]
    xas = [px[:, 2 * MXU_TILE:] + _dot(pk, _bdiag(v16[rs_of(bi), gs_of(q)], mask16))
           for (bi, q), px, pk in zip(chains, pxs, pks)]
    tinvs = _unit_lower_inverse_cat([pb[:c] for pb in pbs], eye_cat, mask16)
    us = [_dot(tinv, _bdiag(xa[:c], mask16)) for tinv, xa in zip(tinvs, xas)]
    ys = [xa[c:] + _dot(pb[c:], _bdiag(u, mask16)) for xa, pb, u in zip(xas, pbs, us)]
    for (bi, q), s_bd, u in zip(chains, s_bds, us):
        rs, gs = rs_of(bi), gs_of(q)
        uv = jnp.concatenate([u.astype(BF16), v16[rs, gs]], axis=0)
        bk = jnp.concatenate([b_h[rs, gs], k_h[rs, gs]], axis=0)
        s_ref[bi, q] = s_bd * w_tot[bi][:, gs] + _dot_tn(uv, bk) * maskf
    y_rows = [jnp.concatenate(ys[bi * RWKV_GROUPS:(bi + 1) * RWKV_GROUPS], axis=1) for bi in range(bb)]

    y = jnp.concatenate(y_rows, axis=0)
    inv_hd = 1.0 / hd
    (mean,) = head_sums(y)
    yc = y - mean * inv_hd
    (var,) = head_sums(yc * yc)
    yn = yc * lax.rsqrt(var * inv_hd + RWKV_GN_EPS) * lnw_ref[...] + lnb_ref[...]
    out = (yn + rk_sum * v) * g
    for bi in range(bb):
        y_ref[bi] = out[bi * c:(bi + 1) * c, :].astype(y_ref.dtype)


def _rwkv(p3, params, mask16, maskf, tril16, bb):
    b, t, wd = p3.shape
    c = CHUNK
    full = lambda arr: pl.BlockSpec(arr.shape, lambda i, j: (0,) * arr.ndim)
    return pl.pallas_call(
        _rwkv_body,
        grid=(b // bb, t // c),
        in_specs=[pl.BlockSpec((bb, c, wd), lambda i, j: (i, j, 0))] + [full(a) for a in params]
        + [full(mask16), full(maskf), full(tril16)],
        out_specs=pl.BlockSpec((bb, c, RWKV_WIDTH), lambda i, j: (i, j, 0)),
        out_shape=jax.ShapeDtypeStruct((b, t, RWKV_WIDTH), BF16),
        scratch_shapes=[
            pltpu.VMEM((bb, SUBLANES, wd), F32),
            pltpu.VMEM((bb, RWKV_GROUPS, MXU_TILE, MXU_TILE), F32),
        ],
        compiler_params=pltpu.CompilerParams(dimension_semantics=("arbitrary", "arbitrary"),
                                             vmem_limit_bytes=VMEM_LIMIT),
        name="rwkv",
    )(p3, *params, mask16, maskf, tril16)


def _gdn_body(qkv_ref, z_ref, ab_ref, alog_ref, dtb_ref, nw_ref, m16_ref, hm16_ref, tril_ref, y_ref, s_ref):
    bb, c, _ = qkv_ref.shape
    hd = GDN_HEAD_DIM
    nh = GDN_HEADS

    @pl.when(pl.program_id(1) == 0)
    def _():
        s_ref[...] = jnp.zeros_like(s_ref)

    qkv = jnp.concatenate([qkv_ref[bi] for bi in range(bb)], axis=0).astype(F32)
    w = GDN_WIDTH

    ab = jnp.concatenate([ab_ref[bi] for bi in range(bb)], axis=0)
    g_all = -jnp.exp(alog_ref[...]) * _softplus(ab + dtb_ref[...])
    beta_all = _sigmoid(ab)

    g_hi = g_all.astype(BF16)
    rem = g_all - g_hi.astype(F32)
    g_mid = rem.astype(BF16)
    g_lo = (rem - g_mid.astype(F32)).astype(BF16)
    gcp = jnp.dot(tril_ref[...], jnp.concatenate([g_hi, g_mid, g_lo], axis=1), preferred_element_type=F32)
    gc_all = gcp[:, 0:LANES] + gcp[:, LANES:2 * LANES] + gcp[:, 2 * LANES:3 * LANES]
    eg_all = jnp.exp(gc_all)

    mask16 = m16_ref[...]
    hmask16 = hm16_ref[...]
    strict_cat, incl_cat, eye_cat = _cat_masks()
    lane_head = lax.broadcasted_iota(jnp.int32, (c, MXU_TILE), 1) // CHUNK

    qs, ks, kbs = [], [], []
    for h in range(nh):
        qh = qkv[:, h * hd:(h + 1) * hd]
        kh = qkv[:, w + h * hd:w + (h + 1) * hd]
        qs.append(qh * lax.rsqrt(jnp.sum(qh * qh, axis=-1, keepdims=True) + L2_EPS) * (hd ** -0.5))
        kh = kh * lax.rsqrt(jnp.sum(kh * kh, axis=-1, keepdims=True) + L2_EPS)
        ks.append(kh)
        kbs.append(kh * beta_all[:, nh + h:nh + h + 1])
    k_all = jnp.concatenate(ks, axis=1).astype(BF16)
    kb_all = jnp.concatenate(kbs, axis=1).astype(BF16)
    q_all = jnp.concatenate(qs, axis=1).astype(BF16)

    rs_of = lambda bi: slice(bi * c, (bi + 1) * c)
    gc_bs = [gc_all[rs_of(bi)] for bi in range(bb)]
    glasts = [gc_b[c - 1:c, :] for gc_b in gc_bs]
    erests = [jnp.exp(glast - gc_b) for glast, gc_b in zip(glasts, gc_bs)]
    elasts = [jnp.exp(glast) for glast in glasts]

    decays = []
    for gc_b in gc_bs:
        f_cat = jnp.zeros((c, MXU_TILE), F32)
        for h in range(nh):
            f_cat = jnp.where(lane_head == h, gc_b[:, h:h + 1], f_cat)
        r_cat = jnp.sum(f_cat * eye_cat, axis=0, keepdims=True)
        decays.append(jnp.where(incl_cat, jnp.exp(jnp.where(incl_cat, f_cat - r_cat, 0.0)), 0.0))

    kqs = [_dot_nt(jnp.concatenate([kb_all[rs_of(bi)], q_all[rs_of(bi)]], axis=0),
                   jnp.concatenate([k_all[rs_of(bi)]] * nh, axis=0) * hmask16) for bi in range(bb)]
    attns = [kq[c:] * decay for kq, decay in zip(kqs, decays)]
    tinvs = _unit_lower_inverse_cat([-jnp.where(strict_cat, kq[:c] * decay, 0.0) for kq, decay in zip(kqs, decays)],
                                    eye_cat, mask16)

    chains = [(bi, h) for bi in range(bb) for h in range(nh)]
    tl_of = lambda h: slice(h * c, (h + 1) * c)
    egs = [eg_all[rs_of(bi), h:h + 1] for bi, h in chains]
    uws = []
    for (bi, h), eg in zip(chains, egs):
        rs = rs_of(bi)
        vh = qkv[rs, 2 * w + h * hd:2 * w + (h + 1) * hd]
        beta = beta_all[rs, nh + h:nh + h + 1]
        uws.append(_dot(tinvs[bi][:, tl_of(h)], jnp.concatenate([vh * beta, kbs[h][rs] * eg], axis=1)))
    ss = [s_ref[bi, h] for bi, h in chains]
    wqs = [_dot(jnp.concatenate([uw[:, hd:], qs[h][rs_of(bi)] * eg], axis=0), s)
           for (bi, h), uw, eg, s in zip(chains, uws, egs, ss)]
    v_news = [uw[:, :hd] - wq[:c] for uw, wq in zip(uws, wqs)]
    os_ = [wq[c:] + _dot(attns[bi][:, tl_of(h)], v_new) for (bi, h), wq, v_new in zip(chains, wqs, v_news)]
    for (bi, h), s, v_new in zip(chains, ss, v_news):
        s_ref[bi, h] = s * elasts[bi][:, h:h + 1] + _dot_tn(ks[h][rs_of(bi)] * erests[bi][:, h:h + 1], v_new)
    for (bi, h), o in zip(chains, os_):
        sl = slice(h * hd, (h + 1) * hd)
        on = o * lax.rsqrt(jnp.mean(o * o, axis=-1, keepdims=True) + NORM_EPS) * nw_ref[...]
        zh = z_ref[bi, :, sl].astype(F32)
        y_ref[bi, :, sl] = (on * (zh * _sigmoid(zh))).astype(y_ref.dtype)


def _gdn(qkv3, z3, ab3, alog, dtb, nw, mask16, hmask16, tril16, bb):
    b, t, wd = qkv3.shape
    c = CHUNK
    full = lambda arr: pl.BlockSpec(arr.shape, lambda i, j: (0,) * arr.ndim)
    return pl.pallas_call(
        _gdn_body,
        grid=(b // bb, t // c),
        in_specs=[
            pl.BlockSpec((bb, c, wd), lambda i, j: (i, j, 0)),
            pl.BlockSpec((bb, c, GDN_WIDTH), lambda i, j: (i, j, 0)),
            pl.BlockSpec((bb, c, AB_PAD), lambda i, j: (i, j, 0)),
            full(alog), full(dtb), full(nw), full(mask16), full(hmask16), full(tril16),
        ],
        out_specs=pl.BlockSpec((bb, c, GDN_WIDTH), lambda i, j: (i, j, 0)),
        out_shape=jax.ShapeDtypeStruct((b, t, GDN_WIDTH), BF16),
        scratch_shapes=[pltpu.VMEM((bb, GDN_HEADS, GDN_HEAD_DIM, GDN_HEAD_DIM), F32)],
        compiler_params=pltpu.CompilerParams(dimension_semantics=("arbitrary", "arbitrary"),
                                             vmem_limit_bytes=VMEM_LIMIT),
        name="gdn",
    )(qkv3, z3, ab3, alog, dtb, nw, mask16, hmask16, tril16)


def _mixffn_body(x_ref, ya_ref, yb_ref, gate_ref, rp_ref, gp_ref, wo_ref, g_ref, up_ref, cw_ref, down_ref, fg_ref,
                 o_ref, prev_ref, *, col_block, apply_final):
    tm, d = x_ref.shape
    hidden = down_ref.shape[0]

    @pl.when(pl.program_id(1) == 0)
    def _():
        prev_ref[...] = jnp.zeros_like(prev_ref)

    ya = jnp.dot(ya_ref[...], rp_ref[...], preferred_element_type=F32)
    yb = jnp.dot(yb_ref[...], gp_ref[...], preferred_element_type=F32)
    mixed = _sigmoid(gate_ref[:, :d].astype(F32)) * ya + _sigmoid(gate_ref[:, d:].astype(F32)) * yb
    x = x_ref[...] + jnp.dot(mixed.astype(BF16), wo_ref[...], preferred_element_type=F32)

    ms = jnp.mean(x * x, axis=-1, keepdims=True)
    xn = (x * lax.rsqrt(ms + NORM_EPS) * g_ref[...]).astype(BF16)

    def up_cols(lo):
        return jnp.dot(xn, up_ref[:, lo:lo + col_block], preferred_element_type=F32)

    def conv_cols(h, lo):
        hp = jnp.concatenate([prev_ref[:, lo:lo + col_block], h], axis=0)
        out = h * cw_ref[FFN_CONV - 1:FFN_CONV, lo:lo + col_block]
        for i in range(FFN_CONV - 1):
            s = SUBLANES - (FFN_CONV - 1) + i
            out = out + hp[s:s + tm, :] * cw_ref[i:i + 1, lo:lo + col_block]
        prev_ref[:, lo:lo + col_block] = h[tm - SUBLANES:tm, :]
        return out

    nblk = hidden // col_block
    acc = x
    hs = (up_cols(0), up_cols(hidden))
    for j in range(nblk):
        lo = j * col_block
        nxt = (up_cols(lo + col_block), up_cols(hidden + lo + col_block)) if j + 1 < nblk else None
        hg = conv_cols(hs[0], lo)
        hu = conv_cols(hs[1], hidden + lo)
        act = (hg * _sigmoid(hg) * hu).astype(BF16)
        acc = acc + jnp.dot(act, down_ref[lo:lo + col_block, :], preferred_element_type=F32)
        hs = nxt
    if apply_final:
        ms2 = jnp.mean(acc * acc, axis=-1, keepdims=True)
        acc = acc * lax.rsqrt(ms2 + NORM_EPS) * fg_ref[...]
    o_ref[...] = acc


def _mixffn(x3, ya3, yb3, gate3, rp, gp, wo, gain, up, cw, down, final_g, tm, apply_final):
    b, t, d = x3.shape
    hidden = down.shape[0]
    full = lambda arr: pl.BlockSpec(arr.shape, lambda i, j: (0, 0))
    single = lambda arr: pl.BlockSpec(arr.shape, lambda i, j: (0, 0), pipeline_mode=pl.Buffered(1))
    tile = lambda arr: pl.BlockSpec((None, tm, arr.shape[-1]), lambda i, j: (i, j, 0))
    body = functools.partial(_mixffn_body, col_block=MXU_TILE, apply_final=apply_final)
    return pl.pallas_call(
        body,
        grid=(b, t // tm),
        in_specs=[tile(x3), tile(ya3), tile(yb3), tile(gate3), single(rp), single(gp), single(wo),
                  full(gain), single(up), full(cw), single(down), full(final_g)],
        out_specs=pl.BlockSpec((None, tm, d), lambda i, j: (i, j, 0)),
        out_shape=jax.ShapeDtypeStruct((b, t, d), F32),
        scratch_shapes=[pltpu.VMEM((SUBLANES, 2 * hidden), F32)],
        compiler_params=pltpu.CompilerParams(dimension_semantics=("arbitrary", "arbitrary"),
                                             vmem_limit_bytes=VMEM_LIMIT),
        name="mixffn",
    )(x3, ya3, yb3, gate3, rp, gp, wo, gain, up, cw, down, final_g)


def _row(vec):
    return vec.reshape(1, -1).astype(F32)


def _pad_lanes(vec, offset=0):
    out = jnp.zeros((1, LANES), F32)
    return lax.dynamic_update_slice(out, vec.reshape(1, -1).astype(F32), (0, offset))


def kernel(x, norm1_g, w_in, rwkv_mu, rwkv_w0, rwkv_w2, rwkv_a0, rwkv_a2, rwkv_g2, rwkv_k_k, rwkv_k_a,
           rwkv_r_k, rwkv_ln_w, rwkv_ln_b, rwkv_proj, gdn_conv_w, gdn_a_log, gdn_dt_bias, gdn_norm_w,
           gdn_proj, w_out, norm2_g, ffn_up, ffn_conv_w, ffn_down, final_g):
    b, t, d = x.shape
    n = b * t
    depth = norm1_g.shape[0]
    tm = 256
    tm_in = 512
    bb = next(c for c in (8, 4, 2, 1) if b % c == 0)

    blk = jnp.arange(MXU_TILE) // CHUNK
    maskf = (blk[:, None] == blk[None, :]).astype(F32)
    mask16 = maskf.astype(BF16)
    rr = jnp.arange(bb * CHUNK)
    tril16 = ((rr[:, None] >= rr[None, :]) & (rr[:, None] // CHUNK == rr[None, :] // CHUNK)).astype(BF16)
    hmask16 = (jnp.arange(GDN_HEADS * CHUNK)[:, None] // CHUNK == jnp.arange(GDN_WIDTH)[None, :] // GDN_HEAD_DIM).astype(BF16)

    widths = (RWKV_SHIFT_WIDTH, 3 * GDN_WIDTH, GDN_WIDTH, AB_PAD, 2 * d)
    dtypes = (BF16, BF16, BF16, F32, BF16)

    for l in range(depth):
        p_rwkv, qkv, z, ab, gate = _inproj(x.reshape(n, d), _row(norm1_g[l]), w_in[l].astype(BF16),
                                           gdn_conv_w[l].astype(F32), widths, dtypes, tm_in, t // tm_in)

        rwkv_params = (_row(rwkv_mu[l]), _row(rwkv_w0[l]), rwkv_w2[l].astype(BF16), _row(rwkv_a0[l]),
                       rwkv_a2[l].astype(BF16), rwkv_g2[l].astype(BF16), _row(rwkv_k_k[l]), _row(rwkv_k_a[l]),
                       _row(rwkv_r_k[l]), _row(rwkv_ln_w[l]), _row(rwkv_ln_b[l]))
        y_a = _rwkv(p_rwkv.reshape(b, t, -1), rwkv_params, mask16, maskf, tril16, bb)

        y_b = _gdn(qkv.reshape(b, t, -1), z.reshape(b, t, -1), ab.reshape(b, t, -1),
                   _pad_lanes(gdn_a_log[l]), _pad_lanes(gdn_dt_bias[l]), _row(gdn_norm_w[l]), mask16, hmask16, tril16, bb)

        x = _mixffn(x, y_a, y_b, gate.reshape(b, t, -1), rwkv_proj[l].astype(BF16), gdn_proj[l].astype(BF16),
                    w_out[l].astype(BF16), _row(norm2_g[l]), ffn_up[l].astype(BF16), ffn_conv_w[l].astype(F32),
                    ffn_down[l].astype(BF16), _row(final_g), tm, apply_final=(l == depth - 1))
    return x
```

```python
import functools
import math

import jax
import jax.numpy as jnp
from jax import lax
from jax.experimental import pallas as pl
from jax.experimental.pallas import tpu as pltpu

F32 = jnp.float32
BF16 = jnp.bfloat16

CHUNK = 64
RWKV_HEADS = 8
RWKV_HEAD_DIM = 64
RWKV_WIDTH = RWKV_HEADS * RWKV_HEAD_DIM
DECAY_LORA = 64
ICLR_LORA = 64
GATE_LORA = 128
GDN_HEADS = 4
GDN_HEAD_DIM = 128
GDN_WIDTH = GDN_HEADS * GDN_HEAD_DIM
GDN_CONV = 4
FFN_CONV = 3
NORM_EPS = 1e-6
L2_EPS = 1e-6
RWKV_GN_EPS = 64e-5
RWKV_SHIFT_WIDTH = 3 * RWKV_WIDTH + DECAY_LORA + ICLR_LORA + GATE_LORA
LANES = 128
SUBLANES = 8
MXU_TILE = 256
AB_PAD = LANES
GROUP = MXU_TILE // CHUNK
RWKV_GROUPS = RWKV_WIDTH // MXU_TILE

VMEM_LIMIT = 52 * 1024 * 1024


def _sigmoid(x):
    return 1.0 / (1.0 + jnp.exp(-x))


def _softplus(x):
    return jnp.maximum(x, 0.0) + jnp.log(1.0 + jnp.exp(-jnp.abs(x)))


def _dot(a, b):
    return jnp.dot(a.astype(BF16), b.astype(BF16), preferred_element_type=F32)


def _dot_nt(a, b):
    return lax.dot_general(a.astype(BF16), b.astype(BF16), (((1,), (1,)), ((), ())), preferred_element_type=F32)


def _dot_tn(a, b):
    return lax.dot_general(a.astype(BF16), b.astype(BF16), (((0,), (0,)), ((), ())), preferred_element_type=F32)


def _bdiag(x, mask16):
    reps = MXU_TILE // x.shape[0]
    return jnp.concatenate([x.astype(BF16)] * reps, axis=0) * mask16


def _chunk_cumsum(tril16, x16):
    tr = tril16.shape[0]
    return jnp.concatenate([jnp.dot(tril16, x16[r0:r0 + tr], preferred_element_type=F32)
                            for r0 in range(0, x16.shape[0], tr)], axis=0)


def _cat_masks():
    ri = lax.broadcasted_iota(jnp.int32, (CHUNK, MXU_TILE), 0)
    ci = lax.broadcasted_iota(jnp.int32, (CHUNK, MXU_TILE), 1) % CHUNK
    return ri > ci, ri >= ci, (ri == ci).astype(F32)


def _unit_lower_inverse_cat(ms, eye_cat, mask16):
    ts = [eye_cat + m for m in ms]
    mks = [_dot(m, _bdiag(m, mask16)) for m in ms]
    levels = CHUNK.bit_length() - 1
    for _ in range(levels - 2):
        boths = [_dot(jnp.concatenate([mk, t], axis=0), _bdiag(mk, mask16)) for mk, t in zip(mks, ts)]
        mks = [both[:CHUNK] for both in boths]
        ts = [t + both[CHUNK:] for t, both in zip(ts, boths)]
    return [t + _dot(t, _bdiag(mk, mask16)) for t, mk in zip(ts, mks)]


def _inproj_body(x_ref, g_ref, w_ref, cw_ref, o_rwkv, o_qkv, o_z, o_ab, o_gate, wg_ref, tail_ref, *, tiles_per_seq):
    ab_start = o_rwkv.shape[-1] + o_qkv.shape[-1] + o_z.shape[-1]
    gate_start = w_ref.shape[-1] - o_gate.shape[-1]

    @pl.when(pl.program_id(0) == 0)
    def _():
        wg_ref[...] = w_ref[:, gate_start:]

    x = x_ref[...]
    ms = jnp.mean(x * x, axis=-1, keepdims=True)
    u = (x * lax.rsqrt(ms + NORM_EPS) * g_ref[...]).astype(BF16)
    tm = x_ref.shape[0]
    w_r, w_q, w_z = o_rwkv.shape[-1], o_qkv.shape[-1], o_z.shape[-1]

    @pl.when(pl.program_id(0) % tiles_per_seq == 0)
    def _():
        tail_ref[...] = jnp.zeros_like(tail_ref)

    def conv_silu(h, lo):
        cols = slice(lo, lo + MXU_TILE)
        hp = jnp.concatenate([tail_ref[:, cols], h], axis=0)
        conv = h * cw_ref[GDN_CONV - 1:GDN_CONV, cols]
        for i in range(GDN_CONV - 1):
            s = SUBLANES - (GDN_CONV - 1) + i
            conv = conv + hp[s:s + tm, :] * cw_ref[i:i + 1, cols]
        tail_ref[:, cols] = h[tm - SUBLANES:tm, :]
        o_qkv[:, cols] = (conv * _sigmoid(conv)).astype(o_qkv.dtype)

    def project(ref, wref, out_lo, w_lo):
        ref[:, out_lo:out_lo + MXU_TILE] = jnp.dot(
            u, wref[:, w_lo:w_lo + MXU_TILE], preferred_element_type=F32).astype(ref.dtype)

    plain = [functools.partial(project, o_rwkv, w_ref, k, k) for k in range(0, w_r, MXU_TILE)]
    plain += [functools.partial(project, o_z, w_ref, k, w_r + w_q + k) for k in range(0, w_z, MXU_TILE)]
    plain += [functools.partial(project, o_gate, wg_ref, k, k) for k in range(0, o_gate.shape[-1], MXU_TILE)]
    qcol = lambda lo: jnp.dot(u, w_ref[:, w_r + lo:w_r + lo + MXU_TILE], preferred_element_type=F32)
    hq = qcol(0)
    for lo in range(0, w_q, MXU_TILE):
        nxt = qcol(lo + MXU_TILE) if lo + MXU_TILE < w_q else None
        for _ in range(2):
            plain.pop(0)()
        conv_silu(hq, lo)
        hq = nxt
    for emit in plain:
        emit()
    o_ab[...] = jnp.dot(u, w_ref[:, ab_start:ab_start + o_ab.shape[-1]], preferred_element_type=F32)


def _inproj(x2, gain, w16, cw, widths, dtypes, tm, tiles_per_seq):
    n, d = x2.shape
    return pl.pallas_call(
        functools.partial(_inproj_body, tiles_per_seq=tiles_per_seq),
        grid=(n // tm,),
        in_specs=[
            pl.BlockSpec((tm, d), lambda i: (i, 0)),
            pl.BlockSpec((1, d), lambda i: (0, 0)),
            pl.BlockSpec(w16.shape, lambda i: (0, 0), pipeline_mode=pl.Buffered(1)),
            pl.BlockSpec(cw.shape, lambda i: (0, 0)),
        ],
        out_specs=[pl.BlockSpec((tm, w), lambda i: (i, 0)) for w in widths],
        out_shape=[jax.ShapeDtypeStruct((n, w), dt) for w, dt in zip(widths, dtypes)],
        scratch_shapes=[pltpu.VMEM((d, widths[-1]), BF16), pltpu.VMEM((SUBLANES, widths[1]), F32)],
        compiler_params=pltpu.CompilerParams(dimension_semantics=("arbitrary",), vmem_limit_bytes=VMEM_LIMIT),
        name="inproj",
    )(x2, gain, w16, cw)


def _rwkv_body(p_ref, *refs):
    *const_refs, y_ref, prev_ref, s_ref = refs

    @pl.when(pl.program_id(1) == 0)
    def _():
        prev_ref[...] = jnp.zeros_like(prev_ref)
        s_ref[...] = jnp.zeros_like(s_ref)

    for r0 in range(0, p_ref.shape[1], CHUNK):
        _rwkv_chunk(p_ref.at[:, r0:r0 + CHUNK, :], *const_refs, y_ref.at[:, r0:r0 + CHUNK, :], prev_ref, s_ref)


def _rwkv_chunk(p_ref, mu_ref, w0_ref, w2_ref, a0_ref, a2_ref, g2_ref, kk_ref, ka_ref, rk_ref, lnw_ref,
                lnb_ref, m16_ref, mf_ref, tril_ref, y_ref, prev_ref, s_ref):
    bb, c, wd = p_ref.shape
    rows = bb * c
    hd = RWKV_HEAD_DIM
    row8 =lax.broadcasted_iota(jnp.int32, (SUBLANES, 1), 0)
    ps, shifts = [], []
    for bi in range(bb):
        pb_ = p_ref[bi].astype(F32)
        rolled = pltpu.roll(pb_, 1, 0)
        shifts += [jnp.where(row8 == 0, prev_ref[bi, 0:1, :], rolled[:SUBLANES]), rolled[SUBLANES:]]
        prev_ref[bi, 0:1, :] = pb_[c - 1:c, :]
        ps.append(pb_)
    p = jnp.concatenate(ps, axis=0)
    pm = p + (jnp.concatenate(shifts, axis=0) - p) * mu_ref[...]

    w = RWKV_WIDTH
    r = pm[:, 0:w]
    k = pm[:, w:2 * w]
    v = pm[:, 2 * w:3 * w]
    o0 = 3 * w
    wl = pm[:, o0:o0 + DECAY_LORA]
    al = pm[:, o0 + DECAY_LORA:o0 + DECAY_LORA + ICLR_LORA]
    gl = pm[:, o0 + DECAY_LORA + ICLR_LORA:o0 + DECAY_LORA + ICLR_LORA + GATE_LORA]

    z = w0_ref[...] + _dot(jnp.tanh(wl), w2_ref[...])
    lw = (-math.exp(-0.5) * _sigmoid(z)).astype(BF16).astype(F32)
    a = _sigmoid(a0_ref[...] + _dot(al, a2_ref[...]))
    g = _dot(_sigmoid(gl), g2_ref[...])

    mask16 = m16_ref[...]
    maskf = mf_ref[...]

    def head_sums(*xs):
        stacked = jnp.concatenate([x[:, q * MXU_TILE:(q + 1) * MXU_TILE] for x in xs for q in range(RWKV_GROUPS)], axis=0)
        res = jnp.dot(stacked.astype(BF16), mask16, preferred_element_type=F32)
        outs = []
        for i in range(len(xs)):
            base = i * RWKV_GROUPS * rows
            outs.append(jnp.concatenate([res[base + q * rows:base + (q + 1) * rows] for q in range(RWKV_GROUPS)], axis=1))
        return outs

    kx = k * kk_ref[...]
    k2 = k * (1.0 + (a - 1.0) * ka_ref[...])
    kx_ss, rk_sum = head_sums(kx * kx, r * k2 * rk_ref[...])
    kk = kx * lax.rsqrt(kx_ss + L2_EPS)

    cum = _chunk_cumsum(tril_ref[...], lw.astype(BF16))
    e_pos = jnp.exp(cum)
    e_neg = 1.0 / e_pos
    e_prev = jnp.exp(cum - lw)
    w_tot = [e_pos[(bi + 1) * c - 1:(bi + 1) * c, :] for bi in range(bb)]
    e_rest = jnp.concatenate([e_neg[bi * c:(bi + 1) * c] * w_tot[bi] for bi in range(bb)], axis=0)

    b_vec = kk * a
    a_t = (-kk * e_prev).astype(BF16)
    r_t = (r * e_pos).astype(BF16)
    b_t = (b_vec * e_neg).astype(BF16)
    k_t = (k2 * e_neg).astype(BF16)
    b_h = (b_vec * e_rest).astype(BF16)
    k_h = (k2 * e_rest).astype(BF16)
    v16 = v.astype(BF16)

    strict_cat, incl_cat, eye_cat = _cat_masks()
    mask2 = jnp.concatenate([strict_cat, incl_cat], axis=0)

    chains = [(bi, q) for bi in range(bb) for q in range(RWKV_GROUPS)]
    rs_of = lambda bi: slice(bi * c, (bi + 1) * c)
    gs_of = lambda q: slice(q * MXU_TILE, (q + 1) * MXU_TILE)
    s_bds = [s_ref[bi, q] for bi, q in chains]
    pxs = []
    for (bi, q), s_bd in zip(chains, s_bds):
        rs, gs = rs_of(bi), gs_of(q)
        l2 = jnp.concatenate([a_t[rs, gs], r_t[rs, gs]], axis=0)
        w3 = jnp.concatenate([_bdiag(b_t[rs, gs], mask16), _bdiag(k_t[rs, gs], mask16), s_bd.astype(BF16)], axis=0)
        pxs.append(_dot_nt(l2, w3))
    pbs = [jnp.where(mask2, px[:, 0:MXU_TILE], 0.0) for px in pxs]
    pks = [jnp.where(mask2, px[:, MXU_TILE:2 * MXU_TILE], 0.0) for px in pxs]
    xas = [px[:, 2 * MXU_TILE:] + _dot(pk, _bdiag(v16[rs_of(bi), gs_of(q)], mask16))
           for (bi, q), px, pk in zip(chains, pxs, pks)]
    tinvs = _unit_lower_inverse_cat([pb[:c] for pb in pbs], eye_cat, mask16)
    us = [_dot(tinv, _bdiag(xa[:c], mask16)) for tinv, xa in zip(tinvs, xas)]
    ys = [xa[c:] + _dot(pb[c:], _bdiag(u, mask16)) for xa, pb, u in zip(xas, pbs, us)]
    for (bi, q), s_bd, u in zip(chains, s_bds, us):
        rs, gs = rs_of(bi), gs_of(q)
        uv = jnp.concatenate([u.astype(BF16), v16[rs, gs]], axis=0)
        bk = jnp.concatenate([b_h[rs, gs], k_h[rs, gs]], axis=0)
        s_ref[bi, q] = s_bd * w_tot[bi][:, gs] + _dot_tn(uv, bk) * maskf
    y_rows = [jnp.concatenate(ys[bi * RWKV_GROUPS:(bi + 1) * RWKV_GROUPS], axis=1) for bi in range(bb)]

    y = jnp.concatenate(y_rows, axis=0)
    inv_hd = 1.0 / hd
    (mean,) = head_sums(y)
    yc = y - mean * inv_hd
    (var,) = head_sums(yc * yc)
    yn = yc * lax.rsqrt(var * inv_hd + RWKV_GN_EPS) * lnw_ref[...] + lnb_ref[...]
    out = (yn + rk_sum * v) * g
    for bi in range(bb):
        y_ref[bi] = out[bi * c:(bi + 1) * c, :].astype(y_ref.dtype)


def _rwkv(p3, params, mask16, maskf, tril16, bb, rows_per_step):
    b, t, wd = p3.shape
    c = rows_per_step
    full = lambda arr: pl.BlockSpec(arr.shape, lambda i, j: (0,) * arr.ndim)
    return pl.pallas_call(
        _rwkv_body,
        grid=(b // bb, t // c),
        in_specs=[pl.BlockSpec((bb, c, wd), lambda i, j: (i, j, 0))] + [full(a) for a in params]
        + [full(mask16), full(maskf), full(tril16)],
        out_specs=pl.BlockSpec((bb, c, RWKV_WIDTH), lambda i, j: (i, j, 0)),
        out_shape=jax.ShapeDtypeStruct((b, t, RWKV_WIDTH), BF16),
        scratch_shapes=[
            pltpu.VMEM((bb, SUBLANES, wd), F32),
            pltpu.VMEM((bb, RWKV_GROUPS, MXU_TILE, MXU_TILE), F32),
        ],
        compiler_params=pltpu.CompilerParams(dimension_semantics=("arbitrary", "arbitrary"),
                                             vmem_limit_bytes=VMEM_LIMIT),
        name="rwkv",
    )(p3, *params, mask16, maskf, tril16)


def _gdn_body(qkv_ref, z_ref, ab_ref, alog_ref, dtb_ref, nw_ref, m16_ref, hm16_ref, tril_ref, y_ref, s_ref):
    bb, c, _ = qkv_ref.shape
    hd = GDN_HEAD_DIM
    nh = GDN_HEADS

    @pl.when(pl.program_id(1) == 0)
    def _():
        s_ref[...] = jnp.zeros_like(s_ref)

    qkv = jnp.concatenate([qkv_ref[bi] for bi in range(bb)], axis=0).astype(F32)
    w = GDN_WIDTH

    ab = jnp.concatenate([ab_ref[bi] for bi in range(bb)], axis=0)
    g_all = -jnp.exp(alog_ref[...]) * _softplus(ab + dtb_ref[...])
    beta_all = _sigmoid(ab)

    g_hi = g_all.astype(BF16)
    rem = g_all - g_hi.astype(F32)
    g_mid = rem.astype(BF16)
    g_lo = (rem - g_mid.astype(F32)).astype(BF16)
    gcp = _chunk_cumsum(tril_ref[...], jnp.concatenate([g_hi, g_mid, g_lo], axis=1))
    gc_all = gcp[:, 0:LANES] + gcp[:, LANES:2 * LANES] + gcp[:, 2 * LANES:3 * LANES]
    eg_all = jnp.exp(gc_all)

    mask16 = m16_ref[...]
    hmask16 = hm16_ref[...]
    strict_cat, incl_cat, eye_cat = _cat_masks()
    lane_head = lax.broadcasted_iota(jnp.int32, (c, MXU_TILE), 1) // CHUNK

    qs, ks, kbs = [], [], []
    for h in range(nh):
        qh = qkv[:, h * hd:(h + 1) * hd]
        kh = qkv[:, w + h * hd:w + (h + 1) * hd]
        qs.append(qh * lax.rsqrt(jnp.sum(qh * qh, axis=-1, keepdims=True) + L2_EPS) * (hd ** -0.5))
        kh = kh * lax.rsqrt(jnp.sum(kh * kh, axis=-1, keepdims=True) + L2_EPS)
        ks.append(kh)
        kbs.append(kh * beta_all[:, nh + h:nh + h + 1])
    k_all = jnp.concatenate(ks, axis=1).astype(BF16)
    kb_all = jnp.concatenate(kbs, axis=1).astype(BF16)
    q_all = jnp.concatenate(qs, axis=1).astype(BF16)

    rs_of = lambda bi: slice(bi * c, (bi + 1) * c)
    gc_bs = [gc_all[rs_of(bi)] for bi in range(bb)]
    glasts = [gc_b[c - 1:c, :] for gc_b in gc_bs]
    erests = [jnp.exp(glast - gc_b) for glast, gc_b in zip(glasts, gc_bs)]
    elasts = [jnp.exp(glast) for glast in glasts]

    decays = []
    for gc_b in gc_bs:
        f_cat = jnp.zeros((c, MXU_TILE), F32)
        for h in range(nh):
            f_cat = jnp.where(lane_head == h, gc_b[:, h:h + 1], f_cat)
        r_cat = jnp.sum(f_cat * eye_cat, axis=0, keepdims=True)
        decays.append(jnp.where(incl_cat, jnp.exp(jnp.where(incl_cat, f_cat - r_cat, 0.0)), 0.0))

    kqs = [_dot_nt(jnp.concatenate([kb_all[rs_of(bi)], q_all[rs_of(bi)]], axis=0),
                   jnp.concatenate([k_all[rs_of(bi)]] * nh, axis=0) * hmask16) for bi in range(bb)]
    attns = [kq[c:] * decay for kq, decay in zip(kqs, decays)]
    tinvs = _unit_lower_inverse_cat([-jnp.where(strict_cat, kq[:c] * decay, 0.0) for kq, decay in zip(kqs, decays)],
                                    eye_cat, mask16)

    chains = [(bi, h) for bi in range(bb) for h in range(nh)]
    tl_of = lambda h: slice(h * c, (h + 1) * c)
    egs = [eg_all[rs_of(bi), h:h + 1] for bi, h in chains]
    uws = []
    for (bi, h), eg in zip(chains, egs):
        rs = rs_of(bi)
        vh = qkv[rs, 2 * w + h * hd:2 * w + (h + 1) * hd]
        beta = beta_all[rs, nh + h:nh + h + 1]
        uws.append(_dot(tinvs[bi][:, tl_of(h)], jnp.concatenate([vh * beta, kbs[h][rs] * eg], axis=1)))
    ss = [s_ref[bi, h] for bi, h in chains]
    wqs = [_dot(jnp.concatenate([uw[:, hd:], qs[h][rs_of(bi)] * eg], axis=0), s)
           for (bi, h), uw, eg, s in zip(chains, uws, egs, ss)]
    v_news = [uw[:, :hd] - wq[:c] for uw, wq in zip(uws, wqs)]
    os_ = [wq[c:] + _dot(attns[bi][:, tl_of(h)], v_new) for (bi, h), wq, v_new in zip(chains, wqs, v_news)]
    for (bi, h), s, v_new in zip(chains, ss, v_news):
        s_ref[bi, h] = s * elasts[bi][:, h:h + 1] + _dot_tn(ks[h][rs_of(bi)] * erests[bi][:, h:h + 1], v_new)
    for (bi, h), o in zip(chains, os_):
        sl = slice(h * hd, (h + 1) * hd)
        on = o * lax.rsqrt(jnp.mean(o * o, axis=-1, keepdims=True) + NORM_EPS) * nw_ref[...]
        zh = z_ref[bi, :, sl].astype(F32)
        y_ref[bi, :, sl] = (on * (zh * _sigmoid(zh))).astype(y_ref.dtype)


def _gdn(qkv3, z3, ab3, alog, dtb, nw, mask16, hmask16, tril16, bb):
    b, t, wd = qkv3.shape
    c = CHUNK
    full = lambda arr: pl.BlockSpec(arr.shape, lambda i, j: (0,) * arr.ndim)
    return pl.pallas_call(
        _gdn_body,
        grid=(b // bb, t // c),
        in_specs=[
            pl.BlockSpec((bb, c, wd), lambda i, j: (i, j, 0)),
            pl.BlockSpec((bb, c, GDN_WIDTH), lambda i, j: (i, j, 0)),
            pl.BlockSpec((bb, c, AB_PAD), lambda i, j: (i, j, 0)),
            full(alog), full(dtb), full(nw), full(mask16), full(hmask16), full(tril16),
        ],
        out_specs=pl.BlockSpec((bb, c, GDN_WIDTH), lambda i, j: (i, j, 0)),
        out_shape=jax.ShapeDtypeStruct((b, t, GDN_WIDTH), BF16),
        scratch_shapes=[pltpu.VMEM((bb, GDN_HEADS, GDN_HEAD_DIM, GDN_HEAD_DIM), F32)],
        compiler_params=pltpu.CompilerParams(dimension_semantics=("arbitrary", "arbitrary"),
                                             vmem_limit_bytes=VMEM_LIMIT),
        name="gdn",
    )(qkv3, z3, ab3, alog, dtb, nw, mask16, hmask16, tril16)


def _mixffn_body(x_ref, ya_ref, yb_ref, gate_ref, rp_ref, gp_ref, wo_ref, g_ref, up_ref, cw_ref, down_ref, fg_ref,
                 o_ref, prev_ref, *, col_block, apply_final):
    tm, d = x_ref.shape
    hidden = down_ref.shape[0]

    @pl.when(pl.program_id(1) == 0)
    def _():
        prev_ref[...] = jnp.zeros_like(prev_ref)

    ya = jnp.dot(ya_ref[...], rp_ref[...], preferred_element_type=F32)
    yb = jnp.dot(yb_ref[...], gp_ref[...], preferred_element_type=F32)
    mixed = _sigmoid(gate_ref[:, :d].astype(F32)) * ya + _sigmoid(gate_ref[:, d:].astype(F32)) * yb
    x = x_ref[...] + jnp.dot(mixed.astype(BF16), wo_ref[...], preferred_element_type=F32)

    ms = jnp.mean(x * x, axis=-1, keepdims=True)
    xn = (x * lax.rsqrt(ms + NORM_EPS) * g_ref[...]).astype(BF16)

    def up_cols(lo):
        return jnp.dot(xn, up_ref[:, lo:lo + col_block], preferred_element_type=F32)

    def conv_cols(h, lo):
        hp = jnp.concatenate([prev_ref[:, lo:lo + col_block], h], axis=0)
        out = h * cw_ref[FFN_CONV - 1:FFN_CONV, lo:lo + col_block]
        for i in range(FFN_CONV - 1):
            s = SUBLANES - (FFN_CONV - 1) + i
            out = out + hp[s:s + tm, :] * cw_ref[i:i + 1, lo:lo + col_block]
        prev_ref[:, lo:lo + col_block] = h[tm - SUBLANES:tm, :]
        return out

    nblk = hidden // col_block
    acc = x
    hs = (up_cols(0), up_cols(hidden))
    for j in range(nblk):
        lo = j * col_block
        nxt = (up_cols(lo + col_block), up_cols(hidden + lo + col_block)) if j + 1 < nblk else None
        hg = conv_cols(hs[0], lo)
        hu = conv_cols(hs[1], hidden + lo)
        act = (hg * _sigmoid(hg) * hu).astype(BF16)
        acc = acc + jnp.dot(act, down_ref[lo:lo + col_block, :], preferred_element_type=F32)
        hs = nxt
    if apply_final:
        ms2 = jnp.mean(acc * acc, axis=-1, keepdims=True)
        acc = acc * lax.rsqrt(ms2 + NORM_EPS) * fg_ref[...]
    o_ref[...] = acc


def _mixffn(x3, ya3, yb3, gate3, rp, gp, wo, gain, up, cw, down, final_g, tm, apply_final):
    b, t, d = x3.shape
    hidden = down.shape[0]
    full = lambda arr: pl.BlockSpec(arr.shape, lambda i, j: (0, 0))
    single = lambda arr: pl.BlockSpec(arr.shape, lambda i, j: (0, 0), pipeline_mode=pl.Buffered(1))
    tile = lambda arr: pl.BlockSpec((None, tm, arr.shape[-1]), lambda i, j: (i, j, 0))
    body = functools.partial(_mixffn_body, col_block=MXU_TILE, apply_final=apply_final)
    return pl.pallas_call(
        body,
        grid=(b, t // tm),
        in_specs=[tile(x3), tile(ya3), tile(yb3), tile(gate3), single(rp), single(gp), single(wo),
                  full(gain), single(up), full(cw), single(down), full(final_g)],
        out_specs=pl.BlockSpec((None, tm, d), lambda i, j: (i, j, 0)),
        out_shape=jax.ShapeDtypeStruct((b, t, d), F32),
        scratch_shapes=[pltpu.VMEM((SUBLANES, 2 * hidden), F32)],
        compiler_params=pltpu.CompilerParams(dimension_semantics=("arbitrary", "arbitrary"),
                                             vmem_limit_bytes=VMEM_LIMIT),
        name="mixffn",
    )(x3, ya3, yb3, gate3, rp, gp, wo, gain, up, cw, down, final_g)


def _row(vec):
    return vec.reshape(1, -1).astype(F32)


def _pad_lanes(vec, offset=0):
    out = jnp.zeros((1, LANES), F32)
    return lax.dynamic_update_slice(out, vec.reshape(1, -1).astype(F32), (0, offset))


def kernel(x, norm1_g, w_in, rwkv_mu, rwkv_w0, rwkv_w2, rwkv_a0, rwkv_a2, rwkv_g2, rwkv_k_k, rwkv_k_a,
           rwkv_r_k, rwkv_ln_w, rwkv_ln_b, rwkv_proj, gdn_conv_w, gdn_a_log, gdn_dt_bias, gdn_norm_w,
           gdn_proj, w_out, norm2_g, ffn_up, ffn_conv_w, ffn_down, final_g):
    b, t, d = x.shape
    n = b * t
    depth = norm1_g.shape[0]
    tm = 256
    tm_in = 512
    bb = next(c for c in (8, 4, 2, 1) if b % c == 0)
    rwkv_rows = 2 * CHUNK if t % (2 * CHUNK) == 0 else CHUNK
    assert t % tm_in == 0 and t % tm == 0 and t % CHUNK == 0, "sequence length must be a multiple of the tiles"
    assert (bb * CHUNK) % min(bb * CHUNK, MXU_TILE) == 0

    blk = jnp.arange(MXU_TILE) // CHUNK
    maskf = (blk[:, None] == blk[None, :]).astype(F32)
    mask16 = maskf.astype(BF16)
    rr = jnp.arange(min(bb * CHUNK, MXU_TILE))
    tril16 =((rr[:, None] >= rr[None, :]) & (rr[:, None] // CHUNK == rr[None, :] // CHUNK)).astype(BF16)
    hmask16 = (jnp.arange(GDN_HEADS * CHUNK)[:, None] // CHUNK == jnp.arange(GDN_WIDTH)[None, :] // GDN_HEAD_DIM).astype(BF16)

    widths = (RWKV_SHIFT_WIDTH, 3 * GDN_WIDTH, GDN_WIDTH, AB_PAD, 2 * d)
    dtypes = (BF16, BF16, BF16, F32, BF16)

    for l in range(depth):
        p_rwkv, qkv, z, ab, gate = _inproj(x.reshape(n, d), _row(norm1_g[l]), w_in[l].astype(BF16),
                                           gdn_conv_w[l].astype(F32), widths, dtypes, tm_in, t // tm_in)

        rwkv_params = (_row(rwkv_mu[l]), _row(rwkv_w0[l]), rwkv_w2[l].astype(BF16), _row(rwkv_a0[l]),
                       rwkv_a2[l].astype(BF16), rwkv_g2[l].astype(BF16), _row(rwkv_k_k[l]), _row(rwkv_k_a[l]),
                       _row(rwkv_r_k[l]), _row(rwkv_ln_w[l]), _row(rwkv_ln_b[l]))
        y_a = _rwkv(p_rwkv.reshape(b, t, -1), rwkv_params, mask16, maskf, tril16, bb, rwkv_rows)

        y_b = _gdn(qkv.reshape(b, t, -1), z.reshape(b, t, -1), ab.reshape(b, t, -1),
                   _pad_lanes(gdn_a_log[l]), _pad_lanes(gdn_dt_bias[l]), _row(gdn_norm_w[l]), mask16, hmask16, tril16, bb)

        x = _mixffn(x, y_a, y_b, gate.reshape(b, t, -1), rwkv_proj[l].astype(BF16), gdn_proj[l].astype(BF16),
                    w_out[l].astype(BF16), _row(norm2_g[l]), ffn_up[l].astype(BF16), ffn_conv_w[l].astype(F32),
                    ffn_down[l].astype(BF16), _row(final_g), tm, apply_final=(l == depth - 1))
    return x
```

```python
import functools
import math

import jax
import jax.numpy as jnp
from jax import lax
from jax.experimental import pallas as pl
from jax.experimental.pallas import tpu as pltpu

F32 = jnp.float32
BF16 = jnp.bfloat16

CHUNK = 64
RWKV_HEADS = 8
RWKV_HEAD_DIM = 64
RWKV_WIDTH = RWKV_HEADS * RWKV_HEAD_DIM
DECAY_LORA = 64
ICLR_LORA = 64
GATE_LORA = 128
GDN_HEADS = 4
GDN_HEAD_DIM = 128
GDN_WIDTH = GDN_HEADS * GDN_HEAD_DIM
GDN_CONV = 4
FFN_CONV = 3
NORM_EPS = 1e-6
L2_EPS = 1e-6
RWKV_GN_EPS = 64e-5
RWKV_SHIFT_WIDTH = 3 * RWKV_WIDTH + DECAY_LORA + ICLR_LORA + GATE_LORA
LANES = 128
SUBLANES = 8
MXU_TILE = 256
AB_PAD = LANES
GROUP = MXU_TILE // CHUNK
RWKV_GROUPS = RWKV_WIDTH // MXU_TILE

VMEM_LIMIT = 52 * 1024 * 1024


def _sigmoid(x):
    return 1.0 / (1.0 + jnp.exp(-x))


def _softplus(x):
    return jnp.maximum(x, 0.0) + jnp.log(1.0 + jnp.exp(-jnp.abs(x)))


def _dot(a, b):
    return jnp.dot(a.astype(BF16), b.astype(BF16), preferred_element_type=F32)


def _dot_nt(a, b):
    return lax.dot_general(a.astype(BF16), b.astype(BF16), (((1,), (1,)), ((), ())), preferred_element_type=F32)


def _dot_tn(a, b):
    return lax.dot_general(a.astype(BF16), b.astype(BF16), (((0,), (0,)), ((), ())), preferred_element_type=F32)


def _bdiag(x, mask16):
    reps = MXU_TILE // x.shape[0]
    return jnp.concatenate([x.astype(BF16)] * reps, axis=0) * mask16


def _chunk_cumsum(tril16, x16):
    tr = tril16.shape[0]
    return jnp.concatenate([jnp.dot(tril16, x16[r0:r0 + tr], preferred_element_type=F32)
                            for r0 in range(0, x16.shape[0], tr)], axis=0)


def _cat_masks():
    ri = lax.broadcasted_iota(jnp.int32, (CHUNK, MXU_TILE), 0)
    ci = lax.broadcasted_iota(jnp.int32, (CHUNK, MXU_TILE), 1) % CHUNK
    return ri > ci, ri >= ci, (ri == ci).astype(F32)


def _unit_lower_inverse_cat(ms, eye_cat, mask16):
    ts = [eye_cat + m for m in ms]
    mks = [_dot(m, _bdiag(m, mask16)) for m in ms]
    levels = CHUNK.bit_length() - 1
    for _ in range(levels - 2):
        boths = [_dot(jnp.concatenate([mk, t], axis=0), _bdiag(mk, mask16)) for mk, t in zip(mks, ts)]
        mks = [both[:CHUNK] for both in boths]
        ts = [t + both[CHUNK:] for t, both in zip(ts, boths)]
    return [t + _dot(t, _bdiag(mk, mask16)) for t, mk in zip(ts, mks)]


def _inproj_body(x_ref, g_ref, w_ref, cw_ref, o_rwkv, o_qkv, o_z, o_ab, o_gate, wg_ref, tail_ref, *, tiles_per_seq):
    ab_start = o_rwkv.shape[-1] + o_qkv.shape[-1] + o_z.shape[-1]
    gate_start = w_ref.shape[-1] - o_gate.shape[-1]

    @pl.when(pl.program_id(0) == 0)
    def _():
        wg_ref[...] = w_ref[:, gate_start:]

    x = x_ref[...]
    ms = jnp.mean(x * x, axis=-1, keepdims=True)
    u = (x * lax.rsqrt(ms + NORM_EPS) * g_ref[...]).astype(BF16)
    tm = x_ref.shape[0]
    w_r, w_q, w_z = o_rwkv.shape[-1], o_qkv.shape[-1], o_z.shape[-1]

    @pl.when(pl.program_id(0) % tiles_per_seq == 0)
    def _():
        tail_ref[...] = jnp.zeros_like(tail_ref)

    def conv_silu(h, lo):
        cols = slice(lo, lo + MXU_TILE)
        hp = jnp.concatenate([tail_ref[:, cols], h], axis=0)
        conv = h * cw_ref[GDN_CONV - 1:GDN_CONV, cols]
        for i in range(GDN_CONV - 1):
            s = SUBLANES - (GDN_CONV - 1) + i
            conv = conv + hp[s:s + tm, :] * cw_ref[i:i + 1, cols]
        tail_ref[:, cols] = h[tm - SUBLANES:tm, :]
        o_qkv[:, cols] = (conv * _sigmoid(conv)).astype(o_qkv.dtype)

    def project(ref, wref, out_lo, w_lo):
        ref[:, out_lo:out_lo + MXU_TILE] = jnp.dot(
            u, wref[:, w_lo:w_lo + MXU_TILE], preferred_element_type=F32).astype(ref.dtype)

    plain = [functools.partial(project, o_rwkv, w_ref, k, k) for k in range(0, w_r, MXU_TILE)]
    plain += [functools.partial(project, o_z, w_ref, k, w_r + w_q + k) for k in range(0, w_z, MXU_TILE)]
    plain += [functools.partial(project, o_gate, wg_ref, k, k) for k in range(0, o_gate.shape[-1], MXU_TILE)]
    qcol = lambda lo: jnp.dot(u, w_ref[:, w_r + lo:w_r + lo + MXU_TILE], preferred_element_type=F32)
    hq = qcol(0)
    for lo in range(0, w_q, MXU_TILE):
        nxt = qcol(lo + MXU_TILE) if lo + MXU_TILE < w_q else None
        for _ in range(2):
            plain.pop(0)()
        conv_silu(hq, lo)
        hq = nxt
    for emit in plain:
        emit()
    o_ab[...] = jnp.dot(u, w_ref[:, ab_start:ab_start + o_ab.shape[-1]], preferred_element_type=F32)


def _inproj(x2, gain, w16, cw, widths, dtypes, tm, tiles_per_seq):
    n, d = x2.shape
    return pl.pallas_call(
        functools.partial(_inproj_body, tiles_per_seq=tiles_per_seq),
        grid=(n // tm,),
        in_specs=[
            pl.BlockSpec((tm, d), lambda i: (i, 0)),
            pl.BlockSpec((1, d), lambda i: (0, 0)),
            pl.BlockSpec(w16.shape, lambda i: (0, 0), pipeline_mode=pl.Buffered(1)),
            pl.BlockSpec(cw.shape, lambda i: (0, 0)),
        ],
        out_specs=[pl.BlockSpec((tm, w), lambda i: (i, 0)) for w in widths],
        out_shape=[jax.ShapeDtypeStruct((n, w), dt) for w, dt in zip(widths, dtypes)],
        scratch_shapes=[pltpu.VMEM((d, widths[-1]), BF16), pltpu.VMEM((SUBLANES, widths[1]), F32)],
        compiler_params=pltpu.CompilerParams(dimension_semantics=("arbitrary",), vmem_limit_bytes=VMEM_LIMIT),
        name="inproj",
    )(x2, gain, w16, cw)


def _rwkv_body(p_ref, *refs):
    *const_refs, y_ref, prev_ref, s_ref = refs

    @pl.when(pl.program_id(1) == 0)
    def _():
        prev_ref[...] = jnp.zeros_like(prev_ref)
        s_ref[...] = jnp.zeros_like(s_ref)

    for r0 in range(0, p_ref.shape[1], CHUNK):
        _rwkv_chunk(p_ref.at[:, r0:r0 + CHUNK, :], *const_refs, y_ref.at[:, r0:r0 + CHUNK, :], prev_ref, s_ref)


def _rwkv_chunk(p_ref, mu_ref, w0_ref, w2_ref, a0_ref, a2_ref, g2_ref, kk_ref, ka_ref, rk_ref, lnw_ref,
                lnb_ref, m16_ref, mf_ref, tril_ref, y_ref, prev_ref, s_ref):
    bb, c, wd = p_ref.shape
    rows = bb * c
    hd = RWKV_HEAD_DIM
    row8 =lax.broadcasted_iota(jnp.int32, (SUBLANES, 1), 0)
    ps, shifts = [], []
    for bi in range(bb):
        pb_ = p_ref[bi].astype(F32)
        rolled = pltpu.roll(pb_, 1, 0)
        shifts += [jnp.where(row8 == 0, prev_ref[bi, 0:1, :], rolled[:SUBLANES]), rolled[SUBLANES:]]
        prev_ref[bi, 0:1, :] = pb_[c - 1:c, :]
        ps.append(pb_)
    p = jnp.concatenate(ps, axis=0)
    pm = p + (jnp.concatenate(shifts, axis=0) - p) * mu_ref[...]

    w = RWKV_WIDTH
    r = pm[:, 0:w]
    k = pm[:, w:2 * w]
    v = pm[:, 2 * w:3 * w]
    o0 = 3 * w
    wl = pm[:, o0:o0 + DECAY_LORA]
    al = pm[:, o0 + DECAY_LORA:o0 + DECAY_LORA + ICLR_LORA]
    gl = pm[:, o0 + DECAY_LORA + ICLR_LORA:o0 + DECAY_LORA + ICLR_LORA + GATE_LORA]

    z = w0_ref[...] + _dot(jnp.tanh(wl), w2_ref[...])
    lw = (-math.exp(-0.5) * _sigmoid(z)).astype(BF16).astype(F32)
    a = _sigmoid(a0_ref[...] + _dot(al, a2_ref[...]))
    g = _dot(_sigmoid(gl), g2_ref[...])

    mask16 = m16_ref[...]
    maskf = mf_ref[...]

    def head_sums(*xs):
        stacked = jnp.concatenate([x[:, q * MXU_TILE:(q + 1) * MXU_TILE] for x in xs for q in range(RWKV_GROUPS)], axis=0)
        res = jnp.dot(stacked.astype(BF16), mask16, preferred_element_type=F32)
        outs = []
        for i in range(len(xs)):
            base = i * RWKV_GROUPS * rows
            outs.append(jnp.concatenate([res[base + q * rows:base + (q + 1) * rows] for q in range(RWKV_GROUPS)], axis=1))
        return outs

    kx = k * kk_ref[...]
    k2 = k * (1.0 + (a - 1.0) * ka_ref[...])
    kx_ss, rk_sum = head_sums(kx * kx, r * k2 * rk_ref[...])
    kk = kx * lax.rsqrt(kx_ss + L2_EPS)

    cum = _chunk_cumsum(tril_ref[...], lw.astype(BF16))
    e_pos = jnp.exp(cum)
    e_neg = 1.0 / e_pos
    e_prev = jnp.exp(cum - lw)
    w_tot = [e_pos[(bi + 1) * c - 1:(bi + 1) * c, :] for bi in range(bb)]
    e_rest = jnp.concatenate([e_neg[bi * c:(bi + 1) * c] * w_tot[bi] for bi in range(bb)], axis=0)

    b_vec = kk * a
    a_t = (-kk * e_prev).astype(BF16)
    r_t = (r * e_pos).astype(BF16)
    b_t = (b_vec * e_neg).astype(BF16)
    k_t = (k2 * e_neg).astype(BF16)
    b_h = (b_vec * e_rest).astype(BF16)
    k_h = (k2 * e_rest).astype(BF16)
    v16 = v.astype(BF16)

    strict_cat, incl_cat, eye_cat = _cat_masks()
    mask2 = jnp.concatenate([strict_cat, incl_cat], axis=0)

    chains = [(bi, q) for bi in range(bb) for q in range(RWKV_GROUPS)]
    rs_of = lambda bi: slice(bi * c, (bi + 1) * c)
    gs_of = lambda q: slice(q * MXU_TILE, (q + 1) * MXU_TILE)
    s_bds = [s_ref[bi, q] for bi, q in chains]
    pxs = []
    for (bi, q), s_bd in zip(chains, s_bds):
        rs, gs = rs_of(bi), gs_of(q)
        l2 = jnp.concatenate([a_t[rs, gs], r_t[rs, gs]], axis=0)
        w3 = jnp.concatenate([_bdiag(b_t[rs, gs], mask16), _bdiag(k_t[rs, gs], mask16), s_bd.astype(BF16)], axis=0)
        pxs.append(_dot_nt(l2, w3))
    pbs = [jnp.where(mask2, px[:, 0:MXU_TILE], 0.0) for px in pxs]
    pks = [jnp.where(mask2, px[:, MXU_TILE:2 * MXU_TILE], 0.0) for px in pxs]
    xas = [px[:, 2 * MXU_TILE:] + _dot(pk, _bdiag(v16[rs_of(bi), gs_of(q)], mask16))
           for (bi, q), px, pk in zip(chains, pxs, pks)]
    tinvs = _unit_lower_inverse_cat([pb[:c] for pb in pbs], eye_cat, mask16)
    us = [_dot(tinv, _bdiag(xa[:c], mask16)) for tinv, xa in zip(tinvs, xas)]
    ys = [xa[c:] + _dot(pb[c:], _bdiag(u, mask16)) for xa, pb, u in zip(xas, pbs, us)]
    for (bi, q), s_bd, u in zip(chains, s_bds, us):
        rs, gs = rs_of(bi), gs_of(q)
        uv = jnp.concatenate([u.astype(BF16), v16[rs, gs]], axis=0)
        bk = jnp.concatenate([b_h[rs, gs], k_h[rs, gs]], axis=0)
        s_ref[bi, q] = s_bd * w_tot[bi][:, gs] + _dot_tn(uv, bk) * maskf
    y_rows = [jnp.concatenate(ys[bi * RWKV_GROUPS:(bi + 1) * RWKV_GROUPS], axis=1) for bi in range(bb)]

    y = jnp.concatenate(y_rows, axis=0)
    inv_hd = 1.0 / hd
    (mean,) = head_sums(y)
    yc = y - mean * inv_hd
    (var,) = head_sums(yc * yc)
    yn = yc * lax.rsqrt(var * inv_hd + RWKV_GN_EPS) * lnw_ref[...] + lnb_ref[...]
    out = (yn + rk_sum * v) * g
    for bi in range(bb):
        y_ref[bi] = out[bi * c:(bi + 1) * c, :].astype(y_ref.dtype)


def _rwkv(p3, params, mask16, maskf, tril16, bb, rows_per_step):
    b, t, wd = p3.shape
    c = rows_per_step
    full = lambda arr: pl.BlockSpec(arr.shape, lambda i, j: (0,) * arr.ndim)
    return pl.pallas_call(
        _rwkv_body,
        grid=(b // bb, t // c),
        in_specs=[pl.BlockSpec((bb, c, wd), lambda i, j: (i, j, 0))] + [full(a) for a in params]
        + [full(mask16), full(maskf), full(tril16)],
        out_specs=pl.BlockSpec((bb, c, RWKV_WIDTH), lambda i, j: (i, j, 0)),
        out_shape=jax.ShapeDtypeStruct((b, t, RWKV_WIDTH), BF16),
        scratch_shapes=[
            pltpu.VMEM((bb, SUBLANES, wd), F32),
            pltpu.VMEM((bb, RWKV_GROUPS, MXU_TILE, MXU_TILE), F32),
        ],
        compiler_params=pltpu.CompilerParams(dimension_semantics=("arbitrary", "arbitrary"),
                                             vmem_limit_bytes=VMEM_LIMIT),
        name="rwkv",
    )(p3, *params, mask16, maskf, tril16)


def _gdn_body(qkv_ref, z_ref, ab_ref, *refs):
    *const_refs, y_ref, s_ref = refs

    @pl.when(pl.program_id(1) == 0)
    def _():
        s_ref[...] = jnp.zeros_like(s_ref)

    for r0 in range(0, qkv_ref.shape[1], CHUNK):
        rs = slice(r0, r0 + CHUNK)
        _gdn_chunk(qkv_ref.at[:, rs, :], z_ref.at[:, rs, :], ab_ref.at[:, rs, :], *const_refs, y_ref.at[:, rs, :], s_ref)


def _gdn_chunk(qkv_ref, z_ref, ab_ref, alog_ref, dtb_ref, nw_ref, m16_ref, hm16_ref, tril_ref, y_ref, s_ref):
    bb, c, _ = qkv_ref.shape
    hd = GDN_HEAD_DIM
    nh = GDN_HEADS
    qkv = jnp.concatenate([qkv_ref[bi] for bi in range(bb)], axis=0).astype(F32)
    w = GDN_WIDTH

    ab = jnp.concatenate([ab_ref[bi] for bi in range(bb)], axis=0)
    g_all = -jnp.exp(alog_ref[...]) * _softplus(ab + dtb_ref[...])
    beta_all = _sigmoid(ab)

    g_hi = g_all.astype(BF16)
    rem = g_all - g_hi.astype(F32)
    g_mid = rem.astype(BF16)
    g_lo = (rem - g_mid.astype(F32)).astype(BF16)
    gcp = _chunk_cumsum(tril_ref[...], jnp.concatenate([g_hi, g_mid, g_lo], axis=1))
    gc_all = gcp[:, 0:LANES] + gcp[:, LANES:2 * LANES] + gcp[:, 2 * LANES:3 * LANES]
    eg_all = jnp.exp(gc_all)

    mask16 = m16_ref[...]
    hmask16 = hm16_ref[...]
    strict_cat, incl_cat, eye_cat = _cat_masks()
    lane_head = lax.broadcasted_iota(jnp.int32, (c, MXU_TILE), 1) // CHUNK

    qs, ks, kbs = [], [], []
    for h in range(nh):
        qh = qkv[:, h * hd:(h + 1) * hd]
        kh = qkv[:, w + h * hd:w + (h + 1) * hd]
        qs.append(qh * lax.rsqrt(jnp.sum(qh * qh, axis=-1, keepdims=True) + L2_EPS) * (hd ** -0.5))
        kh = kh * lax.rsqrt(jnp.sum(kh * kh, axis=-1, keepdims=True) + L2_EPS)
        ks.append(kh)
        kbs.append(kh * beta_all[:, nh + h:nh + h + 1])
    k_all = jnp.concatenate(ks, axis=1).astype(BF16)
    kb_all = jnp.concatenate(kbs, axis=1).astype(BF16)
    q_all = jnp.concatenate(qs, axis=1).astype(BF16)

    rs_of = lambda bi: slice(bi * c, (bi + 1) * c)
    gc_bs = [gc_all[rs_of(bi)] for bi in range(bb)]
    glasts = [gc_b[c - 1:c, :] for gc_b in gc_bs]
    erests = [jnp.exp(glast - gc_b) for glast, gc_b in zip(glasts, gc_bs)]
    elasts = [jnp.exp(glast) for glast in glasts]

    decays = []
    for gc_b in gc_bs:
        f_cat = jnp.zeros((c, MXU_TILE), F32)
        for h in range(nh):
            f_cat = jnp.where(lane_head == h, gc_b[:, h:h + 1], f_cat)
        r_cat = jnp.sum(f_cat * eye_cat, axis=0, keepdims=True)
        decays.append(jnp.where(incl_cat, jnp.exp(jnp.where(incl_cat, f_cat - r_cat, 0.0)), 0.0))

    kqs = [_dot_nt(jnp.concatenate([kb_all[rs_of(bi)], q_all[rs_of(bi)]], axis=0),
                   jnp.concatenate([k_all[rs_of(bi)]] * nh, axis=0) * hmask16) for bi in range(bb)]
    attns = [kq[c:] * decay for kq, decay in zip(kqs, decays)]
    tinvs = _unit_lower_inverse_cat([-jnp.where(strict_cat, kq[:c] * decay, 0.0) for kq, decay in zip(kqs, decays)],
                                    eye_cat, mask16)

    chains = [(bi, h) for bi in range(bb) for h in range(nh)]
    tl_of = lambda h: slice(h * c, (h + 1) * c)
    egs = [eg_all[rs_of(bi), h:h + 1] for bi, h in chains]
    uws = []
    for (bi, h), eg in zip(chains, egs):
        rs = rs_of(bi)
        vh = qkv[rs, 2 * w + h * hd:2 * w + (h + 1) * hd]
        beta = beta_all[rs, nh + h:nh + h + 1]
        uws.append(_dot(tinvs[bi][:, tl_of(h)], jnp.concatenate([vh * beta, kbs[h][rs] * eg], axis=1)))
    ss = [s_ref[bi, h] for bi, h in chains]
    wqs = [_dot(jnp.concatenate([uw[:, hd:], qs[h][rs_of(bi)] * eg], axis=0), s)
           for (bi, h), uw, eg, s in zip(chains, uws, egs, ss)]
    v_news = [uw[:, :hd] - wq[:c] for uw, wq in zip(uws, wqs)]
    os_ = [wq[c:] + _dot(attns[bi][:, tl_of(h)], v_new) for (bi, h), wq, v_new in zip(chains, wqs, v_news)]
    for (bi, h), s, v_new in zip(chains, ss, v_news):
        s_ref[bi, h] = s * elasts[bi][:, h:h + 1] + _dot_tn(ks[h][rs_of(bi)] * erests[bi][:, h:h + 1], v_new)
    for (bi, h), o in zip(chains, os_):
        sl = slice(h * hd, (h + 1) * hd)
        on = o * lax.rsqrt(jnp.mean(o * o, axis=-1, keepdims=True) + NORM_EPS) * nw_ref[...]
        zh = z_ref[bi, :, sl].astype(F32)
        y_ref[bi, :, sl] = (on * (zh * _sigmoid(zh))).astype(y_ref.dtype)


def _gdn(qkv3, z3, ab3, alog, dtb, nw, mask16, hmask16, tril16, bb, rows_per_step):
    b, t, wd = qkv3.shape
    c = rows_per_step
    full = lambda arr: pl.BlockSpec(arr.shape, lambda i, j: (0,) * arr.ndim)
    return pl.pallas_call(
        _gdn_body,
        grid=(b // bb, t // c),
        in_specs=[
            pl.BlockSpec((bb, c, wd), lambda i, j: (i, j, 0)),
            pl.BlockSpec((bb, c, GDN_WIDTH), lambda i, j: (i, j, 0)),
            pl.BlockSpec((bb, c, AB_PAD), lambda i, j: (i, j, 0)),
            full(alog), full(dtb), full(nw), full(mask16), full(hmask16), full(tril16),
        ],
        out_specs=pl.BlockSpec((bb, c, GDN_WIDTH), lambda i, j: (i, j, 0)),
        out_shape=jax.ShapeDtypeStruct((b, t, GDN_WIDTH), BF16),
        scratch_shapes=[pltpu.VMEM((bb, GDN_HEADS, GDN_HEAD_DIM, GDN_HEAD_DIM), F32)],
        compiler_params=pltpu.CompilerParams(dimension_semantics=("arbitrary", "arbitrary"),
                                             vmem_limit_bytes=VMEM_LIMIT),
        name="gdn",
    )(qkv3, z3, ab3, alog, dtb, nw, mask16, hmask16, tril16)


def _mixffn_body(x_ref, ya_ref, yb_ref, gate_ref, rp_ref, gp_ref, wo_ref, g_ref, up_ref, cw_ref, down_ref, fg_ref,
                 o_ref, prev_ref, *, col_block, apply_final):
    tm, d = x_ref.shape
    hidden = down_ref.shape[0]

    @pl.when(pl.program_id(1) == 0)
    def _():
        prev_ref[...] = jnp.zeros_like(prev_ref)

    ya = jnp.dot(ya_ref[...], rp_ref[...], preferred_element_type=F32)
    yb = jnp.dot(yb_ref[...], gp_ref[...], preferred_element_type=F32)
    mixed = _sigmoid(gate_ref[:, :d].astype(F32)) * ya + _sigmoid(gate_ref[:, d:].astype(F32)) * yb
    x = x_ref[...] + jnp.dot(mixed.astype(BF16), wo_ref[...], preferred_element_type=F32)

    ms = jnp.mean(x * x, axis=-1, keepdims=True)
    xn = (x * lax.rsqrt(ms + NORM_EPS) * g_ref[...]).astype(BF16)

    def up_cols(lo):
        return jnp.dot(xn, up_ref[:, lo:lo + col_block], preferred_element_type=F32)

    def conv_cols(h, lo):
        hp = jnp.concatenate([prev_ref[:, lo:lo + col_block], h], axis=0)
        out = h * cw_ref[FFN_CONV - 1:FFN_CONV, lo:lo + col_block]
        for i in range(FFN_CONV - 1):
            s = SUBLANES - (FFN_CONV - 1) + i
            out = out + hp[s:s + tm, :] * cw_ref[i:i + 1, lo:lo + col_block]
        prev_ref[:, lo:lo + col_block] = h[tm - SUBLANES:tm, :]
        return out

    nblk = hidden // col_block
    acc = x
    hs = (up_cols(0), up_cols(hidden))
    for j in range(nblk):
        lo = j * col_block
        nxt = (up_cols(lo + col_block), up_cols(hidden + lo + col_block)) if j + 1 < nblk else None
        hg = conv_cols(hs[0], lo)
        hu = conv_cols(hs[1], hidden + lo)
        act = (hg * _sigmoid(hg) * hu).astype(BF16)
        acc = acc + jnp.dot(act, down_ref[lo:lo + col_block, :], preferred_element_type=F32)
        hs = nxt
    if apply_final:
        ms2 = jnp.mean(acc * acc, axis=-1, keepdims=True)
        acc = acc * lax.rsqrt(ms2 + NORM_EPS) * fg_ref[...]
    o_ref[...] = acc


def _mixffn(x3, ya3, yb3, gate3, rp, gp, wo, gain, up, cw, down, final_g, tm, apply_final):
    b, t, d = x3.shape
    hidden = down.shape[0]
    full = lambda arr: pl.BlockSpec(arr.shape, lambda i, j: (0, 0))
    single = lambda arr: pl.BlockSpec(arr.shape, lambda i, j: (0, 0), pipeline_mode=pl.Buffered(1))
    tile = lambda arr: pl.BlockSpec((None, tm, arr.shape[-1]), lambda i, j: (i, j, 0))
    body = functools.partial(_mixffn_body, col_block=MXU_TILE, apply_final=apply_final)
    return pl.pallas_call(
        body,
        grid=(b, t // tm),
        in_specs=[tile(x3), tile(ya3), tile(yb3), tile(gate3), single(rp), single(gp), single(wo),
                  full(gain), single(up), full(cw), single(down), full(final_g)],
        out_specs=pl.BlockSpec((None, tm, d), lambda i, j: (i, j, 0)),
        out_shape=jax.ShapeDtypeStruct((b, t, d), F32),
        scratch_shapes=[pltpu.VMEM((SUBLANES, 2 * hidden), F32)],
        compiler_params=pltpu.CompilerParams(dimension_semantics=("arbitrary", "arbitrary"),
                                             vmem_limit_bytes=VMEM_LIMIT),
        name="mixffn",
    )(x3, ya3, yb3, gate3, rp, gp, wo, gain, up, cw, down, final_g)


def _row(vec):
    return vec.reshape(1, -1).astype(F32)


def _pad_lanes(vec, offset=0):
    out = jnp.zeros((1, LANES), F32)
    return lax.dynamic_update_slice(out, vec.reshape(1, -1).astype(F32), (0, offset))


def kernel(x, norm1_g, w_in, rwkv_mu, rwkv_w0, rwkv_w2, rwkv_a0, rwkv_a2, rwkv_g2, rwkv_k_k, rwkv_k_a,
           rwkv_r_k, rwkv_ln_w, rwkv_ln_b, rwkv_proj, gdn_conv_w, gdn_a_log, gdn_dt_bias, gdn_norm_w,
           gdn_proj, w_out, norm2_g, ffn_up, ffn_conv_w, ffn_down, final_g):
    b, t, d = x.shape
    n = b * t
    depth = norm1_g.shape[0]
    tm = 256
    tm_in = 512
    bb = next(c for c in (8, 4, 2, 1) if b % c == 0)
    scan_rows = 2 * CHUNK if t % (2 * CHUNK) == 0 else CHUNK
    assert t % tm_in == 0 and t % tm == 0 and t % CHUNK == 0, "sequence length must be a multiple of the tiles"
    assert (bb * CHUNK) % min(bb * CHUNK, MXU_TILE) == 0

    blk = jnp.arange(MXU_TILE) // CHUNK
    maskf = (blk[:, None] == blk[None, :]).astype(F32)
    mask16 = maskf.astype(BF16)
    rr = jnp.arange(min(bb * CHUNK, MXU_TILE))
    tril16 =((rr[:, None] >= rr[None, :]) & (rr[:, None] // CHUNK == rr[None, :] // CHUNK)).astype(BF16)
    hmask16 = (jnp.arange(GDN_HEADS * CHUNK)[:, None] // CHUNK == jnp.arange(GDN_WIDTH)[None, :] // GDN_HEAD_DIM).astype(BF16)

    widths = (RWKV_SHIFT_WIDTH, 3 * GDN_WIDTH, GDN_WIDTH, AB_PAD, 2 * d)
    dtypes = (BF16, BF16, BF16, F32, BF16)

    for l in range(depth):
        p_rwkv, qkv, z, ab, gate = _inproj(x.reshape(n, d), _row(norm1_g[l]), w_in[l].astype(BF16),
                                           gdn_conv_w[l].astype(F32), widths, dtypes, tm_in, t // tm_in)

        rwkv_params = (_row(rwkv_mu[l]), _row(rwkv_w0[l]), rwkv_w2[l].astype(BF16), _row(rwkv_a0[l]),
                       rwkv_a2[l].astype(BF16), rwkv_g2[l].astype(BF16), _row(rwkv_k_k[l]), _row(rwkv_k_a[l]),
                       _row(rwkv_r_k[l]), _row(rwkv_ln_w[l]), _row(rwkv_ln_b[l]))
        y_a = _rwkv(p_rwkv.reshape(b, t, -1), rwkv_params, mask16, maskf, tril16, bb, scan_rows)

        y_b = _gdn(qkv.reshape(b, t, -1), z.reshape(b, t, -1), ab.reshape(b, t, -1),
                   _pad_lanes(gdn_a_log[l]), _pad_lanes(gdn_dt_bias[l]), _row(gdn_norm_w[l]), mask16, hmask16, tril16, bb,
                   scan_rows)

        x = _mixffn(x, y_a, y_b, gate.reshape(b, t, -1), rwkv_proj[l].astype(BF16), gdn_proj[l].astype(BF16),
                    w_out[l].astype(BF16), _row(norm2_g[l]), ffn_up[l].astype(BF16), ffn_conv_w[l].astype(F32),
                    ffn_down[l].astype(BF16), _row(final_g), tm, apply_final=(l == depth - 1))
    return x
```

```python
import functools
import math

import jax
import jax.numpy as jnp
from jax import lax
from jax.experimental import pallas as pl
from jax.experimental.pallas import tpu as pltpu

F32 = jnp.float32
BF16 = jnp.bfloat16

CHUNK = 64
RWKV_HEADS = 8
RWKV_HEAD_DIM = 64
RWKV_WIDTH = RWKV_HEADS * RWKV_HEAD_DIM
DECAY_LORA = 64
ICLR_LORA = 64
GATE_LORA = 128
GDN_HEADS = 4
GDN_HEAD_DIM = 128
GDN_WIDTH = GDN_HEADS * GDN_HEAD_DIM
GDN_CONV = 4
FFN_CONV = 3
NORM_EPS = 1e-6
L2_EPS = 1e-6
RWKV_GN_EPS = 64e-5
RWKV_SHIFT_WIDTH = 3 * RWKV_WIDTH + DECAY_LORA + ICLR_LORA + GATE_LORA
LANES = 128
SUBLANES = 8
MXU_TILE = 256
AB_PAD = LANES
GROUP = MXU_TILE // CHUNK
RWKV_GROUPS = RWKV_WIDTH // MXU_TILE
assert RWKV_HEAD_DIM == CHUNK and GDN_HEADS == GROUP and RWKV_HEADS % GROUP == 0

VMEM_LIMIT = 52 * 1024 * 1024


def _sigmoid(x):
    return 1.0 / (1.0 + jnp.exp(-x))


def _softplus(x):
    return jnp.maximum(x, 0.0) + jnp.log(1.0 + jnp.exp(-jnp.abs(x)))


def _dot(a, b):
    return jnp.dot(a.astype(BF16), b.astype(BF16), preferred_element_type=F32)


def _dot_nt(a, b):
    return lax.dot_general(a.astype(BF16), b.astype(BF16), (((1,), (1,)), ((), ())), preferred_element_type=F32)


def _dot_tn(a, b):
    return lax.dot_general(a.astype(BF16), b.astype(BF16), (((0,), (0,)), ((), ())), preferred_element_type=F32)


def _bdiag(x, mask16):
    reps = MXU_TILE // x.shape[0]
    return jnp.concatenate([x.astype(BF16)] * reps, axis=0) * mask16


def _chunk_cumsum(tril16, x16):
    tr = tril16.shape[0]
    return jnp.concatenate([jnp.dot(tril16, x16[r0:r0 + tr], preferred_element_type=F32)
                            for r0 in range(0, x16.shape[0], tr)], axis=0)


def _cat_masks():
    ri = lax.broadcasted_iota(jnp.int32, (CHUNK, MXU_TILE), 0)
    ci = lax.broadcasted_iota(jnp.int32, (CHUNK, MXU_TILE), 1) % CHUNK
    return ri > ci, ri >= ci, (ri == ci).astype(F32)


def _unit_lower_inverse_cat(ms, eye_cat, mask16):
    ts = [eye_cat + m for m in ms]
    mks = [_dot(m, _bdiag(m, mask16)) for m in ms]
    levels = CHUNK.bit_length() - 1
    for _ in range(levels - 2):
        boths = [_dot(jnp.concatenate([mk, t], axis=0), _bdiag(mk, mask16)) for mk, t in zip(mks, ts)]
        mks = [both[:CHUNK] for both in boths]
        ts = [t + both[CHUNK:] for t, both in zip(ts, boths)]
    return [t + _dot(t, _bdiag(mk, mask16)) for t, mk in zip(ts, mks)]


def _inproj_body(x_ref, g_ref, w_ref, cw_ref, o_rwkv, o_qkv, o_z, o_ab, o_gate, wg_ref, tail_ref, *, tiles_per_seq):
    ab_start = o_rwkv.shape[-1] + o_qkv.shape[-1] + o_z.shape[-1]
    gate_start = w_ref.shape[-1] - o_gate.shape[-1]

    @pl.when(pl.program_id(0) == 0)
    def _():
        wg_ref[...] = w_ref[:, gate_start:]

    x = x_ref[...]
    ms = jnp.mean(x * x, axis=-1, keepdims=True)
    u = (x * lax.rsqrt(ms + NORM_EPS) * g_ref[...]).astype(BF16)
    tm = x_ref.shape[0]
    w_r, w_q, w_z = o_rwkv.shape[-1], o_qkv.shape[-1], o_z.shape[-1]

    @pl.when(pl.program_id(0) % tiles_per_seq == 0)
    def _():
        tail_ref[...] = jnp.zeros_like(tail_ref)

    def conv_silu(h, lo):
        cols = slice(lo, lo + MXU_TILE)
        hp = jnp.concatenate([tail_ref[:, cols], h], axis=0)
        conv = h * cw_ref[GDN_CONV - 1:GDN_CONV, cols]
        for i in range(GDN_CONV - 1):
            s = SUBLANES - (GDN_CONV - 1) + i
            conv = conv + hp[s:s + tm, :] * cw_ref[i:i + 1, cols]
        tail_ref[:, cols] = h[tm - SUBLANES:tm, :]
        o_qkv[:, cols] = (conv * _sigmoid(conv)).astype(o_qkv.dtype)

    def project(ref, wref, out_lo, w_lo):
        ref[:, out_lo:out_lo + MXU_TILE] = jnp.dot(
            u, wref[:, w_lo:w_lo + MXU_TILE], preferred_element_type=F32).astype(ref.dtype)

    plain = [functools.partial(project, o_rwkv, w_ref, k, k) for k in range(0, w_r, MXU_TILE)]
    plain += [functools.partial(project, o_z, w_ref, k, w_r + w_q + k) for k in range(0, w_z, MXU_TILE)]
    plain += [functools.partial(project, o_gate, wg_ref, k, k) for k in range(0, o_gate.shape[-1], MXU_TILE)]
    qcol = lambda lo: jnp.dot(u, w_ref[:, w_r + lo:w_r + lo + MXU_TILE], preferred_element_type=F32)
    hq = qcol(0)
    for lo in range(0, w_q, MXU_TILE):
        nxt = qcol(lo + MXU_TILE) if lo + MXU_TILE < w_q else None
        for _ in range(2):
            plain.pop(0)()
        conv_silu(hq, lo)
        hq = nxt
    for emit in plain:
        emit()
    o_ab[...] = jnp.dot(u, w_ref[:, ab_start:ab_start + o_ab.shape[-1]], preferred_element_type=F32)


def _inproj(x2, gain, w16, cw, widths, dtypes, tm, tiles_per_seq):
    n, d = x2.shape
    return pl.pallas_call(
        functools.partial(_inproj_body, tiles_per_seq=tiles_per_seq),
        grid=(n // tm,),
        in_specs=[
            pl.BlockSpec((tm, d), lambda i: (i, 0)),
            pl.BlockSpec((1, d), lambda i: (0, 0)),
            pl.BlockSpec(w16.shape, lambda i: (0, 0), pipeline_mode=pl.Buffered(1)),
            pl.BlockSpec(cw.shape, lambda i: (0, 0)),
        ],
        out_specs=[pl.BlockSpec((tm, w), lambda i: (i, 0)) for w in widths],
        out_shape=[jax.ShapeDtypeStruct((n, w), dt) for w, dt in zip(widths, dtypes)],
        scratch_shapes=[pltpu.VMEM((d, widths[-1]), BF16), pltpu.VMEM((SUBLANES, widths[1]), F32)],
        compiler_params=pltpu.CompilerParams(dimension_semantics=("arbitrary",), vmem_limit_bytes=VMEM_LIMIT),
        name="inproj",
    )(x2, gain, w16, cw)


def _rwkv_body(p_ref, *refs):
    *const_refs, y_ref, prev_ref, s_ref = refs

    @pl.when(pl.program_id(1) == 0)
    def _():
        prev_ref[...] = jnp.zeros_like(prev_ref)
        s_ref[...] = jnp.zeros_like(s_ref)

    for r0 in range(0, p_ref.shape[1], CHUNK):
        _rwkv_chunk(p_ref.at[:, r0:r0 + CHUNK, :], *const_refs, y_ref.at[:, r0:r0 + CHUNK, :], prev_ref, s_ref)


def _rwkv_chunk(p_ref, mu_ref, w0_ref, w2_ref, a0_ref, a2_ref, g2_ref, kk_ref, ka_ref, rk_ref, lnw_ref,
                lnb_ref, m16_ref, mf_ref, tril_ref, y_ref, prev_ref, s_ref):
    bb, c, wd = p_ref.shape
    rows = bb * c
    hd = RWKV_HEAD_DIM
    row8 =lax.broadcasted_iota(jnp.int32, (SUBLANES, 1), 0)
    ps, shifts = [], []
    for bi in range(bb):
        pb_ = p_ref[bi].astype(F32)
        rolled = pltpu.roll(pb_, 1, 0)
        shifts += [jnp.where(row8 == 0, prev_ref[bi, 0:1, :], rolled[:SUBLANES]), rolled[SUBLANES:]]
        prev_ref[bi, 0:1, :] = pb_[c - 1:c, :]
        ps.append(pb_)
    p = jnp.concatenate(ps, axis=0)
    pm = p + (jnp.concatenate(shifts, axis=0) - p) * mu_ref[...]

    w = RWKV_WIDTH
    r = pm[:, 0:w]
    k = pm[:, w:2 * w]
    v = pm[:, 2 * w:3 * w]
    o0 = 3 * w
    wl = pm[:, o0:o0 + DECAY_LORA]
    al = pm[:, o0 + DECAY_LORA:o0 + DECAY_LORA + ICLR_LORA]
    gl = pm[:, o0 + DECAY_LORA + ICLR_LORA:o0 + DECAY_LORA + ICLR_LORA + GATE_LORA]

    z = w0_ref[...] + _dot(jnp.tanh(wl), w2_ref[...])
    lw = (-math.exp(-0.5) * _sigmoid(z)).astype(BF16).astype(F32)
    a = _sigmoid(a0_ref[...] + _dot(al, a2_ref[...]))
    g = _dot(_sigmoid(gl), g2_ref[...])

    mask16 = m16_ref[...]
    maskf = mf_ref[...]

    def head_sums(*xs):
        stacked = jnp.concatenate([x[:, q * MXU_TILE:(q + 1) * MXU_TILE] for x in xs for q in range(RWKV_GROUPS)], axis=0)
        res = jnp.dot(stacked.astype(BF16), mask16, preferred_element_type=F32)
        outs = []
        for i in range(len(xs)):
            base = i * RWKV_GROUPS * rows
            outs.append(jnp.concatenate([res[base + q * rows:base + (q + 1) * rows] for q in range(RWKV_GROUPS)], axis=1))
        return outs

    kx = k * kk_ref[...]
    k2 = k * (1.0 + (a - 1.0) * ka_ref[...])
    kx_ss, rk_sum = head_sums(kx * kx, r * k2 * rk_ref[...])
    kk = kx * lax.rsqrt(kx_ss + L2_EPS)

    cum = _chunk_cumsum(tril_ref[...], lw.astype(BF16))
    e_pos = jnp.exp(cum)
    e_neg = 1.0 / e_pos
    e_prev = jnp.exp(cum - lw)
    w_tot = [e_pos[(bi + 1) * c - 1:(bi + 1) * c, :] for bi in range(bb)]
    e_rest = jnp.concatenate([e_neg[bi * c:(bi + 1) * c] * w_tot[bi] for bi in range(bb)], axis=0)

    b_vec = kk * a
    a_t = (-kk * e_prev).astype(BF16)
    r_t = (r * e_pos).astype(BF16)
    b_t = (b_vec * e_neg).astype(BF16)
    k_t = (k2 * e_neg).astype(BF16)
    b_h = (b_vec * e_rest).astype(BF16)
    k_h = (k2 * e_rest).astype(BF16)
    v16 = v.astype(BF16)

    strict_cat, incl_cat, eye_cat = _cat_masks()
    mask2 = jnp.concatenate([strict_cat, incl_cat], axis=0)

    chains = [(bi, q) for bi in range(bb) for q in range(RWKV_GROUPS)]
    rs_of = lambda bi: slice(bi * c, (bi + 1) * c)
    gs_of = lambda q: slice(q * MXU_TILE, (q + 1) * MXU_TILE)
    s_bds = [s_ref[bi, q] for bi, q in chains]
    pxs = []
    for (bi, q), s_bd in zip(chains, s_bds):
        rs, gs = rs_of(bi), gs_of(q)
        l2 = jnp.concatenate([a_t[rs, gs], r_t[rs, gs]], axis=0)
        w3 = jnp.concatenate([_bdiag(b_t[rs, gs], mask16), _bdiag(k_t[rs, gs], mask16), s_bd.astype(BF16)], axis=0)
        pxs.append(_dot_nt(l2, w3))
    pbs = [jnp.where(mask2, px[:, 0:MXU_TILE], 0.0) for px in pxs]
    pks = [jnp.where(mask2, px[:, MXU_TILE:2 * MXU_TILE], 0.0) for px in pxs]
    xas = [px[:, 2 * MXU_TILE:] + _dot(pk, _bdiag(v16[rs_of(bi), gs_of(q)], mask16))
           for (bi, q), px, pk in zip(chains, pxs, pks)]
    tinvs = _unit_lower_inverse_cat([pb[:c] for pb in pbs], eye_cat, mask16)
    us = [_dot(tinv, _bdiag(xa[:c], mask16)) for tinv, xa in zip(tinvs, xas)]
    ys = [xa[c:] + _dot(pb[c:], _bdiag(u, mask16)) for xa, pb, u in zip(xas, pbs, us)]
    for (bi, q), s_bd, u in zip(chains, s_bds, us):
        rs, gs = rs_of(bi), gs_of(q)
        uv = jnp.concatenate([u.astype(BF16), v16[rs, gs]], axis=0)
        bk = jnp.concatenate([b_h[rs, gs], k_h[rs, gs]], axis=0)
        s_ref[bi, q] = s_bd * w_tot[bi][:, gs] + _dot_tn(uv, bk) * maskf
    y_rows = [jnp.concatenate(ys[bi * RWKV_GROUPS:(bi + 1) * RWKV_GROUPS], axis=1) for bi in range(bb)]

    y = jnp.concatenate(y_rows, axis=0)
    inv_hd = 1.0 / hd
    (mean,) = head_sums(y)
    yc = y - mean * inv_hd
    (var,) = head_sums(yc * yc)
    yn = yc * lax.rsqrt(var * inv_hd + RWKV_GN_EPS) * lnw_ref[...] + lnb_ref[...]
    out = (yn + rk_sum * v) * g
    for bi in range(bb):
        y_ref[bi] = out[bi * c:(bi + 1) * c, :].astype(y_ref.dtype)


def _rwkv(p3, params, mask16, maskf, tril16, bb, rows_per_step):
    b, t, wd = p3.shape
    c = rows_per_step
    full = lambda arr: pl.BlockSpec(arr.shape, lambda i, j: (0,) * arr.ndim)
    return pl.pallas_call(
        _rwkv_body,
        grid=(b // bb, t // c),
        in_specs=[pl.BlockSpec((bb, c, wd), lambda i, j: (i, j, 0))] + [full(a) for a in params]
        + [full(mask16), full(maskf), full(tril16)],
        out_specs=pl.BlockSpec((bb, c, RWKV_WIDTH), lambda i, j: (i, j, 0)),
        out_shape=jax.ShapeDtypeStruct((b, t, RWKV_WIDTH), BF16),
        scratch_shapes=[
            pltpu.VMEM((bb, SUBLANES, wd), F32),
            pltpu.VMEM((bb, RWKV_GROUPS, MXU_TILE, MXU_TILE), F32),
        ],
        compiler_params=pltpu.CompilerParams(dimension_semantics=("arbitrary", "arbitrary"),
                                             vmem_limit_bytes=VMEM_LIMIT),
        name="rwkv",
    )(p3, *params, mask16, maskf, tril16)


def _gdn_body(qkv_ref, z_ref, ab_ref, *refs):
    *const_refs, y_ref, s_ref = refs

    @pl.when(pl.program_id(1) == 0)
    def _():
        s_ref[...] = jnp.zeros_like(s_ref)

    for r0 in range(0, qkv_ref.shape[1], CHUNK):
        rs = slice(r0, r0 + CHUNK)
        _gdn_chunk(qkv_ref.at[:, rs, :], z_ref.at[:, rs, :], ab_ref.at[:, rs, :], *const_refs, y_ref.at[:, rs, :], s_ref)


def _gdn_chunk(qkv_ref, z_ref, ab_ref, alog_ref, dtb_ref, nw_ref, m16_ref, hm16_ref, tril_ref, y_ref, s_ref):
    bb, c, _ = qkv_ref.shape
    hd = GDN_HEAD_DIM
    nh = GDN_HEADS
    qkv = jnp.concatenate([qkv_ref[bi] for bi in range(bb)], axis=0).astype(F32)
    w = GDN_WIDTH

    ab = jnp.concatenate([ab_ref[bi] for bi in range(bb)], axis=0)
    g_all = -jnp.exp(alog_ref[...]) * _softplus(ab + dtb_ref[...])
    beta_all = _sigmoid(ab)

    g_hi = g_all.astype(BF16)
    rem = g_all - g_hi.astype(F32)
    g_mid = rem.astype(BF16)
    g_lo = (rem - g_mid.astype(F32)).astype(BF16)
    gcp = _chunk_cumsum(tril_ref[...], jnp.concatenate([g_hi, g_mid, g_lo], axis=1))
    gc_all = gcp[:, 0:LANES] + gcp[:, LANES:2 * LANES] + gcp[:, 2 * LANES:3 * LANES]
    eg_all = jnp.exp(gc_all)

    mask16 = m16_ref[...]
    hmask16 = hm16_ref[...]
    strict_cat, incl_cat, eye_cat = _cat_masks()
    lane_head = lax.broadcasted_iota(jnp.int32, (c, MXU_TILE), 1) // CHUNK

    qs, ks, kbs = [], [], []
    for h in range(nh):
        qh = qkv[:, h * hd:(h + 1) * hd]
        kh = qkv[:, w + h * hd:w + (h + 1) * hd]
        qs.append(qh * lax.rsqrt(jnp.sum(qh * qh, axis=-1, keepdims=True) + L2_EPS) * (hd ** -0.5))
        kh = kh * lax.rsqrt(jnp.sum(kh * kh, axis=-1, keepdims=True) + L2_EPS)
        ks.append(kh)
        kbs.append(kh * beta_all[:, nh + h:nh + h + 1])
    k_all = jnp.concatenate(ks, axis=1).astype(BF16)
    kb_all = jnp.concatenate(kbs, axis=1).astype(BF16)
    q_all = jnp.concatenate(qs, axis=1).astype(BF16)

    rs_of = lambda bi: slice(bi * c, (bi + 1) * c)
    gc_bs = [gc_all[rs_of(bi)] for bi in range(bb)]
    glasts = [gc_b[c - 1:c, :] for gc_b in gc_bs]
    erests = [jnp.exp(glast - gc_b) for glast, gc_b in zip(glasts, gc_bs)]
    elasts = [jnp.exp(glast) for glast in glasts]

    decays = []
    for gc_b in gc_bs:
        f_cat = jnp.zeros((c, MXU_TILE), F32)
        for h in range(nh):
            f_cat = jnp.where(lane_head == h, gc_b[:, h:h + 1], f_cat)
        r_cat = jnp.sum(f_cat * eye_cat, axis=0, keepdims=True)
        decays.append(jnp.where(incl_cat, jnp.exp(jnp.where(incl_cat, f_cat - r_cat, 0.0)), 0.0))

    kqs = [_dot_nt(jnp.concatenate([kb_all[rs_of(bi)], q_all[rs_of(bi)]], axis=0),
                   jnp.concatenate([k_all[rs_of(bi)]] * nh, axis=0) * hmask16) for bi in range(bb)]
    attns = [kq[c:] * decay for kq, decay in zip(kqs, decays)]
    tinvs = _unit_lower_inverse_cat([-jnp.where(strict_cat, kq[:c] * decay, 0.0) for kq, decay in zip(kqs, decays)],
                                    eye_cat, mask16)

    chains = [(bi, h) for bi in range(bb) for h in range(nh)]
    tl_of = lambda h: slice(h * c, (h + 1) * c)
    egs = [eg_all[rs_of(bi), h:h + 1] for bi, h in chains]
    uws = []
    for (bi, h), eg in zip(chains, egs):
        rs = rs_of(bi)
        vh = qkv[rs, 2 * w + h * hd:2 * w + (h + 1) * hd]
        beta = beta_all[rs, nh + h:nh + h + 1]
        uws.append(_dot(tinvs[bi][:, tl_of(h)], jnp.concatenate([vh * beta, kbs[h][rs] * eg], axis=1)))
    ss = [s_ref[bi, h] for bi, h in chains]
    wqs = [_dot(jnp.concatenate([uw[:, hd:], qs[h][rs_of(bi)] * eg], axis=0), s)
           for (bi, h), uw, eg, s in zip(chains, uws, egs, ss)]
    v_news = [uw[:, :hd] - wq[:c] for uw, wq in zip(uws, wqs)]
    os_ = [wq[c:] + _dot(attns[bi][:, tl_of(h)], v_new) for (bi, h), wq, v_new in zip(chains, wqs, v_news)]
    for (bi, h), s, v_new in zip(chains, ss, v_news):
        s_ref[bi, h] = s * elasts[bi][:, h:h + 1] + _dot_tn(ks[h][rs_of(bi)] * erests[bi][:, h:h + 1], v_new)
    for (bi, h), o in zip(chains, os_):
        sl = slice(h * hd, (h + 1) * hd)
        on = o * lax.rsqrt(jnp.mean(o * o, axis=-1, keepdims=True) + NORM_EPS) * nw_ref[...]
        zh = z_ref[bi, :, sl].astype(F32)
        y_ref[bi, :, sl] = (on * (zh * _sigmoid(zh))).astype(y_ref.dtype)


def _gdn(qkv3, z3, ab3, alog, dtb, nw, mask16, hmask16, tril16, bb, rows_per_step):
    b, t, wd = qkv3.shape
    c = rows_per_step
    full = lambda arr: pl.BlockSpec(arr.shape, lambda i, j: (0,) * arr.ndim)
    return pl.pallas_call(
        _gdn_body,
        grid=(b // bb, t // c),
        in_specs=[
            pl.BlockSpec((bb, c, wd), lambda i, j: (i, j, 0)),
            pl.BlockSpec((bb, c, GDN_WIDTH), lambda i, j: (i, j, 0)),
            pl.BlockSpec((bb, c, AB_PAD), lambda i, j: (i, j, 0)),
            full(alog), full(dtb), full(nw), full(mask16), full(hmask16), full(tril16),
        ],
        out_specs=pl.BlockSpec((bb, c, GDN_WIDTH), lambda i, j: (i, j, 0)),
        out_shape=jax.ShapeDtypeStruct((b, t, GDN_WIDTH), BF16),
        scratch_shapes=[pltpu.VMEM((bb, GDN_HEADS, GDN_HEAD_DIM, GDN_HEAD_DIM), F32)],
        compiler_params=pltpu.CompilerParams(dimension_semantics=("arbitrary", "arbitrary"),
                                             vmem_limit_bytes=VMEM_LIMIT),
        name="gdn",
    )(qkv3, z3, ab3, alog, dtb, nw, mask16, hmask16, tril16)


def _mixffn_body(x_ref, ya_ref, yb_ref, gate_ref, rp_ref, gp_ref, wo_ref, g_ref, up_ref, cw_ref, down_ref, fg_ref,
                 o_ref, prev_ref, *, col_block, apply_final):
    tm, d = x_ref.shape
    hidden = down_ref.shape[0]

    @pl.when(pl.program_id(1) == 0)
    def _():
        prev_ref[...] = jnp.zeros_like(prev_ref)

    ya = jnp.dot(ya_ref[...], rp_ref[...], preferred_element_type=F32)
    yb = jnp.dot(yb_ref[...], gp_ref[...], preferred_element_type=F32)
    mixed = _sigmoid(gate_ref[:, :d].astype(F32)) * ya + _sigmoid(gate_ref[:, d:].astype(F32)) * yb
    x = x_ref[...] + jnp.dot(mixed.astype(BF16), wo_ref[...], preferred_element_type=F32)

    ms = jnp.mean(x * x, axis=-1, keepdims=True)
    xn = (x * lax.rsqrt(ms + NORM_EPS) * g_ref[...]).astype(BF16)

    def up_cols(lo):
        return jnp.dot(xn, up_ref[:, lo:lo + col_block], preferred_element_type=F32)

    def conv_cols(h, lo):
        hp = jnp.concatenate([prev_ref[:, lo:lo + col_block], h], axis=0)
        out = h * cw_ref[FFN_CONV - 1:FFN_CONV, lo:lo + col_block]
        for i in range(FFN_CONV - 1):
            s = SUBLANES - (FFN_CONV - 1) + i
            out = out + hp[s:s + tm, :] * cw_ref[i:i + 1, lo:lo + col_block]
        prev_ref[:, lo:lo + col_block] = h[tm - SUBLANES:tm, :]
        return out

    nblk = hidden // col_block
    acc = x
    hs = (up_cols(0), up_cols(hidden))
    for j in range(nblk):
        lo = j * col_block
        nxt = (up_cols(lo + col_block), up_cols(hidden + lo + col_block)) if j + 1 < nblk else None
        hg = conv_cols(hs[0], lo)
        hu = conv_cols(hs[1], hidden + lo)
        act = (hg * _sigmoid(hg) * hu).astype(BF16)
        acc = acc + jnp.dot(act, down_ref[lo:lo + col_block, :], preferred_element_type=F32)
        hs = nxt
    if apply_final:
        ms2 = jnp.mean(acc * acc, axis=-1, keepdims=True)
        acc = acc * lax.rsqrt(ms2 + NORM_EPS) * fg_ref[...]
    o_ref[...] = acc


def _mixffn(x3, ya3, yb3, gate3, rp, gp, wo, gain, up, cw, down, final_g, tm, apply_final):
    b, t, d = x3.shape
    hidden = down.shape[0]
    full = lambda arr: pl.BlockSpec(arr.shape, lambda i, j: (0, 0))
    single = lambda arr: pl.BlockSpec(arr.shape, lambda i, j: (0, 0), pipeline_mode=pl.Buffered(1))
    tile = lambda arr: pl.BlockSpec((None, tm, arr.shape[-1]), lambda i, j: (i, j, 0))
    body = functools.partial(_mixffn_body, col_block=MXU_TILE, apply_final=apply_final)
    return pl.pallas_call(
        body,
        grid=(b, t // tm),
        in_specs=[tile(x3), tile(ya3), tile(yb3), tile(gate3), single(rp), single(gp), single(wo),
                  full(gain), single(up), full(cw), single(down), full(final_g)],
        out_specs=pl.BlockSpec((None, tm, d), lambda i, j: (i, j, 0)),
        out_shape=jax.ShapeDtypeStruct((b, t, d), F32),
        scratch_shapes=[pltpu.VMEM((SUBLANES, 2 * hidden), F32)],
        compiler_params=pltpu.CompilerParams(dimension_semantics=("arbitrary", "arbitrary"),
                                             vmem_limit_bytes=VMEM_LIMIT),
        name="mixffn",
    )(x3, ya3, yb3, gate3, rp, gp, wo, gain, up, cw, down, final_g)


def _row(vec):
    return vec.reshape(1, -1).astype(F32)


def _pad_lanes(vec, offset=0):
    out = jnp.zeros((1, LANES), F32)
    return lax.dynamic_update_slice(out, vec.reshape(1, -1).astype(F32), (0, offset))


def kernel(x, norm1_g, w_in, rwkv_mu, rwkv_w0, rwkv_w2, rwkv_a0, rwkv_a2, rwkv_g2, rwkv_k_k, rwkv_k_a,
           rwkv_r_k, rwkv_ln_w, rwkv_ln_b, rwkv_proj, gdn_conv_w, gdn_a_log, gdn_dt_bias, gdn_norm_w,
           gdn_proj, w_out, norm2_g, ffn_up, ffn_conv_w, ffn_down, final_g):
    b, t, d = x.shape
    n = b * t
    depth = norm1_g.shape[0]
    tm = 256
    tm_in = 512
    bb = next(c for c in (8, 4, 2, 1) if b % c == 0)
    scan_rows = 2 * CHUNK if t % (2 * CHUNK) == 0 else CHUNK
    assert t % tm_in == 0 and t % tm == 0 and t % CHUNK == 0, "sequence length must be a multiple of the tiles"
    assert (bb * CHUNK) % min(bb * CHUNK, MXU_TILE) == 0

    blk = jnp.arange(MXU_TILE) // CHUNK
    maskf = (blk[:, None] == blk[None, :]).astype(F32)
    mask16 = maskf.astype(BF16)
    rr = jnp.arange(min(bb * CHUNK, MXU_TILE))
    tril16 =((rr[:, None] >= rr[None, :]) & (rr[:, None] // CHUNK == rr[None, :] // CHUNK)).astype(BF16)
    hmask16 = (jnp.arange(GDN_HEADS * CHUNK)[:, None] // CHUNK == jnp.arange(GDN_WIDTH)[None, :] // GDN_HEAD_DIM).astype(BF16)

    widths = (RWKV_SHIFT_WIDTH, 3 * GDN_WIDTH, GDN_WIDTH, AB_PAD, 2 * d)
    dtypes = (BF16, BF16, BF16, F32, BF16)

    for l in range(depth):
        p_rwkv, qkv, z, ab, gate = _inproj(x.reshape(n, d), _row(norm1_g[l]), w_in[l].astype(BF16),
                                           gdn_conv_w[l].astype(F32), widths, dtypes, tm_in, t // tm_in)

        rwkv_params = (_row(rwkv_mu[l]), _row(rwkv_w0[l]), rwkv_w2[l].astype(BF16), _row(rwkv_a0[l]),
                       rwkv_a2[l].astype(BF16), rwkv_g2[l].astype(BF16), _row(rwkv_k_k[l]), _row(rwkv_k_a[l]),
                       _row(rwkv_r_k[l]), _row(rwkv_ln_w[l]), _row(rwkv_ln_b[l]))
        y_a = _rwkv(p_rwkv.reshape(b, t, -1), rwkv_params, mask16, maskf, tril16, bb, scan_rows)

        y_b = _gdn(qkv.reshape(b, t, -1), z.reshape(b, t, -1), ab.reshape(b, t, -1),
                   _pad_lanes(gdn_a_log[l]), _pad_lanes(gdn_dt_bias[l]), _row(gdn_norm_w[l]), mask16, hmask16, tril16, bb,
                   scan_rows)

        x = _mixffn(x, y_a, y_b, gate.reshape(b, t, -1), rwkv_proj[l].astype(BF16), gdn_proj[l].astype(BF16),
                    w_out[l].astype(BF16), _row(norm2_g[l]), ffn_up[l].astype(BF16), ffn_conv_w[l].astype(F32),
                    ffn_down[l].astype(BF16), _row(final_g), tm, apply_final=(l == depth - 1))
    return x
```

```python
import functools
import math

import jax
import jax.numpy as jnp
from jax import lax
from jax.experimental import pallas as pl
from jax.experimental.pallas import tpu as pltpu

F32 = jnp.float32
BF16 = jnp.bfloat16

CHUNK = 64
RWKV_HEADS = 8
RWKV_HEAD_DIM = 64
RWKV_WIDTH = RWKV_HEADS * RWKV_HEAD_DIM
DECAY_LORA = 64
ICLR_LORA = 64
GATE_LORA = 128
GDN_HEADS = 4
GDN_HEAD_DIM = 128
GDN_WIDTH = GDN_HEADS * GDN_HEAD_DIM
GDN_CONV = 4
FFN_CONV = 3
NORM_EPS = 1e-6
L2_EPS = 1e-6
RWKV_GN_EPS = 64e-5
RWKV_SHIFT_WIDTH = 3 * RWKV_WIDTH + DECAY_LORA + ICLR_LORA + GATE_LORA
LANES = 128
SUBLANES = 8
MXU_TILE = 256
AB_PAD = LANES
GROUP = MXU_TILE // CHUNK
RWKV_GROUPS = RWKV_WIDTH // MXU_TILE
assert RWKV_HEAD_DIM == CHUNK and GDN_HEADS == GROUP and RWKV_HEADS % GROUP == 0

VMEM_LIMIT = 52 * 1024 * 1024


def _sigmoid(x):
    return 1.0 / (1.0 + jnp.exp(-x))


def _softplus(x):
    return jnp.maximum(x, 0.0) + jnp.log(1.0 + jnp.exp(-jnp.abs(x)))


def _dot(a, b):
    return jnp.dot(a.astype(BF16), b.astype(BF16), preferred_element_type=F32)


def _dot_nt(a, b):
    return lax.dot_general(a.astype(BF16), b.astype(BF16), (((1,), (1,)), ((), ())), preferred_element_type=F32)


def _dot_tn(a, b):
    return lax.dot_general(a.astype(BF16), b.astype(BF16), (((0,), (0,)), ((), ())), preferred_element_type=F32)


def _bdiag(x, mask16):
    reps = MXU_TILE // x.shape[0]
    return jnp.concatenate([x.astype(BF16)] * reps, axis=0) * mask16


def _chunk_cumsum(tril16, x16):
    tr = tril16.shape[0]
    return jnp.concatenate([jnp.dot(tril16, x16[r0:r0 + tr], preferred_element_type=F32)
                            for r0 in range(0, x16.shape[0], tr)], axis=0)


def _cat_masks():
    ri = lax.broadcasted_iota(jnp.int32, (CHUNK, MXU_TILE), 0)
    ci = lax.broadcasted_iota(jnp.int32, (CHUNK, MXU_TILE), 1) % CHUNK
    return ri > ci, ri >= ci, (ri == ci).astype(F32)


def _unit_lower_inverse_cat(ms, eye_cat, mask16):
    ts = [eye_cat + m for m in ms]
    mks = [_dot(m, _bdiag(m, mask16)) for m in ms]
    levels = CHUNK.bit_length() - 1
    for _ in range(levels - 2):
        boths = [_dot(jnp.concatenate([mk, t], axis=0), _bdiag(mk, mask16)) for mk, t in zip(mks, ts)]
        mks = [both[:CHUNK] for both in boths]
        ts = [t + both[CHUNK:] for t, both in zip(ts, boths)]
    return [t + _dot(t, _bdiag(mk, mask16)) for t, mk in zip(ts, mks)]


def _inproj_body(x_ref, g_ref, w_ref, cw_ref, o_rwkv, o_qkv, o_z, o_ab, o_gate, wg_ref, tail_ref, *, tiles_per_seq):
    ab_start = o_rwkv.shape[-1] + o_qkv.shape[-1] + o_z.shape[-1]
    gate_start = w_ref.shape[-1] - o_gate.shape[-1]

    @pl.when(pl.program_id(0) == 0)
    def _():
        wg_ref[...] = w_ref[:, gate_start:]

    x = x_ref[...]
    ms = jnp.mean(x * x, axis=-1, keepdims=True)
    u = (x * lax.rsqrt(ms + NORM_EPS) * g_ref[...]).astype(BF16)
    tm = x_ref.shape[0]
    w_r, w_q, w_z = o_rwkv.shape[-1], o_qkv.shape[-1], o_z.shape[-1]

    @pl.when(pl.program_id(0) % tiles_per_seq == 0)
    def _():
        tail_ref[...] = jnp.zeros_like(tail_ref)

    def conv_silu(h, lo):
        cols = slice(lo, lo + MXU_TILE)
        hp = jnp.concatenate([tail_ref[:, cols], h], axis=0)
        conv = h * cw_ref[GDN_CONV - 1:GDN_CONV, cols]
        for i in range(GDN_CONV - 1):
            s = SUBLANES - (GDN_CONV - 1) + i
            conv = conv + hp[s:s + tm, :] * cw_ref[i:i + 1, cols]
        tail_ref[:, cols] = h[tm - SUBLANES:tm, :]
        o_qkv[:, cols] = (conv * _sigmoid(conv)).astype(o_qkv.dtype)

    def project(ref, wref, out_lo, w_lo):
        ref[:, out_lo:out_lo + MXU_TILE] = jnp.dot(
            u, wref[:, w_lo:w_lo + MXU_TILE], preferred_element_type=F32).astype(ref.dtype)

    plain = [functools.partial(project, o_rwkv, w_ref, k, k) for k in range(0, w_r, MXU_TILE)]
    plain += [functools.partial(project, o_z, w_ref, k, w_r + w_q + k) for k in range(0, w_z, MXU_TILE)]
    plain += [functools.partial(project, o_gate, wg_ref, k, k) for k in range(0, o_gate.shape[-1], MXU_TILE)]
    qcol = lambda lo: jnp.dot(u, w_ref[:, w_r + lo:w_r + lo + MXU_TILE], preferred_element_type=F32)
    hq = qcol(0)
    for lo in range(0, w_q, MXU_TILE):
        nxt = qcol(lo + MXU_TILE) if lo + MXU_TILE < w_q else None
        for _ in range(2):
            plain.pop(0)()
        conv_silu(hq, lo)
        hq = nxt
    for emit in plain:
        emit()
    o_ab[...] = jnp.dot(u, w_ref[:, ab_start:ab_start + o_ab.shape[-1]], preferred_element_type=F32)


def _inproj(x2, gain, w16, cw, widths, dtypes, tm, tiles_per_seq):
    n, d = x2.shape
    return pl.pallas_call(
        functools.partial(_inproj_body, tiles_per_seq=tiles_per_seq),
        grid=(n // tm,),
        in_specs=[
            pl.BlockSpec((tm, d), lambda i: (i, 0)),
            pl.BlockSpec((1, d), lambda i: (0, 0)),
            pl.BlockSpec(w16.shape, lambda i: (0, 0), pipeline_mode=pl.Buffered(1)),
            pl.BlockSpec(cw.shape, lambda i: (0, 0)),
        ],
        out_specs=[pl.BlockSpec((tm, w), lambda i: (i, 0)) for w in widths],
        out_shape=[jax.ShapeDtypeStruct((n, w), dt) for w, dt in zip(widths, dtypes)],
        scratch_shapes=[pltpu.VMEM((d, widths[-1]), BF16), pltpu.VMEM((SUBLANES, widths[1]), F32)],
        compiler_params=pltpu.CompilerParams(dimension_semantics=("arbitrary",), vmem_limit_bytes=VMEM_LIMIT),
        name="inproj",
    )(x2, gain, w16, cw)


def _rwkv_body(p_ref, *refs):
    *const_refs, y_ref, prev_ref, s_ref = refs

    @pl.when(pl.program_id(1) == 0)
    def _():
        prev_ref[...] = jnp.zeros_like(prev_ref)
        s_ref[...] = jnp.zeros_like(s_ref)

    for r0 in range(0, p_ref.shape[1], CHUNK):
        _rwkv_chunk(p_ref.at[:, r0:r0 + CHUNK, :], *const_refs, y_ref.at[:, r0:r0 + CHUNK, :], prev_ref, s_ref)


def _rwkv_chunk(p_ref, mu_ref, w0_ref, w2_ref, a0_ref, a2_ref, g2_ref, kk_ref, ka_ref, rk_ref, lnw_ref,
                lnb_ref, m16_ref, mf_ref, tril_ref, y_ref, prev_ref, s_ref):
    bb, c, wd = p_ref.shape
    rows = bb * c
    hd = RWKV_HEAD_DIM
    row8 =lax.broadcasted_iota(jnp.int32, (SUBLANES, 1), 0)
    ps, shifts = [], []
    for bi in range(bb):
        pb_ = p_ref[bi].astype(F32)
        rolled = pltpu.roll(pb_, 1, 0)
        shifts += [jnp.where(row8 == 0, prev_ref[bi, 0:1, :], rolled[:SUBLANES]), rolled[SUBLANES:]]
        prev_ref[bi, 0:1, :] = pb_[c - 1:c, :]
        ps.append(pb_)
    p = jnp.concatenate(ps, axis=0)
    pm = p + (jnp.concatenate(shifts, axis=0) - p) * mu_ref[...]

    w = RWKV_WIDTH
    r = pm[:, 0:w]
    k = pm[:, w:2 * w]
    v = pm[:, 2 * w:3 * w]
    o0 = 3 * w
    wl = pm[:, o0:o0 + DECAY_LORA]
    al = pm[:, o0 + DECAY_LORA:o0 + DECAY_LORA + ICLR_LORA]
    gl = pm[:, o0 + DECAY_LORA + ICLR_LORA:o0 + DECAY_LORA + ICLR_LORA + GATE_LORA]

    z = w0_ref[...] + _dot(jnp.tanh(wl), w2_ref[...])
    lw = (-math.exp(-0.5) * _sigmoid(z)).astype(BF16).astype(F32)
    a = _sigmoid(a0_ref[...] + _dot(al, a2_ref[...]))
    g = _dot(_sigmoid(gl), g2_ref[...])

    mask16 = m16_ref[...]
    maskf = mf_ref[...]

    def head_sums(*xs):
        stacked = jnp.concatenate([x[:, q * MXU_TILE:(q + 1) * MXU_TILE] for x in xs for q in range(RWKV_GROUPS)], axis=0)
        res = jnp.dot(stacked.astype(BF16), mask16, preferred_element_type=F32)
        outs = []
        for i in range(len(xs)):
            base = i * RWKV_GROUPS * rows
            outs.append(jnp.concatenate([res[base + q * rows:base + (q + 1) * rows] for q in range(RWKV_GROUPS)], axis=1))
        return outs

    kx = k * kk_ref[...]
    k2 = k * (1.0 + (a - 1.0) * ka_ref[...])
    kx_ss, rk_sum = head_sums(kx * kx, r * k2 * rk_ref[...])
    kk = kx * lax.rsqrt(kx_ss + L2_EPS)

    cum = _chunk_cumsum(tril_ref[...], lw.astype(BF16))
    e_pos = jnp.exp(cum)
    e_neg = 1.0 / e_pos
    e_prev = jnp.exp(cum - lw)
    w_tot = [e_pos[(bi + 1) * c - 1:(bi + 1) * c, :] for bi in range(bb)]
    e_rest = jnp.concatenate([e_neg[bi * c:(bi + 1) * c] * w_tot[bi] for bi in range(bb)], axis=0)

    b_vec = kk * a
    a_t = (-kk * e_prev).astype(BF16)
    r_t = (r * e_pos).astype(BF16)
    b_t = (b_vec * e_neg).astype(BF16)
    k_t = (k2 * e_neg).astype(BF16)
    b_h = (b_vec * e_rest).astype(BF16)
    k_h = (k2 * e_rest).astype(BF16)
    v16 = v.astype(BF16)

    strict_cat, incl_cat, eye_cat = _cat_masks()
    mask2 = jnp.concatenate([strict_cat, incl_cat], axis=0)

    chains = [(bi, q) for bi in range(bb) for q in range(RWKV_GROUPS)]
    rs_of = lambda bi: slice(bi * c, (bi + 1) * c)
    gs_of = lambda q: slice(q * MXU_TILE, (q + 1) * MXU_TILE)
    s_bds = [s_ref[bi, q] for bi, q in chains]
    pxs = []
    for (bi, q), s_bd in zip(chains, s_bds):
        rs, gs = rs_of(bi), gs_of(q)
        l2 = jnp.concatenate([a_t[rs, gs], r_t[rs, gs]], axis=0)
        w3 = jnp.concatenate([_bdiag(b_t[rs, gs], mask16), _bdiag(k_t[rs, gs], mask16), s_bd.astype(BF16)], axis=0)
        pxs.append(_dot_nt(l2, w3))
    pbs = [jnp.where(mask2, px[:, 0:MXU_TILE], 0.0) for px in pxs]
    pks = [jnp.where(mask2, px[:, MXU_TILE:2 * MXU_TILE], 0.0) for px in pxs]
    xas = [px[:, 2 * MXU_TILE:] + _dot(pk, _bdiag(v16[rs_of(bi), gs_of(q)], mask16))
           for (bi, q), px, pk in zip(chains, pxs, pks)]
    tinvs = _unit_lower_inverse_cat([pb[:c] for pb in pbs], eye_cat, mask16)
    us = [_dot(tinv, _bdiag(xa[:c], mask16)) for tinv, xa in zip(tinvs, xas)]
    ys = [xa[c:] + _dot(pb[c:], _bdiag(u, mask16)) for xa, pb, u in zip(xas, pbs, us)]
    for (bi, q), s_bd, u in zip(chains, s_bds, us):
        rs, gs = rs_of(bi), gs_of(q)
        uv = jnp.concatenate([u.astype(BF16), v16[rs, gs]], axis=0)
        bk = jnp.concatenate([b_h[rs, gs], k_h[rs, gs]], axis=0)
        s_ref[bi, q] = s_bd * w_tot[bi][:, gs] + _dot_tn(uv, bk) * maskf
    y_rows = [jnp.concatenate(ys[bi * RWKV_GROUPS:(bi + 1) * RWKV_GROUPS], axis=1) for bi in range(bb)]

    y = jnp.concatenate(y_rows, axis=0)
    inv_hd = 1.0 / hd
    (mean,) = head_sums(y)
    yc = y - mean * inv_hd
    (var,) = head_sums(yc * yc)
    yn = yc * lax.rsqrt(var * inv_hd + RWKV_GN_EPS) * lnw_ref[...] + lnb_ref[...]
    out = (yn + rk_sum * v) * g
    for bi in range(bb):
        y_ref[bi] = out[bi * c:(bi + 1) * c, :].astype(y_ref.dtype)


def _rwkv(p3, params, mask16, maskf, tril16, bb, rows_per_step):
    b, t, wd = p3.shape
    c = rows_per_step
    full = lambda arr: pl.BlockSpec(arr.shape, lambda i, j: (0,) * arr.ndim)
    return pl.pallas_call(
        _rwkv_body,
        grid=(b // bb, t // c),
        in_specs=[pl.BlockSpec((bb, c, wd), lambda i, j: (i, j, 0))] + [full(a) for a in params]
        + [full(mask16), full(maskf), full(tril16)],
        out_specs=pl.BlockSpec((bb, c, RWKV_WIDTH), lambda i, j: (i, j, 0)),
        out_shape=jax.ShapeDtypeStruct((b, t, RWKV_WIDTH), BF16),
        scratch_shapes=[
            pltpu.VMEM((bb, SUBLANES, wd), F32),
            pltpu.VMEM((bb, RWKV_GROUPS, MXU_TILE, MXU_TILE), F32),
        ],
        compiler_params=pltpu.CompilerParams(dimension_semantics=("arbitrary", "arbitrary"),
                                             vmem_limit_bytes=VMEM_LIMIT),
        name="rwkv",
    )(p3, *params, mask16, maskf, tril16)


def _gdn_body(qkv_ref, z_ref, ab_ref, *refs):
    *const_refs, y_ref, s_ref = refs

    @pl.when(pl.program_id(1) == 0)
    def _():
        s_ref[...] = jnp.zeros_like(s_ref)

    for r0 in range(0, qkv_ref.shape[1], CHUNK):
        rs = slice(r0, r0 + CHUNK)
        _gdn_chunk(qkv_ref.at[:, rs, :], z_ref.at[:, rs, :], ab_ref.at[:, rs, :], *const_refs, y_ref.at[:, rs, :], s_ref)


def _gdn_chunk(qkv_ref, z_ref, ab_ref, alog_ref, dtb_ref, nw_ref, m16_ref, hm16_ref, tril_ref, y_ref, s_ref):
    bb, c, _ = qkv_ref.shape
    hd = GDN_HEAD_DIM
    nh = GDN_HEADS
    qkv = jnp.concatenate([qkv_ref[bi] for bi in range(bb)], axis=0).astype(F32)
    w = GDN_WIDTH

    ab = jnp.concatenate([ab_ref[bi] for bi in range(bb)], axis=0)
    g_all = -jnp.exp(alog_ref[...]) * _softplus(ab + dtb_ref[...])
    beta_all = _sigmoid(ab)

    g_hi = g_all.astype(BF16)
    rem = g_all - g_hi.astype(F32)
    g_mid = rem.astype(BF16)
    g_lo = (rem - g_mid.astype(F32)).astype(BF16)
    gcp = _chunk_cumsum(tril_ref[...], jnp.concatenate([g_hi, g_mid, g_lo], axis=1))
    gc_all = gcp[:, 0:LANES] + gcp[:, LANES:2 * LANES] + gcp[:, 2 * LANES:3 * LANES]
    eg_all = jnp.exp(gc_all)

    mask16 = m16_ref[...]
    hmask16 = hm16_ref[...]
    strict_cat, incl_cat, eye_cat = _cat_masks()
    lane_head = lax.broadcasted_iota(jnp.int32, (c, MXU_TILE), 1) // CHUNK

    qs, ks, kbs = [], [], []
    for h in range(nh):
        qh = qkv[:, h * hd:(h + 1) * hd]
        kh = qkv[:, w + h * hd:w + (h + 1) * hd]
        qs.append(qh * lax.rsqrt(jnp.sum(qh * qh, axis=-1, keepdims=True) + L2_EPS) * (hd ** -0.5))
        kh = kh * lax.rsqrt(jnp.sum(kh * kh, axis=-1, keepdims=True) + L2_EPS)
        ks.append(kh)
        kbs.append(kh * beta_all[:, nh + h:nh + h + 1])
    k_all = jnp.concatenate(ks, axis=1).astype(BF16)
    kb_all = jnp.concatenate(kbs, axis=1).astype(BF16)
    q_all = jnp.concatenate(qs, axis=1).astype(BF16)

    rs_of = lambda bi: slice(bi * c, (bi + 1) * c)
    gc_bs = [gc_all[rs_of(bi)] for bi in range(bb)]
    glasts = [gc_b[c - 1:c, :] for gc_b in gc_bs]
    erests = [jnp.exp(glast - gc_b) for glast, gc_b in zip(glasts, gc_bs)]
    elasts = [jnp.exp(glast) for glast in glasts]

    decays = []
    for gc_b in gc_bs:
        f_cat = jnp.zeros((c, MXU_TILE), F32)
        for h in range(nh):
            f_cat = jnp.where(lane_head == h, gc_b[:, h:h + 1], f_cat)
        r_cat = jnp.sum(f_cat * eye_cat, axis=0, keepdims=True)
        decays.append(jnp.where(incl_cat, jnp.exp(jnp.where(incl_cat, f_cat - r_cat, 0.0)), 0.0))

    kqs = [_dot_nt(jnp.concatenate([kb_all[rs_of(bi)], q_all[rs_of(bi)]], axis=0),
                   jnp.concatenate([k_all[rs_of(bi)]] * nh, axis=0) * hmask16) for bi in range(bb)]
    attns = [kq[c:] * decay for kq, decay in zip(kqs, decays)]
    tinvs = _unit_lower_inverse_cat([-jnp.where(strict_cat, kq[:c] * decay, 0.0) for kq, decay in zip(kqs, decays)],
                                    eye_cat, mask16)

    chains = [(bi, h) for bi in range(bb) for h in range(nh)]
    tl_of = lambda h: slice(h * c, (h + 1) * c)
    egs = [eg_all[rs_of(bi), h:h + 1] for bi, h in chains]
    uws = []
    for (bi, h), eg in zip(chains, egs):
        rs = rs_of(bi)
        vh = qkv[rs, 2 * w + h * hd:2 * w + (h + 1) * hd]
        beta = beta_all[rs, nh + h:nh + h + 1]
        uws.append(_dot(tinvs[bi][:, tl_of(h)], jnp.concatenate([vh * beta, kbs[h][rs] * eg], axis=1)))
    ss = [s_ref[bi, h] for bi, h in chains]
    wqs = [_dot(jnp.concatenate([uw[:, hd:], qs[h][rs_of(bi)] * eg], axis=0), s)
           for (bi, h), uw, eg, s in zip(chains, uws, egs, ss)]
    v_news = [uw[:, :hd] - wq[:c] for uw, wq in zip(uws, wqs)]
    os_ = [wq[c:] + _dot(attns[bi][:, tl_of(h)], v_new) for (bi, h), wq, v_new in zip(chains, wqs, v_news)]
    for (bi, h), s, v_new in zip(chains, ss, v_news):
        s_ref[bi, h] = s * elasts[bi][:, h:h + 1] + _dot_tn(ks[h][rs_of(bi)] * erests[bi][:, h:h + 1], v_new)
    for (bi, h), o in zip(chains, os_):
        sl = slice(h * hd, (h + 1) * hd)
        on = o * lax.rsqrt(jnp.mean(o * o, axis=-1, keepdims=True) + NORM_EPS) * nw_ref[...]
        zh = z_ref[bi, :, sl].astype(F32)
        y_ref[bi, :, sl] = (on * (zh * _sigmoid(zh))).astype(y_ref.dtype)


def _gdn(qkv3, z3, ab3, alog, dtb, nw, mask16, hmask16, tril16, bb, rows_per_step):
    b, t, wd = qkv3.shape
    c = rows_per_step
    full = lambda arr: pl.BlockSpec(arr.shape, lambda i, j: (0,) * arr.ndim)
    return pl.pallas_call(
        _gdn_body,
        grid=(b // bb, t // c),
        in_specs=[
            pl.BlockSpec((bb, c, wd), lambda i, j: (i, j, 0)),
            pl.BlockSpec((bb, c, GDN_WIDTH), lambda i, j: (i, j, 0)),
            pl.BlockSpec((bb, c, AB_PAD), lambda i, j: (i, j, 0)),
            full(alog), full(dtb), full(nw), full(mask16), full(hmask16), full(tril16),
        ],
        out_specs=pl.BlockSpec((bb, c, GDN_WIDTH), lambda i, j: (i, j, 0)),
        out_shape=jax.ShapeDtypeStruct((b, t, GDN_WIDTH), BF16),
        scratch_shapes=[pltpu.VMEM((bb, GDN_HEADS, GDN_HEAD_DIM, GDN_HEAD_DIM), F32)],
        compiler_params=pltpu.CompilerParams(dimension_semantics=("arbitrary", "arbitrary"),
                                             vmem_limit_bytes=VMEM_LIMIT),
        name="gdn",
    )(qkv3, z3, ab3, alog, dtb, nw, mask16, hmask16, tril16)


def _mixffn_body(x_ref, ya_ref, yb_ref, gate_ref, *refs, tile_rows, col_block, apply_final):
    *const_refs, o_ref, prev_ref = refs

    @pl.when(pl.program_id(1) == 0)
    def _():
        prev_ref[...] = jnp.zeros_like(prev_ref)

    for r0 in range(0, x_ref.shape[0], tile_rows):
        rs = slice(r0, r0 + tile_rows)
        _mixffn_tile(x_ref.at[rs, :], ya_ref.at[rs, :], yb_ref.at[rs, :], gate_ref.at[rs, :], *const_refs,
                     o_ref.at[rs, :], prev_ref, col_block=col_block, apply_final=apply_final)


def _mixffn_tile(x_ref, ya_ref, yb_ref, gate_ref, rp_ref, gp_ref, wo_ref, g_ref, up_ref, cw_ref, down_ref, fg_ref,
                 o_ref, prev_ref, *, col_block, apply_final):
    tm, d = x_ref.shape
    hidden = down_ref.shape[0]

    ya = jnp.dot(ya_ref[...], rp_ref[...], preferred_element_type=F32)
    yb = jnp.dot(yb_ref[...], gp_ref[...], preferred_element_type=F32)
    mixed = _sigmoid(gate_ref[:, :d].astype(F32)) * ya + _sigmoid(gate_ref[:, d:].astype(F32)) * yb
    x = x_ref[...] + jnp.dot(mixed.astype(BF16), wo_ref[...], preferred_element_type=F32)

    ms = jnp.mean(x * x, axis=-1, keepdims=True)
    xn = (x * lax.rsqrt(ms + NORM_EPS) * g_ref[...]).astype(BF16)

    def up_cols(lo):
        return jnp.dot(xn, up_ref[:, lo:lo + col_block], preferred_element_type=F32)

    def conv_cols(h, lo):
        hp = jnp.concatenate([prev_ref[:, lo:lo + col_block], h], axis=0)
        out = h * cw_ref[FFN_CONV - 1:FFN_CONV, lo:lo + col_block]
        for i in range(FFN_CONV - 1):
            s = SUBLANES - (FFN_CONV - 1) + i
            out = out + hp[s:s + tm, :] * cw_ref[i:i + 1, lo:lo + col_block]
        prev_ref[:, lo:lo + col_block] = h[tm - SUBLANES:tm, :]
        return out

    nblk = hidden // col_block
    acc = x
    hs = (up_cols(0), up_cols(hidden))
    for j in range(nblk):
        lo = j * col_block
        nxt = (up_cols(lo + col_block), up_cols(hidden + lo + col_block)) if j + 1 < nblk else None
        hg = conv_cols(hs[0], lo)
        hu = conv_cols(hs[1], hidden + lo)
        act = (hg * _sigmoid(hg) * hu).astype(BF16)
        acc = acc + jnp.dot(act, down_ref[lo:lo + col_block, :], preferred_element_type=F32)
        hs = nxt
    if apply_final:
        ms2 = jnp.mean(acc * acc, axis=-1, keepdims=True)
        acc = acc * lax.rsqrt(ms2 + NORM_EPS) * fg_ref[...]
    o_ref[...] = acc


def _mixffn(x3, ya3, yb3, gate3, rp, gp, wo, gain, up, cw, down, final_g, tm, tiles_per_step, apply_final):
    b, t, d = x3.shape
    hidden = down.shape[0]
    full = lambda arr: pl.BlockSpec(arr.shape, lambda i, j: (0, 0))
    single = lambda arr: pl.BlockSpec(arr.shape, lambda i, j: (0, 0), pipeline_mode=pl.Buffered(1))
    rows = tm * tiles_per_step
    tile = lambda arr: pl.BlockSpec((None, rows, arr.shape[-1]), lambda i, j: (i, j, 0))
    body = functools.partial(_mixffn_body, tile_rows=tm, col_block=MXU_TILE, apply_final=apply_final)
    return pl.pallas_call(
        body,
        grid=(b, t // rows),
        in_specs=[tile(x3), tile(ya3), tile(yb3), tile(gate3), single(rp), single(gp), single(wo),
                  full(gain), single(up), full(cw), single(down), full(final_g)],
        out_specs=pl.BlockSpec((None, rows, d), lambda i, j: (i, j, 0)),
        out_shape=jax.ShapeDtypeStruct((b, t, d), F32),
        scratch_shapes=[pltpu.VMEM((SUBLANES, 2 * hidden), F32)],
        compiler_params=pltpu.CompilerParams(dimension_semantics=("arbitrary", "arbitrary"),
                                             vmem_limit_bytes=VMEM_LIMIT),
        name="mixffn",
    )(x3, ya3, yb3, gate3, rp, gp, wo, gain, up, cw, down, final_g)


def _row(vec):
    return vec.reshape(1, -1).astype(F32)


def _pad_lanes(vec, offset=0):
    out = jnp.zeros((1, LANES), F32)
    return lax.dynamic_update_slice(out, vec.reshape(1, -1).astype(F32), (0, offset))


def kernel(x, norm1_g, w_in, rwkv_mu, rwkv_w0, rwkv_w2, rwkv_a0, rwkv_a2, rwkv_g2, rwkv_k_k, rwkv_k_a,
           rwkv_r_k, rwkv_ln_w, rwkv_ln_b, rwkv_proj, gdn_conv_w, gdn_a_log, gdn_dt_bias, gdn_norm_w,
           gdn_proj, w_out, norm2_g, ffn_up, ffn_conv_w, ffn_down, final_g):
    b, t, d = x.shape
    n = b * t
    depth = norm1_g.shape[0]
    tm = 256
    tm_in = 512
    bb = next(c for c in (8, 4, 2, 1) if b % c == 0)
    scan_rows = 2 * CHUNK if t % (2 * CHUNK) == 0 else CHUNK
    ffn_tiles = 2 if t % (2 * tm) == 0 else 1
    assert t % tm_in == 0 and t % tm == 0 and t % CHUNK == 0, "sequence length must be a multiple of the tiles"
    assert (bb * CHUNK) % min(bb * CHUNK, MXU_TILE) == 0

    blk = jnp.arange(MXU_TILE) // CHUNK
    maskf = (blk[:, None] == blk[None, :]).astype(F32)
    mask16 = maskf.astype(BF16)
    rr = jnp.arange(min(bb * CHUNK, MXU_TILE))
    tril16 =((rr[:, None] >= rr[None, :]) & (rr[:, None] // CHUNK == rr[None, :] // CHUNK)).astype(BF16)
    hmask16 = (jnp.arange(GDN_HEADS * CHUNK)[:, None] // CHUNK == jnp.arange(GDN_WIDTH)[None, :] // GDN_HEAD_DIM).astype(BF16)

    widths = (RWKV_SHIFT_WIDTH, 3 * GDN_WIDTH, GDN_WIDTH, AB_PAD, 2 * d)
    dtypes = (BF16, BF16, BF16, F32, BF16)

    for l in range(depth):
        p_rwkv, qkv, z, ab, gate = _inproj(x.reshape(n, d), _row(norm1_g[l]), w_in[l].astype(BF16),
                                           gdn_conv_w[l].astype(F32), widths, dtypes, tm_in, t // tm_in)

        rwkv_params = (_row(rwkv_mu[l]), _row(rwkv_w0[l]), rwkv_w2[l].astype(BF16), _row(rwkv_a0[l]),
                       rwkv_a2[l].astype(BF16), rwkv_g2[l].astype(BF16), _row(rwkv_k_k[l]), _row(rwkv_k_a[l]),
                       _row(rwkv_r_k[l]), _row(rwkv_ln_w[l]), _row(rwkv_ln_b[l]))
        y_a = _rwkv(p_rwkv.reshape(b, t, -1), rwkv_params, mask16, maskf, tril16, bb, scan_rows)

        y_b = _gdn(qkv.reshape(b, t, -1), z.reshape(b, t, -1), ab.reshape(b, t, -1),
                   _pad_lanes(gdn_a_log[l]), _pad_lanes(gdn_dt_bias[l]), _row(gdn_norm_w[l]), mask16, hmask16, tril16, bb,
                   scan_rows)

        x = _mixffn(x, y_a, y_b, gate.reshape(b, t, -1), rwkv_proj[l].astype(BF16), gdn_proj[l].astype(BF16),
                    w_out[l].astype(BF16), _row(norm2_g[l]), ffn_up[l].astype(BF16), ffn_conv_w[l].astype(F32),
                    ffn_down[l].astype(BF16), _row(final_g), tm, ffn_tiles, apply_final=(l == depth - 1))
    return x
```

```python
import functools
import math

import jax
import jax.numpy as jnp
from jax import lax
from jax.experimental import pallas as pl
from jax.experimental.pallas import tpu as pltpu

F32 = jnp.float32
BF16 = jnp.bfloat16

CHUNK = 64
RWKV_HEADS = 8
RWKV_HEAD_DIM = 64
RWKV_WIDTH = RWKV_HEADS * RWKV_HEAD_DIM
DECAY_LORA = 64
ICLR_LORA = 64
GATE_LORA = 128
GDN_HEADS = 4
GDN_HEAD_DIM = 128
GDN_WIDTH = GDN_HEADS * GDN_HEAD_DIM
GDN_CONV = 4
FFN_CONV = 3
NORM_EPS = 1e-6
L2_EPS = 1e-6
RWKV_GN_EPS = 64e-5
RWKV_SHIFT_WIDTH = 3 * RWKV_WIDTH + DECAY_LORA + ICLR_LORA + GATE_LORA
LANES = 128
SUBLANES = 8
MXU_TILE = 256
AB_PAD = LANES
GROUP = MXU_TILE // CHUNK
RWKV_GROUPS = RWKV_WIDTH // MXU_TILE
assert RWKV_HEAD_DIM == CHUNK and GDN_HEADS == GROUP and RWKV_HEADS % GROUP == 0

VMEM_LIMIT = 52 * 1024 * 1024


def _sigmoid(x):
    return 1.0 / (1.0 + jnp.exp(-x))


def _softplus(x):
    return jnp.maximum(x, 0.0) + jnp.log(1.0 + jnp.exp(-jnp.abs(x)))


def _dot(a, b):
    return jnp.dot(a.astype(BF16), b.astype(BF16), preferred_element_type=F32)


def _dot_nt(a, b):
    return lax.dot_general(a.astype(BF16), b.astype(BF16), (((1,), (1,)), ((), ())), preferred_element_type=F32)


def _dot_tn(a, b):
    return lax.dot_general(a.astype(BF16), b.astype(BF16), (((0,), (0,)), ((), ())), preferred_element_type=F32)


def _bdiag(x, mask16):
    reps = MXU_TILE // x.shape[0]
    return jnp.concatenate([x.astype(BF16)] * reps, axis=0) * mask16


def _chunk_cumsum(tril16, x16):
    tr = tril16.shape[0]
    return jnp.concatenate([jnp.dot(tril16, x16[r0:r0 + tr], preferred_element_type=F32)
                            for r0 in range(0, x16.shape[0], tr)], axis=0)


def _cat_masks():
    ri = lax.broadcasted_iota(jnp.int32, (CHUNK, MXU_TILE), 0)
    ci = lax.broadcasted_iota(jnp.int32, (CHUNK, MXU_TILE), 1) % CHUNK
    return ri > ci, ri >= ci, (ri == ci).astype(F32)


def _unit_lower_inverse_cat(ms, eye_cat, mask16):
    ts = [eye_cat + m for m in ms]
    mks = [_dot(m, _bdiag(m, mask16)) for m in ms]
    levels = CHUNK.bit_length() - 1
    for _ in range(levels - 2):
        boths = [_dot(jnp.concatenate([mk, t], axis=0), _bdiag(mk, mask16)) for mk, t in zip(mks, ts)]
        mks = [both[:CHUNK] for both in boths]
        ts = [t + both[CHUNK:] for t, both in zip(ts, boths)]
    return [t + _dot(t, _bdiag(mk, mask16)) for t, mk in zip(ts, mks)]


def _inproj_body(x_ref, g_ref, w_ref, cw_ref, o_rwkv, o_qkv, o_z, o_ab, o_gate, wg_ref, tail_ref, *, tile_rows,
                 steps_per_seq):
    @pl.when(pl.program_id(0) == 0)
    def _():
        wg_ref[...] = w_ref[:, w_ref.shape[-1] - o_gate.shape[-1]:]

    @pl.when(pl.program_id(0) % steps_per_seq == 0)
    def _():
        tail_ref[...] = jnp.zeros_like(tail_ref)

    for r0 in range(0, x_ref.shape[0], tile_rows):
        rs = slice(r0, r0 + tile_rows)
        _inproj_tile(x_ref.at[rs, :], g_ref, w_ref, cw_ref, *[o.at[rs, :] for o in (o_rwkv, o_qkv, o_z, o_ab, o_gate)],
                     wg_ref, tail_ref)


def _inproj_tile(x_ref, g_ref, w_ref, cw_ref, o_rwkv, o_qkv, o_z, o_ab, o_gate, wg_ref, tail_ref):
    ab_start = o_rwkv.shape[-1] + o_qkv.shape[-1] + o_z.shape[-1]
    x = x_ref[...]
    ms = jnp.mean(x * x, axis=-1, keepdims=True)
    u = (x * lax.rsqrt(ms + NORM_EPS) * g_ref[...]).astype(BF16)
    tm = x_ref.shape[0]
    w_r, w_q, w_z = o_rwkv.shape[-1], o_qkv.shape[-1], o_z.shape[-1]

    def conv_silu(h, lo):
        cols = slice(lo, lo + MXU_TILE)
        hp = jnp.concatenate([tail_ref[:, cols], h], axis=0)
        conv = h * cw_ref[GDN_CONV - 1:GDN_CONV, cols]
        for i in range(GDN_CONV - 1):
            s = SUBLANES - (GDN_CONV - 1) + i
            conv = conv + hp[s:s + tm, :] * cw_ref[i:i + 1, cols]
        tail_ref[:, cols] = h[tm - SUBLANES:tm, :]
        o_qkv[:, cols] = (conv * _sigmoid(conv)).astype(o_qkv.dtype)

    def project(ref, wref, out_lo, w_lo):
        ref[:, out_lo:out_lo + MXU_TILE] = jnp.dot(
            u, wref[:, w_lo:w_lo + MXU_TILE], preferred_element_type=F32).astype(ref.dtype)

    plain = [functools.partial(project, o_rwkv, w_ref, k, k) for k in range(0, w_r, MXU_TILE)]
    plain += [functools.partial(project, o_z, w_ref, k, w_r + w_q + k) for k in range(0, w_z, MXU_TILE)]
    plain += [functools.partial(project, o_gate, wg_ref, k, k) for k in range(0, o_gate.shape[-1], MXU_TILE)]
    qcol = lambda lo: jnp.dot(u, w_ref[:, w_r + lo:w_r + lo + MXU_TILE], preferred_element_type=F32)
    hq = qcol(0)
    for lo in range(0, w_q, MXU_TILE):
        nxt = qcol(lo + MXU_TILE) if lo + MXU_TILE < w_q else None
        for _ in range(2):
            plain.pop(0)()
        conv_silu(hq, lo)
        hq = nxt
    for emit in plain:
        emit()
    o_ab[...] = jnp.dot(u, w_ref[:, ab_start:ab_start + o_ab.shape[-1]], preferred_element_type=F32)


def _inproj(x2, gain, w16, cw, widths, dtypes, tm, tile_rows, steps_per_seq):
    n, d = x2.shape
    return pl.pallas_call(
        functools.partial(_inproj_body, tile_rows=tile_rows, steps_per_seq=steps_per_seq),
        grid=(n // tm,),
        in_specs=[
            pl.BlockSpec((tm, d), lambda i: (i, 0)),
            pl.BlockSpec((1, d), lambda i: (0, 0)),
            pl.BlockSpec(w16.shape, lambda i: (0, 0), pipeline_mode=pl.Buffered(1)),
            pl.BlockSpec(cw.shape, lambda i: (0, 0)),
        ],
        out_specs=[pl.BlockSpec((tm, w), lambda i: (i, 0)) for w in widths],
        out_shape=[jax.ShapeDtypeStruct((n, w), dt) for w, dt in zip(widths, dtypes)],
        scratch_shapes=[pltpu.VMEM((d, widths[-1]), BF16), pltpu.VMEM((SUBLANES, widths[1]), F32)],
        compiler_params=pltpu.CompilerParams(dimension_semantics=("arbitrary",), vmem_limit_bytes=VMEM_LIMIT),
        name="inproj",
    )(x2, gain, w16, cw)


def _rwkv_body(p_ref, *refs):
    *const_refs, y_ref, prev_ref, s_ref = refs

    @pl.when(pl.program_id(1) == 0)
    def _():
        prev_ref[...] = jnp.zeros_like(prev_ref)
        s_ref[...] = jnp.zeros_like(s_ref)

    for r0 in range(0, p_ref.shape[1], CHUNK):
        _rwkv_chunk(p_ref.at[:, r0:r0 + CHUNK, :], *const_refs, y_ref.at[:, r0:r0 + CHUNK, :], prev_ref, s_ref)


def _rwkv_chunk(p_ref, mu_ref, w0_ref, w2_ref, a0_ref, a2_ref, g2_ref, kk_ref, ka_ref, rk_ref, lnw_ref,
                lnb_ref, m16_ref, mf_ref, tril_ref, y_ref, prev_ref, s_ref):
    bb, c, wd = p_ref.shape
    rows = bb * c
    hd = RWKV_HEAD_DIM
    row8 =lax.broadcasted_iota(jnp.int32, (SUBLANES, 1), 0)
    ps, shifts = [], []
    for bi in range(bb):
        pb_ = p_ref[bi].astype(F32)
        rolled = pltpu.roll(pb_, 1, 0)
        shifts += [jnp.where(row8 == 0, prev_ref[bi, 0:1, :], rolled[:SUBLANES]), rolled[SUBLANES:]]
        prev_ref[bi, 0:1, :] = pb_[c - 1:c, :]
        ps.append(pb_)
    p = jnp.concatenate(ps, axis=0)
    pm = p + (jnp.concatenate(shifts, axis=0) - p) * mu_ref[...]

    w = RWKV_WIDTH
    r = pm[:, 0:w]
    k = pm[:, w:2 * w]
    v = pm[:, 2 * w:3 * w]
    o0 = 3 * w
    wl = pm[:, o0:o0 + DECAY_LORA]
    al = pm[:, o0 + DECAY_LORA:o0 + DECAY_LORA + ICLR_LORA]
    gl = pm[:, o0 + DECAY_LORA + ICLR_LORA:o0 + DECAY_LORA + ICLR_LORA + GATE_LORA]

    z = w0_ref[...] + _dot(jnp.tanh(wl), w2_ref[...])
    lw = (-math.exp(-0.5) * _sigmoid(z)).astype(BF16).astype(F32)
    a = _sigmoid(a0_ref[...] + _dot(al, a2_ref[...]))
    g = _dot(_sigmoid(gl), g2_ref[...])

    mask16 = m16_ref[...]
    maskf = mf_ref[...]

    def head_sums(*xs):
        stacked = jnp.concatenate([x[:, q * MXU_TILE:(q + 1) * MXU_TILE] for x in xs for q in range(RWKV_GROUPS)], axis=0)
        res = jnp.dot(stacked.astype(BF16), mask16, preferred_element_type=F32)
        outs = []
        for i in range(len(xs)):
            base = i * RWKV_GROUPS * rows
            outs.append(jnp.concatenate([res[base + q * rows:base + (q + 1) * rows] for q in range(RWKV_GROUPS)], axis=1))
        return outs

    kx = k * kk_ref[...]
    k2 = k * (1.0 + (a - 1.0) * ka_ref[...])
    kx_ss, rk_sum = head_sums(kx * kx, r * k2 * rk_ref[...])
    kk = kx * lax.rsqrt(kx_ss + L2_EPS)

    cum = _chunk_cumsum(tril_ref[...], lw.astype(BF16))
    e_pos = jnp.exp(cum)
    e_neg = 1.0 / e_pos
    e_prev = jnp.exp(cum - lw)
    w_tot = [e_pos[(bi + 1) * c - 1:(bi + 1) * c, :] for bi in range(bb)]
    e_rest = jnp.concatenate([e_neg[bi * c:(bi + 1) * c] * w_tot[bi] for bi in range(bb)], axis=0)

    b_vec = kk * a
    a_t = (-kk * e_prev).astype(BF16)
    r_t = (r * e_pos).astype(BF16)
    b_t = (b_vec * e_neg).astype(BF16)
    k_t = (k2 * e_neg).astype(BF16)
    b_h = (b_vec * e_rest).astype(BF16)
    k_h = (k2 * e_rest).astype(BF16)
    v16 = v.astype(BF16)

    strict_cat, incl_cat, eye_cat = _cat_masks()
    mask2 = jnp.concatenate([strict_cat, incl_cat], axis=0)

    chains = [(bi, q) for bi in range(bb) for q in range(RWKV_GROUPS)]
    rs_of = lambda bi: slice(bi * c, (bi + 1) * c)
    gs_of = lambda q: slice(q * MXU_TILE, (q + 1) * MXU_TILE)
    s_bds = [s_ref[bi, q] for bi, q in chains]
    pxs = []
    for (bi, q), s_bd in zip(chains, s_bds):
        rs, gs = rs_of(bi), gs_of(q)
        l2 = jnp.concatenate([a_t[rs, gs], r_t[rs, gs]], axis=0)
        w3 = jnp.concatenate([_bdiag(b_t[rs, gs], mask16), _bdiag(k_t[rs, gs], mask16), s_bd.astype(BF16)], axis=0)
        pxs.append(_dot_nt(l2, w3))
    pbs = [jnp.where(mask2, px[:, 0:MXU_TILE], 0.0) for px in pxs]
    pks = [jnp.where(mask2, px[:, MXU_TILE:2 * MXU_TILE], 0.0) for px in pxs]
    xas = [px[:, 2 * MXU_TILE:] + _dot(pk, _bdiag(v16[rs_of(bi), gs_of(q)], mask16))
           for (bi, q), px, pk in zip(chains, pxs, pks)]
    tinvs = _unit_lower_inverse_cat([pb[:c] for pb in pbs], eye_cat, mask16)
    us = [_dot(tinv, _bdiag(xa[:c], mask16)) for tinv, xa in zip(tinvs, xas)]
    ys = [xa[c:] + _dot(pb[c:], _bdiag(u, mask16)) for xa, pb, u in zip(xas, pbs, us)]
    for (bi, q), s_bd, u in zip(chains, s_bds, us):
        rs, gs = rs_of(bi), gs_of(q)
        uv = jnp.concatenate([u.astype(BF16), v16[rs, gs]], axis=0)
        bk = jnp.concatenate([b_h[rs, gs], k_h[rs, gs]], axis=0)
        s_ref[bi, q] = s_bd * w_tot[bi][:, gs] + _dot_tn(uv, bk) * maskf
    y_rows = [jnp.concatenate(ys[bi * RWKV_GROUPS:(bi + 1) * RWKV_GROUPS], axis=1) for bi in range(bb)]

    y = jnp.concatenate(y_rows, axis=0)
    inv_hd = 1.0 / hd
    (mean,) = head_sums(y)
    yc = y - mean * inv_hd
    (var,) = head_sums(yc * yc)
    yn = yc * lax.rsqrt(var * inv_hd + RWKV_GN_EPS) * lnw_ref[...] + lnb_ref[...]
    out = (yn + rk_sum * v) * g
    for bi in range(bb):
        y_ref[bi] = out[bi * c:(bi + 1) * c, :].astype(y_ref.dtype)


def _rwkv(p3, params, mask16, maskf, tril16, bb, rows_per_step):
    b, t, wd = p3.shape
    c = rows_per_step
    full = lambda arr: pl.BlockSpec(arr.shape, lambda i, j: (0,) * arr.ndim)
    return pl.pallas_call(
        _rwkv_body,
        grid=(b // bb, t // c),
        in_specs=[pl.BlockSpec((bb, c, wd), lambda i, j: (i, j, 0))] + [full(a) for a in params]
        + [full(mask16), full(maskf), full(tril16)],
        out_specs=pl.BlockSpec((bb, c, RWKV_WIDTH), lambda i, j: (i, j, 0)),
        out_shape=jax.ShapeDtypeStruct((b, t, RWKV_WIDTH), BF16),
        scratch_shapes=[
            pltpu.VMEM((bb, SUBLANES, wd), F32),
            pltpu.VMEM((bb, RWKV_GROUPS, MXU_TILE, MXU_TILE), F32),
        ],
        compiler_params=pltpu.CompilerParams(dimension_semantics=("arbitrary", "arbitrary"),
                                             vmem_limit_bytes=VMEM_LIMIT),
        name="rwkv",
    )(p3, *params, mask16, maskf, tril16)


def _gdn_body(qkv_ref, z_ref, ab_ref, *refs):
    *const_refs, y_ref, s_ref = refs

    @pl.when(pl.program_id(1) == 0)
    def _():
        s_ref[...] = jnp.zeros_like(s_ref)

    for r0 in range(0, qkv_ref.shape[1], CHUNK):
        rs = slice(r0, r0 + CHUNK)
        _gdn_chunk(qkv_ref.at[:, rs, :], z_ref.at[:, rs, :], ab_ref.at[:, rs, :], *const_refs, y_ref.at[:, rs, :], s_ref)


def _gdn_chunk(qkv_ref, z_ref, ab_ref, alog_ref, dtb_ref, nw_ref, m16_ref, hm16_ref, tril_ref, y_ref, s_ref):
    bb, c, _ = qkv_ref.shape
    hd = GDN_HEAD_DIM
    nh = GDN_HEADS
    qkv = jnp.concatenate([qkv_ref[bi] for bi in range(bb)], axis=0).astype(F32)
    w = GDN_WIDTH

    ab = jnp.concatenate([ab_ref[bi] for bi in range(bb)], axis=0)
    g_all = -jnp.exp(alog_ref[...]) * _softplus(ab + dtb_ref[...])
    beta_all = _sigmoid(ab)

    g_hi = g_all.astype(BF16)
    rem = g_all - g_hi.astype(F32)
    g_mid = rem.astype(BF16)
    g_lo = (rem - g_mid.astype(F32)).astype(BF16)
    gcp = _chunk_cumsum(tril_ref[...], jnp.concatenate([g_hi, g_mid, g_lo], axis=1))
    gc_all = gcp[:, 0:LANES] + gcp[:, LANES:2 * LANES] + gcp[:, 2 * LANES:3 * LANES]
    eg_all = jnp.exp(gc_all)

    mask16 = m16_ref[...]
    hmask16 = hm16_ref[...]
    strict_cat, incl_cat, eye_cat = _cat_masks()
    lane_head = lax.broadcasted_iota(jnp.int32, (c, MXU_TILE), 1) // CHUNK

    qs, ks, kbs = [], [], []
    for h in range(nh):
        qh = qkv[:, h * hd:(h + 1) * hd]
        kh = qkv[:, w + h * hd:w + (h + 1) * hd]
        qs.append(qh * lax.rsqrt(jnp.sum(qh * qh, axis=-1, keepdims=True) + L2_EPS) * (hd ** -0.5))
        kh = kh * lax.rsqrt(jnp.sum(kh * kh, axis=-1, keepdims=True) + L2_EPS)
        ks.append(kh)
        kbs.append(kh * beta_all[:, nh + h:nh + h + 1])
    k_all = jnp.concatenate(ks, axis=1).astype(BF16)
    kb_all = jnp.concatenate(kbs, axis=1).astype(BF16)
    q_all = jnp.concatenate(qs, axis=1).astype(BF16)

    rs_of = lambda bi: slice(bi * c, (bi + 1) * c)
    gc_bs = [gc_all[rs_of(bi)] for bi in range(bb)]
    glasts = [gc_b[c - 1:c, :] for gc_b in gc_bs]
    erests = [jnp.exp(glast - gc_b) for glast, gc_b in zip(glasts, gc_bs)]
    elasts = [jnp.exp(glast) for glast in glasts]

    decays = []
    for gc_b in gc_bs:
        f_cat = jnp.zeros((c, MXU_TILE), F32)
        for h in range(nh):
            f_cat = jnp.where(lane_head == h, gc_b[:, h:h + 1], f_cat)
        r_cat = jnp.sum(f_cat * eye_cat, axis=0, keepdims=True)
        decays.append(jnp.where(incl_cat, jnp.exp(jnp.where(incl_cat, f_cat - r_cat, 0.0)), 0.0))

    kqs = [_dot_nt(jnp.concatenate([kb_all[rs_of(bi)], q_all[rs_of(bi)]], axis=0),
                   jnp.concatenate([k_all[rs_of(bi)]] * nh, axis=0) * hmask16) for bi in range(bb)]
    attns = [kq[c:] * decay for kq, decay in zip(kqs, decays)]
    tinvs = _unit_lower_inverse_cat([-jnp.where(strict_cat, kq[:c] * decay, 0.0) for kq, decay in zip(kqs, decays)],
                                    eye_cat, mask16)

    chains = [(bi, h) for bi in range(bb) for h in range(nh)]
    tl_of = lambda h: slice(h * c, (h + 1) * c)
    egs = [eg_all[rs_of(bi), h:h + 1] for bi, h in chains]
    uws = []
    for (bi, h), eg in zip(chains, egs):
        rs = rs_of(bi)
        vh = qkv[rs, 2 * w + h * hd:2 * w + (h + 1) * hd]
        beta = beta_all[rs, nh + h:nh + h + 1]
        uws.append(_dot(tinvs[bi][:, tl_of(h)], jnp.concatenate([vh * beta, kbs[h][rs] * eg], axis=1)))
    ss = [s_ref[bi, h] for bi, h in chains]
    wqs = [_dot(jnp.concatenate([uw[:, hd:], qs[h][rs_of(bi)] * eg], axis=0), s)
           for (bi, h), uw, eg, s in zip(chains, uws, egs, ss)]
    v_news = [uw[:, :hd] - wq[:c] for uw, wq in zip(uws, wqs)]
    os_ = [wq[c:] + _dot(attns[bi][:, tl_of(h)], v_new) for (bi, h), wq, v_new in zip(chains, wqs, v_news)]
    for (bi, h), s, v_new in zip(chains, ss, v_news):
        s_ref[bi, h] = s * elasts[bi][:, h:h + 1] + _dot_tn(ks[h][rs_of(bi)] * erests[bi][:, h:h + 1], v_new)
    for (bi, h), o in zip(chains, os_):
        sl = slice(h * hd, (h + 1) * hd)
        on = o * lax.rsqrt(jnp.mean(o * o, axis=-1, keepdims=True) + NORM_EPS) * nw_ref[...]
        zh = z_ref[bi, :, sl].astype(F32)
        y_ref[bi, :, sl] = (on * (zh * _sigmoid(zh))).astype(y_ref.dtype)


def _gdn(qkv3, z3, ab3, alog, dtb, nw, mask16, hmask16, tril16, bb, rows_per_step):
    b, t, wd = qkv3.shape
    c = rows_per_step
    full = lambda arr: pl.BlockSpec(arr.shape, lambda i, j: (0,) * arr.ndim)
    return pl.pallas_call(
        _gdn_body,
        grid=(b // bb, t // c),
        in_specs=[
            pl.BlockSpec((bb, c, wd), lambda i, j: (i, j, 0)),
            pl.BlockSpec((bb, c, GDN_WIDTH), lambda i, j: (i, j, 0)),
            pl.BlockSpec((bb, c, AB_PAD), lambda i, j: (i, j, 0)),
            full(alog), full(dtb), full(nw), full(mask16), full(hmask16), full(tril16),
        ],
        out_specs=pl.BlockSpec((bb, c, GDN_WIDTH), lambda i, j: (i, j, 0)),
        out_shape=jax.ShapeDtypeStruct((b, t, GDN_WIDTH), BF16),
        scratch_shapes=[pltpu.VMEM((bb, GDN_HEADS, GDN_HEAD_DIM, GDN_HEAD_DIM), F32)],
        compiler_params=pltpu.CompilerParams(dimension_semantics=("arbitrary", "arbitrary"),
                                             vmem_limit_bytes=VMEM_LIMIT),
        name="gdn",
    )(qkv3, z3, ab3, alog, dtb, nw, mask16, hmask16, tril16)


def _mixffn_body(x_ref, ya_ref, yb_ref, gate_ref, *refs, tile_rows, col_block, apply_final):
    *const_refs, o_ref, prev_ref = refs

    @pl.when(pl.program_id(1) == 0)
    def _():
        prev_ref[...] = jnp.zeros_like(prev_ref)

    for r0 in range(0, x_ref.shape[0], tile_rows):
        rs = slice(r0, r0 + tile_rows)
        _mixffn_tile(x_ref.at[rs, :], ya_ref.at[rs, :], yb_ref.at[rs, :], gate_ref.at[rs, :], *const_refs,
                     o_ref.at[rs, :], prev_ref, col_block=col_block, apply_final=apply_final)


def _mixffn_tile(x_ref, ya_ref, yb_ref, gate_ref, rp_ref, gp_ref, wo_ref, g_ref, up_ref, cw_ref, down_ref, fg_ref,
                 o_ref, prev_ref, *, col_block, apply_final):
    tm, d = x_ref.shape
    hidden = down_ref.shape[0]

    ya = jnp.dot(ya_ref[...], rp_ref[...], preferred_element_type=F32)
    yb = jnp.dot(yb_ref[...], gp_ref[...], preferred_element_type=F32)
    mixed = _sigmoid(gate_ref[:, :d].astype(F32)) * ya + _sigmoid(gate_ref[:, d:].astype(F32)) * yb
    x = x_ref[...] + jnp.dot(mixed.astype(BF16), wo_ref[...], preferred_element_type=F32)

    ms = jnp.mean(x * x, axis=-1, keepdims=True)
    xn = (x * lax.rsqrt(ms + NORM_EPS) * g_ref[...]).astype(BF16)

    def up_cols(lo):
        return jnp.dot(xn, up_ref[:, lo:lo + col_block], preferred_element_type=F32)

    def conv_cols(h, lo):
        hp = jnp.concatenate([prev_ref[:, lo:lo + col_block], h], axis=0)
        out = h * cw_ref[FFN_CONV - 1:FFN_CONV, lo:lo + col_block]
        for i in range(FFN_CONV - 1):
            s = SUBLANES - (FFN_CONV - 1) + i
            out = out + hp[s:s + tm, :] * cw_ref[i:i + 1, lo:lo + col_block]
        prev_ref[:, lo:lo + col_block] = h[tm - SUBLANES:tm, :]
        return out

    nblk = hidden // col_block
    acc = x
    hs = (up_cols(0), up_cols(hidden))
    for j in range(nblk):
        lo = j * col_block
        nxt = (up_cols(lo + col_block), up_cols(hidden + lo + col_block)) if j + 1 < nblk else None
        hg = conv_cols(hs[0], lo)
        hu = conv_cols(hs[1], hidden + lo)
        act = (hg * _sigmoid(hg) * hu).astype(BF16)
        acc = acc + jnp.dot(act, down_ref[lo:lo + col_block, :], preferred_element_type=F32)
        hs = nxt
    if apply_final:
        ms2 = jnp.mean(acc * acc, axis=-1, keepdims=True)
        acc = acc * lax.rsqrt(ms2 + NORM_EPS) * fg_ref[...]
    o_ref[...] = acc


def _mixffn(x3, ya3, yb3, gate3, rp, gp, wo, gain, up, cw, down, final_g, tm, tiles_per_step, apply_final):
    b, t, d = x3.shape
    hidden = down.shape[0]
    full = lambda arr: pl.BlockSpec(arr.shape, lambda i, j: (0, 0))
    single = lambda arr: pl.BlockSpec(arr.shape, lambda i, j: (0, 0), pipeline_mode=pl.Buffered(1))
    rows = tm * tiles_per_step
    tile = lambda arr: pl.BlockSpec((None, rows, arr.shape[-1]), lambda i, j: (i, j, 0))
    body = functools.partial(_mixffn_body, tile_rows=tm, col_block=MXU_TILE, apply_final=apply_final)
    return pl.pallas_call(
        body,
        grid=(b, t // rows),
        in_specs=[tile(x3), tile(ya3), tile(yb3), tile(gate3), single(rp), single(gp), single(wo),
                  full(gain), single(up), full(cw), single(down), full(final_g)],
        out_specs=pl.BlockSpec((None, rows, d), lambda i, j: (i, j, 0)),
        out_shape=jax.ShapeDtypeStruct((b, t, d), F32),
        scratch_shapes=[pltpu.VMEM((SUBLANES, 2 * hidden), F32)],
        compiler_params=pltpu.CompilerParams(dimension_semantics=("arbitrary", "arbitrary"),
                                             vmem_limit_bytes=VMEM_LIMIT),
        name="mixffn",
    )(x3, ya3, yb3, gate3, rp, gp, wo, gain, up, cw, down, final_g)


def _row(vec):
    return vec.reshape(1, -1).astype(F32)


def _pad_lanes(vec, offset=0):
    out = jnp.zeros((1, LANES), F32)
    return lax.dynamic_update_slice(out, vec.reshape(1, -1).astype(F32), (0, offset))


def kernel(x, norm1_g, w_in, rwkv_mu, rwkv_w0, rwkv_w2, rwkv_a0, rwkv_a2, rwkv_g2, rwkv_k_k, rwkv_k_a,
           rwkv_r_k, rwkv_ln_w, rwkv_ln_b, rwkv_proj, gdn_conv_w, gdn_a_log, gdn_dt_bias, gdn_norm_w,
           gdn_proj, w_out, norm2_g, ffn_up, ffn_conv_w, ffn_down, final_g):
    b, t, d = x.shape
    n = b * t
    depth = norm1_g.shape[0]
    tm = 256
    tm_in = 512
    bb = next(c for c in (8, 4, 2, 1) if b % c == 0)
    scan_rows = 2 * CHUNK if t % (2 * CHUNK) == 0 else CHUNK
    ffn_tiles = 2 if t % (2 * tm) == 0 else 1
    assert t % tm_in == 0 and t % tm == 0 and t % CHUNK == 0, "sequence length must be a multiple of the tiles"
    assert (bb * CHUNK) % min(bb * CHUNK, MXU_TILE) == 0

    blk = jnp.arange(MXU_TILE) // CHUNK
    maskf = (blk[:, None] == blk[None, :]).astype(F32)
    mask16 = maskf.astype(BF16)
    rr = jnp.arange(min(bb * CHUNK, MXU_TILE))
    tril16 =((rr[:, None] >= rr[None, :]) & (rr[:, None] // CHUNK == rr[None, :] // CHUNK)).astype(BF16)
    hmask16 = (jnp.arange(GDN_HEADS * CHUNK)[:, None] // CHUNK == jnp.arange(GDN_WIDTH)[None, :] // GDN_HEAD_DIM).astype(BF16)

    widths = (RWKV_SHIFT_WIDTH, 3 * GDN_WIDTH, GDN_WIDTH, AB_PAD, 2 * d)
    dtypes = (BF16, BF16, BF16, F32, BF16)

    for l in range(depth):
        p_rwkv, qkv, z, ab, gate = _inproj(x.reshape(n, d), _row(norm1_g[l]), w_in[l].astype(BF16),
                                           gdn_conv_w[l].astype(F32), widths, dtypes, tm_in, tm, t // tm_in)

        rwkv_params = (_row(rwkv_mu[l]), _row(rwkv_w0[l]), rwkv_w2[l].astype(BF16), _row(rwkv_a0[l]),
                       rwkv_a2[l].astype(BF16), rwkv_g2[l].astype(BF16), _row(rwkv_k_k[l]), _row(rwkv_k_a[l]),
                       _row(rwkv_r_k[l]), _row(rwkv_ln_w[l]), _row(rwkv_ln_b[l]))
        y_a = _rwkv(p_rwkv.reshape(b, t, -1), rwkv_params, mask16, maskf, tril16, bb, scan_rows)

        y_b = _gdn(qkv.reshape(b, t, -1), z.reshape(b, t, -1), ab.reshape(b, t, -1),
                   _pad_lanes(gdn_a_log[l]), _pad_lanes(gdn_dt_bias[l]), _row(gdn_norm_w[l]), mask16, hmask16, tril16, bb,
                   scan_rows)

        x = _mixffn(x, y_a, y_b, gate.reshape(b, t, -1), rwkv_proj[l].astype(BF16), gdn_proj[l].astype(BF16),
                    w_out[l].astype(BF16), _row(norm2_g[l]), ffn_up[l].astype(BF16), ffn_conv_w[l].astype(F32),
                    ffn_down[l].astype(BF16), _row(final_g), tm, ffn_tiles, apply_final=(l == depth - 1))
    return x
```

```python
import functools
import math

import jax
import jax.numpy as jnp
from jax import lax
from jax.experimental import pallas as pl
from jax.experimental.pallas import tpu as pltpu

F32 = jnp.float32
BF16 = jnp.bfloat16

CHUNK = 64
RWKV_HEADS = 8
RWKV_HEAD_DIM = 64
RWKV_WIDTH = RWKV_HEADS * RWKV_HEAD_DIM
DECAY_LORA = 64
ICLR_LORA = 64
GATE_LORA = 128
GDN_HEADS = 4
GDN_HEAD_DIM = 128
GDN_WIDTH = GDN_HEADS * GDN_HEAD_DIM
GDN_CONV = 4
FFN_CONV = 3
NORM_EPS = 1e-6
L2_EPS = 1e-6
RWKV_GN_EPS = 64e-5
RWKV_SHIFT_WIDTH = 3 * RWKV_WIDTH + DECAY_LORA + ICLR_LORA + GATE_LORA
LANES = 128
SUBLANES = 8
MXU_TILE = 256
AB_PAD = LANES
GROUP = MXU_TILE // CHUNK
RWKV_GROUPS = RWKV_WIDTH // MXU_TILE
assert RWKV_HEAD_DIM == CHUNK and GDN_HEADS == GROUP and RWKV_HEADS % GROUP == 0

VMEM_LIMIT = 58 * 1024 * 1024


def _sigmoid(x):
    return 1.0 / (1.0 + jnp.exp(-x))


def _softplus(x):
    return jnp.maximum(x, 0.0) + jnp.log(1.0 + jnp.exp(-jnp.abs(x)))


def _dot(a, b):
    return jnp.dot(a.astype(BF16), b.astype(BF16), preferred_element_type=F32)


def _dot_nt(a, b):
    return lax.dot_general(a.astype(BF16), b.astype(BF16), (((1,), (1,)), ((), ())), preferred_element_type=F32)


def _dot_tn(a, b):
    return lax.dot_general(a.astype(BF16), b.astype(BF16), (((0,), (0,)), ((), ())), preferred_element_type=F32)


def _bdiag(x, mask16):
    reps = MXU_TILE // x.shape[0]
    return jnp.concatenate([x.astype(BF16)] * reps, axis=0) * mask16


def _chunk_cumsum(tril16, x16):
    tr = tril16.shape[0]
    return jnp.concatenate([jnp.dot(tril16, x16[r0:r0 + tr], preferred_element_type=F32)
                            for r0 in range(0, x16.shape[0], tr)], axis=0)


def _cat_masks():
    ri = lax.broadcasted_iota(jnp.int32, (CHUNK, MXU_TILE), 0)
    ci = lax.broadcasted_iota(jnp.int32, (CHUNK, MXU_TILE), 1) % CHUNK
    return ri > ci, ri >= ci, (ri == ci).astype(F32)


def _unit_lower_inverse_cat(ms, eye_cat, mask16):
    ts = [eye_cat + m for m in ms]
    mks = [_dot(m, _bdiag(m, mask16)) for m in ms]
    levels = CHUNK.bit_length() - 1
    for _ in range(levels - 2):
        boths = [_dot(jnp.concatenate([mk, t], axis=0), _bdiag(mk, mask16)) for mk, t in zip(mks, ts)]
        mks = [both[:CHUNK] for both in boths]
        ts = [t + both[CHUNK:] for t, both in zip(ts, boths)]
    return [t + _dot(t, _bdiag(mk, mask16)) for t, mk in zip(ts, mks)]


def _inproj_body(x_ref, g_ref, w_ref, cw_ref, o_rwkv, o_qkv, o_z, o_ab, o_gate, wg_ref, tail_ref, *, tile_rows,
                 steps_per_seq):
    @pl.when(pl.program_id(0) == 0)
    def _():
        wg_ref[...] = w_ref[:, w_ref.shape[-1] - o_gate.shape[-1]:]

    @pl.when(pl.program_id(0) % steps_per_seq == 0)
    def _():
        tail_ref[...] = jnp.zeros_like(tail_ref)

    for r0 in range(0, x_ref.shape[0], tile_rows):
        rs = slice(r0, r0 + tile_rows)
        _inproj_tile(x_ref.at[rs, :], g_ref, w_ref, cw_ref, *[o.at[rs, :] for o in (o_rwkv, o_qkv, o_z, o_ab, o_gate)],
                     wg_ref, tail_ref)


def _inproj_tile(x_ref, g_ref, w_ref, cw_ref, o_rwkv, o_qkv, o_z, o_ab, o_gate, wg_ref, tail_ref):
    ab_start = o_rwkv.shape[-1] + o_qkv.shape[-1] + o_z.shape[-1]
    x = x_ref[...]
    ms = jnp.mean(x * x, axis=-1, keepdims=True)
    u = (x * lax.rsqrt(ms + NORM_EPS) * g_ref[...]).astype(BF16)
    tm = x_ref.shape[0]
    w_r, w_q, w_z = o_rwkv.shape[-1], o_qkv.shape[-1], o_z.shape[-1]

    def conv_silu(h, lo):
        cols = slice(lo, lo + MXU_TILE)
        hp = jnp.concatenate([tail_ref[:, cols], h], axis=0)
        conv = h * cw_ref[GDN_CONV - 1:GDN_CONV, cols]
        for i in range(GDN_CONV - 1):
            s = SUBLANES - (GDN_CONV - 1) + i
            conv = conv + hp[s:s + tm, :] * cw_ref[i:i + 1, cols]
        tail_ref[:, cols] = h[tm - SUBLANES:tm, :]
        o_qkv[:, cols] = (conv * _sigmoid(conv)).astype(o_qkv.dtype)

    def project(ref, wref, out_lo, w_lo):
        ref[:, out_lo:out_lo + MXU_TILE] = jnp.dot(
            u, wref[:, w_lo:w_lo + MXU_TILE], preferred_element_type=F32).astype(ref.dtype)

    plain = [functools.partial(project, o_rwkv, w_ref, k, k) for k in range(0, w_r, MXU_TILE)]
    plain += [functools.partial(project, o_z, w_ref, k, w_r + w_q + k) for k in range(0, w_z, MXU_TILE)]
    plain += [functools.partial(project, o_gate, wg_ref, k, k) for k in range(0, o_gate.shape[-1], MXU_TILE)]
    qcol = lambda lo: jnp.dot(u, w_ref[:, w_r + lo:w_r + lo + MXU_TILE], preferred_element_type=F32)
    hq = qcol(0)
    for lo in range(0, w_q, MXU_TILE):
        nxt = qcol(lo + MXU_TILE) if lo + MXU_TILE < w_q else None
        for _ in range(2):
            plain.pop(0)()
        conv_silu(hq, lo)
        hq = nxt
    for emit in plain:
        emit()
    o_ab[...] = jnp.dot(u, w_ref[:, ab_start:ab_start + o_ab.shape[-1]], preferred_element_type=F32)


def _inproj(x2, gain, w16, cw, widths, dtypes, tm, tile_rows, steps_per_seq):
    n, d = x2.shape
    return pl.pallas_call(
        functools.partial(_inproj_body, tile_rows=tile_rows, steps_per_seq=steps_per_seq),
        grid=(n // tm,),
        in_specs=[
            pl.BlockSpec((tm, d), lambda i: (i, 0)),
            pl.BlockSpec((1, d), lambda i: (0, 0)),
            pl.BlockSpec(w16.shape, lambda i: (0, 0), pipeline_mode=pl.Buffered(1)),
            pl.BlockSpec(cw.shape, lambda i: (0, 0)),
        ],
        out_specs=[pl.BlockSpec((tm, w), lambda i: (i, 0)) for w in widths],
        out_shape=[jax.ShapeDtypeStruct((n, w), dt) for w, dt in zip(widths, dtypes)],
        scratch_shapes=[pltpu.VMEM((d, widths[-1]), BF16), pltpu.VMEM((SUBLANES, widths[1]), F32)],
        compiler_params=pltpu.CompilerParams(dimension_semantics=("arbitrary",), vmem_limit_bytes=VMEM_LIMIT),
        name="inproj",
    )(x2, gain, w16, cw)


def _rwkv_body(p_ref, *refs):
    *const_refs, y_ref, prev_ref, s_ref = refs

    @pl.when(pl.program_id(1) == 0)
    def _():
        prev_ref[...] = jnp.zeros_like(prev_ref)
        s_ref[...] = jnp.zeros_like(s_ref)

    for r0 in range(0, p_ref.shape[1], CHUNK):
        _rwkv_chunk(p_ref.at[:, r0:r0 + CHUNK, :], *const_refs, y_ref.at[:, r0:r0 + CHUNK, :], prev_ref, s_ref)


def _rwkv_chunk(p_ref, mu_ref, w0_ref, w2_ref, a0_ref, a2_ref, g2_ref, kk_ref, ka_ref, rk_ref, lnw_ref,
                lnb_ref, m16_ref, mf_ref, tril_ref, y_ref, prev_ref, s_ref):
    bb, c, wd = p_ref.shape
    rows = bb * c
    hd = RWKV_HEAD_DIM
    row8 =lax.broadcasted_iota(jnp.int32, (SUBLANES, 1), 0)
    ps, shifts = [], []
    for bi in range(bb):
        pb_ = p_ref[bi].astype(F32)
        rolled = pltpu.roll(pb_, 1, 0)
        shifts += [jnp.where(row8 == 0, prev_ref[bi, 0:1, :], rolled[:SUBLANES]), rolled[SUBLANES:]]
        prev_ref[bi, 0:1, :] = pb_[c - 1:c, :]
        ps.append(pb_)
    p = jnp.concatenate(ps, axis=0)
    pm = p + (jnp.concatenate(shifts, axis=0) - p) * mu_ref[...]

    w = RWKV_WIDTH
    r = pm[:, 0:w]
    k = pm[:, w:2 * w]
    v = pm[:, 2 * w:3 * w]
    o0 = 3 * w
    wl = pm[:, o0:o0 + DECAY_LORA]
    al = pm[:, o0 + DECAY_LORA:o0 + DECAY_LORA + ICLR_LORA]
    gl = pm[:, o0 + DECAY_LORA + ICLR_LORA:o0 + DECAY_LORA + ICLR_LORA + GATE_LORA]

    z = w0_ref[...] + _dot(jnp.tanh(wl), w2_ref[...])
    lw = (-math.exp(-0.5) * _sigmoid(z)).astype(BF16).astype(F32)
    a = _sigmoid(a0_ref[...] + _dot(al, a2_ref[...]))
    g = _dot(_sigmoid(gl), g2_ref[...])

    mask16 = m16_ref[...]
    maskf = mf_ref[...]

    def head_sums(*xs):
        stacked = jnp.concatenate([x[:, q * MXU_TILE:(q + 1) * MXU_TILE] for x in xs for q in range(RWKV_GROUPS)], axis=0)
        res = jnp.dot(stacked.astype(BF16), mask16, preferred_element_type=F32)
        outs = []
        for i in range(len(xs)):
            base = i * RWKV_GROUPS * rows
            outs.append(jnp.concatenate([res[base + q * rows:base + (q + 1) * rows] for q in range(RWKV_GROUPS)], axis=1))
        return outs

    kx = k * kk_ref[...]
    k2 = k * (1.0 + (a - 1.0) * ka_ref[...])
    kx_ss, rk_sum = head_sums(kx * kx, r * k2 * rk_ref[...])
    kk = kx * lax.rsqrt(kx_ss + L2_EPS)

    cum = _chunk_cumsum(tril_ref[...], lw.astype(BF16))
    e_pos = jnp.exp(cum)
    e_neg = 1.0 / e_pos
    e_prev = jnp.exp(cum - lw)
    w_tot = [e_pos[(bi + 1) * c - 1:(bi + 1) * c, :] for bi in range(bb)]
    e_rest = jnp.concatenate([e_neg[bi * c:(bi + 1) * c] * w_tot[bi] for bi in range(bb)], axis=0)

    b_vec = kk * a
    a_t = (-kk * e_prev).astype(BF16)
    r_t = (r * e_pos).astype(BF16)
    b_t = (b_vec * e_neg).astype(BF16)
    k_t = (k2 * e_neg).astype(BF16)
    b_h = (b_vec * e_rest).astype(BF16)
    k_h = (k2 * e_rest).astype(BF16)
    v16 = v.astype(BF16)

    strict_cat, incl_cat, eye_cat = _cat_masks()
    mask2 = jnp.concatenate([strict_cat, incl_cat], axis=0)

    chains = [(bi, q) for bi in range(bb) for q in range(RWKV_GROUPS)]
    rs_of = lambda bi: slice(bi * c, (bi + 1) * c)
    gs_of = lambda q: slice(q * MXU_TILE, (q + 1) * MXU_TILE)
    s_bds = [s_ref[bi, q] for bi, q in chains]
    pxs = []
    for (bi, q), s_bd in zip(chains, s_bds):
        rs, gs = rs_of(bi), gs_of(q)
        l2 = jnp.concatenate([a_t[rs, gs], r_t[rs, gs]], axis=0)
        w3 = jnp.concatenate([_bdiag(b_t[rs, gs], mask16), _bdiag(k_t[rs, gs], mask16), s_bd.astype(BF16)], axis=0)
        pxs.append(_dot_nt(l2, w3))
    pbs = [jnp.where(mask2, px[:, 0:MXU_TILE], 0.0) for px in pxs]
    pks = [jnp.where(mask2, px[:, MXU_TILE:2 * MXU_TILE], 0.0) for px in pxs]
    xas = [px[:, 2 * MXU_TILE:] + _dot(pk, _bdiag(v16[rs_of(bi), gs_of(q)], mask16))
           for (bi, q), px, pk in zip(chains, pxs, pks)]
    tinvs = _unit_lower_inverse_cat([pb[:c] for pb in pbs], eye_cat, mask16)
    us = [_dot(tinv, _bdiag(xa[:c], mask16)) for tinv, xa in zip(tinvs, xas)]
    ys = [xa[c:] + _dot(pb[c:], _bdiag(u, mask16)) for xa, pb, u in zip(xas, pbs, us)]
    for (bi, q), s_bd, u in zip(chains, s_bds, us):
        rs, gs = rs_of(bi), gs_of(q)
        uv = jnp.concatenate([u.astype(BF16), v16[rs, gs]], axis=0)
        bk = jnp.concatenate([b_h[rs, gs], k_h[rs, gs]], axis=0)
        s_ref[bi, q] = s_bd * w_tot[bi][:, gs] + _dot_tn(uv, bk) * maskf
    y_rows = [jnp.concatenate(ys[bi * RWKV_GROUPS:(bi + 1) * RWKV_GROUPS], axis=1) for bi in range(bb)]

    y = jnp.concatenate(y_rows, axis=0)
    inv_hd = 1.0 / hd
    (mean,) = head_sums(y)
    yc = y - mean * inv_hd
    (var,) = head_sums(yc * yc)
    yn = yc * lax.rsqrt(var * inv_hd + RWKV_GN_EPS) * lnw_ref[...] + lnb_ref[...]
    out = (yn + rk_sum * v) * g
    for bi in range(bb):
        y_ref[bi] = out[bi * c:(bi + 1) * c, :].astype(y_ref.dtype)


def _rwkv(p3, params, mask16, maskf, tril16, bb, rows_per_step):
    b, t, wd = p3.shape
    c = rows_per_step
    full = lambda arr: pl.BlockSpec(arr.shape, lambda i, j: (0,) * arr.ndim)
    return pl.pallas_call(
        _rwkv_body,
        grid=(b // bb, t // c),
        in_specs=[pl.BlockSpec((bb, c, wd), lambda i, j: (i, j, 0))] + [full(a) for a in params]
        + [full(mask16), full(maskf), full(tril16)],
        out_specs=pl.BlockSpec((bb, c, RWKV_WIDTH), lambda i, j: (i, j, 0)),
        out_shape=jax.ShapeDtypeStruct((b, t, RWKV_WIDTH), BF16),
        scratch_shapes=[
            pltpu.VMEM((bb, SUBLANES, wd), F32),
            pltpu.VMEM((bb, RWKV_GROUPS, MXU_TILE, MXU_TILE), F32),
        ],
        compiler_params=pltpu.CompilerParams(dimension_semantics=("arbitrary", "arbitrary"),
                                             vmem_limit_bytes=VMEM_LIMIT),
        name="rwkv",
    )(p3, *params, mask16, maskf, tril16)


def _gdn_body(qkv_ref, z_ref, ab_ref, *refs):
    *const_refs, y_ref, s_ref = refs

    @pl.when(pl.program_id(1) == 0)
    def _():
        s_ref[...] = jnp.zeros_like(s_ref)

    for r0 in range(0, qkv_ref.shape[1], CHUNK):
        rs = slice(r0, r0 + CHUNK)
        _gdn_chunk(qkv_ref.at[:, rs, :], z_ref.at[:, rs, :], ab_ref.at[:, rs, :], *const_refs, y_ref.at[:, rs, :], s_ref)


def _gdn_chunk(qkv_ref, z_ref, ab_ref, alog_ref, dtb_ref, nw_ref, m16_ref, hm16_ref, tril_ref, y_ref, s_ref):
    bb, c, _ = qkv_ref.shape
    hd = GDN_HEAD_DIM
    nh = GDN_HEADS
    qkv = jnp.concatenate([qkv_ref[bi] for bi in range(bb)], axis=0).astype(F32)
    w = GDN_WIDTH

    ab = jnp.concatenate([ab_ref[bi] for bi in range(bb)], axis=0)
    g_all = -jnp.exp(alog_ref[...]) * _softplus(ab + dtb_ref[...])
    beta_all = _sigmoid(ab)

    g_hi = g_all.astype(BF16)
    rem = g_all - g_hi.astype(F32)
    g_mid = rem.astype(BF16)
    g_lo = (rem - g_mid.astype(F32)).astype(BF16)
    gcp = _chunk_cumsum(tril_ref[...], jnp.concatenate([g_hi, g_mid, g_lo], axis=1))
    gc_all = gcp[:, 0:LANES] + gcp[:, LANES:2 * LANES] + gcp[:, 2 * LANES:3 * LANES]
    eg_all = jnp.exp(gc_all)

    mask16 = m16_ref[...]
    hmask16 = hm16_ref[...]
    strict_cat, incl_cat, eye_cat = _cat_masks()
    lane_head = lax.broadcasted_iota(jnp.int32, (c, MXU_TILE), 1) // CHUNK

    qs, ks, kbs = [], [], []
    for h in range(nh):
        qh = qkv[:, h * hd:(h + 1) * hd]
        kh = qkv[:, w + h * hd:w + (h + 1) * hd]
        qs.append(qh * lax.rsqrt(jnp.sum(qh * qh, axis=-1, keepdims=True) + L2_EPS) * (hd ** -0.5))
        kh = kh * lax.rsqrt(jnp.sum(kh * kh, axis=-1, keepdims=True) + L2_EPS)
        ks.append(kh)
        kbs.append(kh * beta_all[:, nh + h:nh + h + 1])
    k_all = jnp.concatenate(ks, axis=1).astype(BF16)
    kb_all = jnp.concatenate(kbs, axis=1).astype(BF16)
    q_all = jnp.concatenate(qs, axis=1).astype(BF16)

    rs_of = lambda bi: slice(bi * c, (bi + 1) * c)
    gc_bs = [gc_all[rs_of(bi)] for bi in range(bb)]
    glasts = [gc_b[c - 1:c, :] for gc_b in gc_bs]
    erests = [jnp.exp(glast - gc_b) for glast, gc_b in zip(glasts, gc_bs)]
    elasts = [jnp.exp(glast) for glast in glasts]

    decays = []
    for gc_b in gc_bs:
        f_cat = jnp.zeros((c, MXU_TILE), F32)
        for h in range(nh):
            f_cat = jnp.where(lane_head == h, gc_b[:, h:h + 1], f_cat)
        r_cat = jnp.sum(f_cat * eye_cat, axis=0, keepdims=True)
        decays.append(jnp.where(incl_cat, jnp.exp(jnp.where(incl_cat, f_cat - r_cat, 0.0)), 0.0))

    kqs = [_dot_nt(jnp.concatenate([kb_all[rs_of(bi)], q_all[rs_of(bi)]], axis=0),
                   jnp.concatenate([k_all[rs_of(bi)]] * nh, axis=0) * hmask16) for bi in range(bb)]
    attns = [kq[c:] * decay for kq, decay in zip(kqs, decays)]
    tinvs = _unit_lower_inverse_cat([-jnp.where(strict_cat, kq[:c] * decay, 0.0) for kq, decay in zip(kqs, decays)],
                                    eye_cat, mask16)

    chains = [(bi, h) for bi in range(bb) for h in range(nh)]
    tl_of = lambda h: slice(h * c, (h + 1) * c)
    egs = [eg_all[rs_of(bi), h:h + 1] for bi, h in chains]
    uws = []
    for (bi, h), eg in zip(chains, egs):
        rs = rs_of(bi)
        vh = qkv[rs, 2 * w + h * hd:2 * w + (h + 1) * hd]
        beta = beta_all[rs, nh + h:nh + h + 1]
        uws.append(_dot(tinvs[bi][:, tl_of(h)], jnp.concatenate([vh * beta, kbs[h][rs] * eg], axis=1)))
    ss = [s_ref[bi, h] for bi, h in chains]
    wqs = [_dot(jnp.concatenate([uw[:, hd:], qs[h][rs_of(bi)] * eg], axis=0), s)
           for (bi, h), uw, eg, s in zip(chains, uws, egs, ss)]
    v_news = [uw[:, :hd] - wq[:c] for uw, wq in zip(uws, wqs)]
    os_ = [wq[c:] + _dot(attns[bi][:, tl_of(h)], v_new) for (bi, h), wq, v_new in zip(chains, wqs, v_news)]
    for (bi, h), s, v_new in zip(chains, ss, v_news):
        s_ref[bi, h] = s * elasts[bi][:, h:h + 1] + _dot_tn(ks[h][rs_of(bi)] * erests[bi][:, h:h + 1], v_new)
    for (bi, h), o in zip(chains, os_):
        sl = slice(h * hd, (h + 1) * hd)
        on = o * lax.rsqrt(jnp.mean(o * o, axis=-1, keepdims=True) + NORM_EPS) * nw_ref[...]
        zh = z_ref[bi, :, sl].astype(F32)
        y_ref[bi, :, sl] = (on * (zh * _sigmoid(zh))).astype(y_ref.dtype)


def _gdn(qkv3, z3, ab3, alog, dtb, nw, mask16, hmask16, tril16, bb, rows_per_step):
    b, t, wd = qkv3.shape
    c = rows_per_step
    full = lambda arr: pl.BlockSpec(arr.shape, lambda i, j: (0,) * arr.ndim)
    return pl.pallas_call(
        _gdn_body,
        grid=(b // bb, t // c),
        in_specs=[
            pl.BlockSpec((bb, c, wd), lambda i, j: (i, j, 0)),
            pl.BlockSpec((bb, c, GDN_WIDTH), lambda i, j: (i, j, 0)),
            pl.BlockSpec((bb, c, AB_PAD), lambda i, j: (i, j, 0)),
            full(alog), full(dtb), full(nw), full(mask16), full(hmask16), full(tril16),
        ],
        out_specs=pl.BlockSpec((bb, c, GDN_WIDTH), lambda i, j: (i, j, 0)),
        out_shape=jax.ShapeDtypeStruct((b, t, GDN_WIDTH), BF16),
        scratch_shapes=[pltpu.VMEM((bb, GDN_HEADS, GDN_HEAD_DIM, GDN_HEAD_DIM), F32)],
        compiler_params=pltpu.CompilerParams(dimension_semantics=("arbitrary", "arbitrary"),
                                             vmem_limit_bytes=VMEM_LIMIT),
        name="gdn",
    )(qkv3, z3, ab3, alog, dtb, nw, mask16, hmask16, tril16)


def _mixffn_body(x_ref, ya_ref, yb_ref, gate_ref, *refs, tile_rows, col_block, apply_final):
    *const_refs, o_ref, prev_ref = refs

    @pl.when(pl.program_id(1) == 0)
    def _():
        prev_ref[...] = jnp.zeros_like(prev_ref)

    for r0 in range(0, x_ref.shape[0], tile_rows):
        rs = slice(r0, r0 + tile_rows)
        _mixffn_tile(x_ref.at[rs, :], ya_ref.at[rs, :], yb_ref.at[rs, :], gate_ref.at[rs, :], *const_refs,
                     o_ref.at[rs, :], prev_ref, col_block=col_block, apply_final=apply_final)


def _mixffn_tile(x_ref, ya_ref, yb_ref, gate_ref, rp_ref, gp_ref, wo_ref, g_ref, up_ref, cw_ref, down_ref, fg_ref,
                 o_ref, prev_ref, *, col_block, apply_final):
    tm, d = x_ref.shape
    hidden = down_ref.shape[0]

    ya = jnp.dot(ya_ref[...], rp_ref[...], preferred_element_type=F32)
    yb = jnp.dot(yb_ref[...], gp_ref[...], preferred_element_type=F32)
    mixed = _sigmoid(gate_ref[:, :d].astype(F32)) * ya + _sigmoid(gate_ref[:, d:].astype(F32)) * yb
    x = x_ref[...] + jnp.dot(mixed.astype(BF16), wo_ref[...], preferred_element_type=F32)

    ms = jnp.mean(x * x, axis=-1, keepdims=True)
    xn = (x * lax.rsqrt(ms + NORM_EPS) * g_ref[...]).astype(BF16)

    def up_cols(lo):
        return jnp.dot(xn, up_ref[:, lo:lo + col_block], preferred_element_type=F32)

    def conv_cols(h, lo):
        hp = jnp.concatenate([prev_ref[:, lo:lo + col_block], h], axis=0)
        out = h * cw_ref[FFN_CONV - 1:FFN_CONV, lo:lo + col_block]
        for i in range(FFN_CONV - 1):
            s = SUBLANES - (FFN_CONV - 1) + i
            out = out + hp[s:s + tm, :] * cw_ref[i:i + 1, lo:lo + col_block]
        prev_ref[:, lo:lo + col_block] = h[tm - SUBLANES:tm, :]
        return out

    nblk = hidden // col_block
    acc = x
    hs = (up_cols(0), up_cols(hidden))
    for j in range(nblk):
        lo = j * col_block
        nxt = (up_cols(lo + col_block), up_cols(hidden + lo + col_block)) if j + 1 < nblk else None
        hg = conv_cols(hs[0], lo)
        hu = conv_cols(hs[1], hidden + lo)
        act = (hg * _sigmoid(hg) * hu).astype(BF16)
        acc = acc + jnp.dot(act, down_ref[lo:lo + col_block, :], preferred_element_type=F32)
        hs = nxt
    if apply_final:
        ms2 = jnp.mean(acc * acc, axis=-1, keepdims=True)
        acc = acc * lax.rsqrt(ms2 + NORM_EPS) * fg_ref[...]
    o_ref[...] = acc


def _mixffn(x3, ya3, yb3, gate3, rp, gp, wo, gain, up, cw, down, final_g, tm, tiles_per_step, apply_final):
    b, t, d = x3.shape
    hidden = down.shape[0]
    full = lambda arr: pl.BlockSpec(arr.shape, lambda i, j: (0, 0))
    single = lambda arr: pl.BlockSpec(arr.shape, lambda i, j: (0, 0), pipeline_mode=pl.Buffered(1))
    rows = tm * tiles_per_step
    tile = lambda arr: pl.BlockSpec((None, rows, arr.shape[-1]), lambda i, j: (i, j, 0))
    body = functools.partial(_mixffn_body, tile_rows=tm, col_block=MXU_TILE, apply_final=apply_final)
    return pl.pallas_call(
        body,
        grid=(b, t // rows),
        in_specs=[tile(x3), tile(ya3), tile(yb3), tile(gate3), single(rp), single(gp), single(wo),
                  full(gain), single(up), full(cw), single(down), full(final_g)],
        out_specs=pl.BlockSpec((None, rows, d), lambda i, j: (i, j, 0)),
        out_shape=jax.ShapeDtypeStruct((b, t, d), F32),
        scratch_shapes=[pltpu.VMEM((SUBLANES, 2 * hidden), F32)],
        compiler_params=pltpu.CompilerParams(dimension_semantics=("arbitrary", "arbitrary"),
                                             vmem_limit_bytes=VMEM_LIMIT),
        name="mixffn",
    )(x3, ya3, yb3, gate3, rp, gp, wo, gain, up, cw, down, final_g)


def _row(vec):
    return vec.reshape(1, -1).astype(F32)


def _pad_lanes(vec, offset=0):
    out = jnp.zeros((1, LANES), F32)
    return lax.dynamic_update_slice(out, vec.reshape(1, -1).astype(F32), (0, offset))


def kernel(x, norm1_g, w_in, rwkv_mu, rwkv_w0, rwkv_w2, rwkv_a0, rwkv_a2, rwkv_g2, rwkv_k_k, rwkv_k_a,
           rwkv_r_k, rwkv_ln_w, rwkv_ln_b, rwkv_proj, gdn_conv_w, gdn_a_log, gdn_dt_bias, gdn_norm_w,
           gdn_proj, w_out, norm2_g, ffn_up, ffn_conv_w, ffn_down, final_g):
    b, t, d = x.shape
    n = b * t
    depth = norm1_g.shape[0]
    tm = 256
    tm_in = 1024
    bb = next(c for c in (8, 4, 2, 1) if b % c == 0)
    scan_rows = 2 * CHUNK if t % (2 * CHUNK) == 0 else CHUNK
    ffn_tiles = next(k for k in (4, 2, 1) if t % (k * tm) == 0)
    assert t % tm_in == 0 and t % tm == 0 and t % CHUNK == 0, "sequence length must be a multiple of the tiles"
    assert (bb * CHUNK) % min(bb * CHUNK, MXU_TILE) == 0

    blk = jnp.arange(MXU_TILE) // CHUNK
    maskf = (blk[:, None] == blk[None, :]).astype(F32)
    mask16 = maskf.astype(BF16)
    rr = jnp.arange(min(bb * CHUNK, MXU_TILE))
    tril16 =((rr[:, None] >= rr[None, :]) & (rr[:, None] // CHUNK == rr[None, :] // CHUNK)).astype(BF16)
    hmask16 = (jnp.arange(GDN_HEADS * CHUNK)[:, None] // CHUNK == jnp.arange(GDN_WIDTH)[None, :] // GDN_HEAD_DIM).astype(BF16)

    widths = (RWKV_SHIFT_WIDTH, 3 * GDN_WIDTH, GDN_WIDTH, AB_PAD, 2 * d)
    dtypes = (BF16, BF16, BF16, F32, BF16)

    for l in range(depth):
        p_rwkv, qkv, z, ab, gate = _inproj(x.reshape(n, d), _row(norm1_g[l]), w_in[l].astype(BF16),
                                           gdn_conv_w[l].astype(F32), widths, dtypes, tm_in, tm, t // tm_in)

        rwkv_params = (_row(rwkv_mu[l]), _row(rwkv_w0[l]), rwkv_w2[l].astype(BF16), _row(rwkv_a0[l]),
                       rwkv_a2[l].astype(BF16), rwkv_g2[l].astype(BF16), _row(rwkv_k_k[l]), _row(rwkv_k_a[l]),
                       _row(rwkv_r_k[l]), _row(rwkv_ln_w[l]), _row(rwkv_ln_b[l]))
        y_a = _rwkv(p_rwkv.reshape(b, t, -1), rwkv_params, mask16, maskf, tril16, bb, scan_rows)

        y_b = _gdn(qkv.reshape(b, t, -1), z.reshape(b, t, -1), ab.reshape(b, t, -1),
                   _pad_lanes(gdn_a_log[l]), _pad_lanes(gdn_dt_bias[l]), _row(gdn_norm_w[l]), mask16, hmask16, tril16, bb,
                   scan_rows)

        x = _mixffn(x, y_a, y_b, gate.reshape(b, t, -1), rwkv_proj[l].astype(BF16), gdn_proj[l].astype(BF16),
                    w_out[l].astype(BF16), _row(norm2_g[l]), ffn_up[l].astype(BF16), ffn_conv_w[l].astype(F32),
                    ffn_down[l].astype(BF16), _row(final_g), tm, ffn_tiles, apply_final=(l == depth - 1))
    return x
```

```python
import functools
import math

import jax
import jax.numpy as jnp
from jax import lax
from jax.experimental import pallas as pl
from jax.experimental.pallas import tpu as pltpu

F32 = jnp.float32
BF16 = jnp.bfloat16

CHUNK = 64
RWKV_HEADS = 8
RWKV_HEAD_DIM = 64
RWKV_WIDTH = RWKV_HEADS * RWKV_HEAD_DIM
DECAY_LORA = 64
ICLR_LORA = 64
GATE_LORA = 128
GDN_HEADS = 4
GDN_HEAD_DIM = 128
GDN_WIDTH = GDN_HEADS * GDN_HEAD_DIM
GDN_CONV = 4
FFN_CONV = 3
NORM_EPS = 1e-6
L2_EPS = 1e-6
RWKV_GN_EPS = 64e-5
RWKV_SHIFT_WIDTH = 3 * RWKV_WIDTH + DECAY_LORA + ICLR_LORA + GATE_LORA
LANES = 128
SUBLANES = 8
MXU_TILE = 256
AB_PAD = LANES
GROUP = MXU_TILE // CHUNK
RWKV_GROUPS = RWKV_WIDTH // MXU_TILE
assert RWKV_HEAD_DIM == CHUNK and GDN_HEADS == GROUP and RWKV_HEADS % GROUP == 0

VMEM_LIMIT = 58 * 1024 * 1024


def _sigmoid(x):
    return 1.0 / (1.0 + jnp.exp(-x))


def _softplus(x):
    return jnp.maximum(x, 0.0) + jnp.log(1.0 + jnp.exp(-jnp.abs(x)))


def _dot(a, b):
    return jnp.dot(a.astype(BF16), b.astype(BF16), preferred_element_type=F32)


def _dot_nt(a, b):
    return lax.dot_general(a.astype(BF16), b.astype(BF16), (((1,), (1,)), ((), ())), preferred_element_type=F32)


def _dot_tn(a, b):
    return lax.dot_general(a.astype(BF16), b.astype(BF16), (((0,), (0,)), ((), ())), preferred_element_type=F32)


def _bdiag(x, mask16):
    reps = MXU_TILE // x.shape[0]
    return jnp.concatenate([x.astype(BF16)] * reps, axis=0) * mask16


def _chunk_cumsum(tril16, x16):
    tr = tril16.shape[0]
    return jnp.concatenate([jnp.dot(tril16, x16[r0:r0 + tr], preferred_element_type=F32)
                            for r0 in range(0, x16.shape[0], tr)], axis=0)


def _cat_masks():
    ri = lax.broadcasted_iota(jnp.int32, (CHUNK, MXU_TILE), 0)
    ci = lax.broadcasted_iota(jnp.int32, (CHUNK, MXU_TILE), 1) % CHUNK
    return ri > ci, ri >= ci, (ri == ci).astype(F32)


def _unit_lower_inverse_cat(ms, eye_cat, mask16):
    ts = [eye_cat + m for m in ms]
    mks = [_dot(m, _bdiag(m, mask16)) for m in ms]
    levels = CHUNK.bit_length() - 1
    for _ in range(levels - 2):
        boths = [_dot(jnp.concatenate([mk, t], axis=0), _bdiag(mk, mask16)) for mk, t in zip(mks, ts)]
        mks = [both[:CHUNK] for both in boths]
        ts = [t + both[CHUNK:] for t, both in zip(ts, boths)]
    return [t + _dot(t, _bdiag(mk, mask16)) for t, mk in zip(ts, mks)]


def _inproj_body(x_ref, g_ref, w_ref, cw_ref, o_rwkv, o_qkv, o_z, o_ab, o_gate, wg_ref, tail_ref, *, tile_rows,
                 steps_per_seq):
    @pl.when(pl.program_id(0) == 0)
    def _():
        wg_ref[...] = w_ref[:, w_ref.shape[-1] - o_gate.shape[-1]:]

    @pl.when(pl.program_id(0) % steps_per_seq == 0)
    def _():
        tail_ref[...] = jnp.zeros_like(tail_ref)

    for r0 in range(0, x_ref.shape[0], tile_rows):
        rs = slice(r0, r0 + tile_rows)
        _inproj_tile(x_ref.at[rs, :], g_ref, w_ref, cw_ref, *[o.at[rs, :] for o in (o_rwkv, o_qkv, o_z, o_ab, o_gate)],
                     wg_ref, tail_ref)


def _inproj_tile(x_ref, g_ref, w_ref, cw_ref, o_rwkv, o_qkv, o_z, o_ab, o_gate, wg_ref, tail_ref):
    ab_start = o_rwkv.shape[-1] + o_qkv.shape[-1] + o_z.shape[-1]
    x = x_ref[...]
    ms = jnp.mean(x * x, axis=-1, keepdims=True)
    u = (x * lax.rsqrt(ms + NORM_EPS) * g_ref[...]).astype(BF16)
    tm = x_ref.shape[0]
    w_r, w_q, w_z = o_rwkv.shape[-1], o_qkv.shape[-1], o_z.shape[-1]

    def conv_silu(h, lo):
        cols = slice(lo, lo + MXU_TILE)
        hp = jnp.concatenate([tail_ref[:, cols], h], axis=0)
        conv = h * cw_ref[GDN_CONV - 1:GDN_CONV, cols]
        for i in range(GDN_CONV - 1):
            s = SUBLANES - (GDN_CONV - 1) + i
            conv = conv + hp[s:s + tm, :] * cw_ref[i:i + 1, cols]
        tail_ref[:, cols] = h[tm - SUBLANES:tm, :]
        o_qkv[:, cols] = (conv * _sigmoid(conv)).astype(o_qkv.dtype)

    def project(ref, wref, out_lo, w_lo):
        ref[:, out_lo:out_lo + MXU_TILE] = jnp.dot(
            u, wref[:, w_lo:w_lo + MXU_TILE], preferred_element_type=F32).astype(ref.dtype)

    plain = [functools.partial(project, o_rwkv, w_ref, k, k) for k in range(0, w_r, MXU_TILE)]
    plain += [functools.partial(project, o_z, w_ref, k, w_r + w_q + k) for k in range(0, w_z, MXU_TILE)]
    plain += [functools.partial(project, o_gate, wg_ref, k, k) for k in range(0, o_gate.shape[-1], MXU_TILE)]
    qcol = lambda lo: jnp.dot(u, w_ref[:, w_r + lo:w_r + lo + MXU_TILE], preferred_element_type=F32)
    hq = qcol(0)
    for lo in range(0, w_q, MXU_TILE):
        nxt = qcol(lo + MXU_TILE) if lo + MXU_TILE < w_q else None
        for _ in range(2):
            plain.pop(0)()
        conv_silu(hq, lo)
        hq = nxt
    for emit in plain:
        emit()
    o_ab[...] = jnp.dot(u, w_ref[:, ab_start:ab_start + o_ab.shape[-1]], preferred_element_type=F32)


def _inproj(x2, gain, w16, cw, widths, dtypes, tm, tile_rows, steps_per_seq):
    n, d = x2.shape
    return pl.pallas_call(
        functools.partial(_inproj_body, tile_rows=tile_rows, steps_per_seq=steps_per_seq),
        grid=(n // tm,),
        in_specs=[
            pl.BlockSpec((tm, d), lambda i: (i, 0)),
            pl.BlockSpec((1, d), lambda i: (0, 0)),
            pl.BlockSpec(w16.shape, lambda i: (0, 0), pipeline_mode=pl.Buffered(1)),
            pl.BlockSpec(cw.shape, lambda i: (0, 0)),
        ],
        out_specs=[pl.BlockSpec((tm, w), lambda i: (i, 0)) for w in widths],
        out_shape=[jax.ShapeDtypeStruct((n, w), dt) for w, dt in zip(widths, dtypes)],
        scratch_shapes=[pltpu.VMEM((d, widths[-1]), BF16), pltpu.VMEM((SUBLANES, widths[1]), F32)],
        compiler_params=pltpu.CompilerParams(dimension_semantics=("arbitrary",), vmem_limit_bytes=VMEM_LIMIT),
        name="inproj",
    )(x2, gain, w16, cw)


def _rwkv_body(p_ref, *refs):
    *const_refs, y_ref, prev_ref, s_ref = refs

    @pl.when(pl.program_id(1) == 0)
    def _():
        prev_ref[...] = jnp.zeros_like(prev_ref)
        s_ref[...] = jnp.zeros_like(s_ref)

    for r0 in range(0, p_ref.shape[1], CHUNK):
        _rwkv_chunk(p_ref.at[:, r0:r0 + CHUNK, :], *const_refs, y_ref.at[:, r0:r0 + CHUNK, :], prev_ref, s_ref)


def _rwkv_chunk(p_ref, mu_ref, w0_ref, w2_ref, a0_ref, a2_ref, g2_ref, kk_ref, ka_ref, rk_ref, lnw_ref,
                lnb_ref, m16_ref, mf_ref, tril_ref, y_ref, prev_ref, s_ref):
    bb, c, wd = p_ref.shape
    rows = bb * c
    hd = RWKV_HEAD_DIM
    row8 =lax.broadcasted_iota(jnp.int32, (SUBLANES, 1), 0)
    ps, shifts = [], []
    for bi in range(bb):
        pb_ = p_ref[bi].astype(F32)
        rolled = pltpu.roll(pb_, 1, 0)
        shifts += [jnp.where(row8 == 0, prev_ref[bi, 0:1, :], rolled[:SUBLANES]), rolled[SUBLANES:]]
        prev_ref[bi, 0:1, :] = pb_[c - 1:c, :]
        ps.append(pb_)
    p = jnp.concatenate(ps, axis=0)
    pm = p + (jnp.concatenate(shifts, axis=0) - p) * mu_ref[...]

    w = RWKV_WIDTH
    r = pm[:, 0:w]
    k = pm[:, w:2 * w]
    v = pm[:, 2 * w:3 * w]
    o0 = 3 * w
    wl = pm[:, o0:o0 + DECAY_LORA]
    al = pm[:, o0 + DECAY_LORA:o0 + DECAY_LORA + ICLR_LORA]
    gl = pm[:, o0 + DECAY_LORA + ICLR_LORA:o0 + DECAY_LORA + ICLR_LORA + GATE_LORA]

    z = w0_ref[...] + _dot(jnp.tanh(wl), w2_ref[...])
    lw = (-math.exp(-0.5) * _sigmoid(z)).astype(BF16).astype(F32)
    a = _sigmoid(a0_ref[...] + _dot(al, a2_ref[...]))
    g = _dot(_sigmoid(gl), g2_ref[...])

    mask16 = m16_ref[...]
    maskf = mf_ref[...]

    def head_sums(*xs):
        stacked = jnp.concatenate([x[:, q * MXU_TILE:(q + 1) * MXU_TILE] for x in xs for q in range(RWKV_GROUPS)], axis=0)
        res = jnp.dot(stacked.astype(BF16), mask16, preferred_element_type=F32)
        outs = []
        for i in range(len(xs)):
            base = i * RWKV_GROUPS * rows
            outs.append(jnp.concatenate([res[base + q * rows:base + (q + 1) * rows] for q in range(RWKV_GROUPS)], axis=1))
        return outs

    kx = k * kk_ref[...]
    k2 = k * (1.0 + (a - 1.0) * ka_ref[...])
    kx_ss, rk_sum = head_sums(kx * kx, r * k2 * rk_ref[...])
    kk = kx * lax.rsqrt(kx_ss + L2_EPS)

    cum = _chunk_cumsum(tril_ref[...], lw.astype(BF16))
    e_pos = jnp.exp(cum)
    e_neg = 1.0 / e_pos
    e_prev = jnp.exp(cum - lw)
    w_tot = [e_pos[(bi + 1) * c - 1:(bi + 1) * c, :] for bi in range(bb)]
    e_rest = jnp.concatenate([e_neg[bi * c:(bi + 1) * c] * w_tot[bi] for bi in range(bb)], axis=0)

    b_vec = kk * a
    a_t = (-kk * e_prev).astype(BF16)
    r_t = (r * e_pos).astype(BF16)
    b_t = (b_vec * e_neg).astype(BF16)
    k_t = (k2 * e_neg).astype(BF16)
    b_h = (b_vec * e_rest).astype(BF16)
    k_h = (k2 * e_rest).astype(BF16)
    v16 = v.astype(BF16)

    strict_cat, incl_cat, eye_cat = _cat_masks()
    mask2 = jnp.concatenate([strict_cat, incl_cat], axis=0)

    chains = [(bi, q) for bi in range(bb) for q in range(RWKV_GROUPS)]
    rs_of = lambda bi: slice(bi * c, (bi + 1) * c)
    gs_of = lambda q: slice(q * MXU_TILE, (q + 1) * MXU_TILE)
    s_bds = [s_ref[bi, q] for bi, q in chains]
    pxs = []
    for (bi, q), s_bd in zip(chains, s_bds):
        rs, gs = rs_of(bi), gs_of(q)
        l2 = jnp.concatenate([a_t[rs, gs], r_t[rs, gs]], axis=0)
        w3 = jnp.concatenate([_bdiag(b_t[rs, gs], mask16), _bdiag(k_t[rs, gs], mask16), s_bd.astype(BF16)], axis=0)
        pxs.append(_dot_nt(l2, w3))
    pbs = [jnp.where(mask2, px[:, 0:MXU_TILE], 0.0) for px in pxs]
    pks = [jnp.where(mask2, px[:, MXU_TILE:2 * MXU_TILE], 0.0) for px in pxs]
    xas = [px[:, 2 * MXU_TILE:] + _dot(pk, _bdiag(v16[rs_of(bi), gs_of(q)], mask16))
           for (bi, q), px, pk in zip(chains, pxs, pks)]
    tinvs = _unit_lower_inverse_cat([pb[:c] for pb in pbs], eye_cat, mask16)
    us = [_dot(tinv, _bdiag(xa[:c], mask16)) for tinv, xa in zip(tinvs, xas)]
    ys = [xa[c:] + _dot(pb[c:], _bdiag(u, mask16)) for xa, pb, u in zip(xas, pbs, us)]
    for (bi, q), s_bd, u in zip(chains, s_bds, us):
        rs, gs = rs_of(bi), gs_of(q)
        uv = jnp.concatenate([u.astype(BF16), v16[rs, gs]], axis=0)
        bk = jnp.concatenate([b_h[rs, gs], k_h[rs, gs]], axis=0)
        s_ref[bi, q] = s_bd * w_tot[bi][:, gs] + _dot_tn(uv, bk) * maskf
    y_rows = [jnp.concatenate(ys[bi * RWKV_GROUPS:(bi + 1) * RWKV_GROUPS], axis=1) for bi in range(bb)]

    y = jnp.concatenate(y_rows, axis=0)
    inv_hd = 1.0 / hd
    (mean,) = head_sums(y)
    yc = y - mean * inv_hd
    (var,) = head_sums(yc * yc)
    yn = yc * lax.rsqrt(var * inv_hd + RWKV_GN_EPS) * lnw_ref[...] + lnb_ref[...]
    out = (yn + rk_sum * v) * g
    for bi in range(bb):
        y_ref[bi] = out[bi * c:(bi + 1) * c, :].astype(y_ref.dtype)


def _rwkv(p3, params, mask16, maskf, tril16, bb, rows_per_step):
    b, t, wd = p3.shape
    c = rows_per_step
    full = lambda arr: pl.BlockSpec(arr.shape, lambda i, j: (0,) * arr.ndim)
    return pl.pallas_call(
        _rwkv_body,
        grid=(b // bb, t // c),
        in_specs=[pl.BlockSpec((bb, c, wd), lambda i, j: (i, j, 0))] + [full(a) for a in params]
        + [full(mask16), full(maskf), full(tril16)],
        out_specs=pl.BlockSpec((bb, c, RWKV_WIDTH), lambda i, j: (i, j, 0)),
        out_shape=jax.ShapeDtypeStruct((b, t, RWKV_WIDTH), BF16),
        scratch_shapes=[
            pltpu.VMEM((bb, SUBLANES, wd), F32),
            pltpu.VMEM((bb, RWKV_GROUPS, MXU_TILE, MXU_TILE), F32),
        ],
        compiler_params=pltpu.CompilerParams(dimension_semantics=("arbitrary", "arbitrary"),
                                             vmem_limit_bytes=VMEM_LIMIT),
        name="rwkv",
    )(p3, *params, mask16, maskf, tril16)


def _gdn_body(qkv_ref, z_ref, ab_ref, *refs):
    *const_refs, y_ref, s_ref = refs

    @pl.when(pl.program_id(1) == 0)
    def _():
        s_ref[...] = jnp.zeros_like(s_ref)

    for r0 in range(0, qkv_ref.shape[1], CHUNK):
        rs = slice(r0, r0 + CHUNK)
        _gdn_chunk(qkv_ref.at[:, rs, :], z_ref.at[:, rs, :], ab_ref.at[:, rs, :], *const_refs, y_ref.at[:, rs, :], s_ref)


def _gdn_chunk(qkv_ref, z_ref, ab_ref, alog_ref, dtb_ref, nw_ref, m16_ref, hm16_ref, tril_ref, y_ref, s_ref):
    bb, c, _ = qkv_ref.shape
    hd = GDN_HEAD_DIM
    nh = GDN_HEADS
    qkv = jnp.concatenate([qkv_ref[bi] for bi in range(bb)], axis=0).astype(F32)
    w = GDN_WIDTH

    ab = jnp.concatenate([ab_ref[bi] for bi in range(bb)], axis=0)
    g_all = -jnp.exp(alog_ref[...]) * _softplus(ab + dtb_ref[...])
    beta_all = _sigmoid(ab)

    g_hi = g_all.astype(BF16)
    rem = g_all - g_hi.astype(F32)
    g_mid = rem.astype(BF16)
    g_lo = (rem - g_mid.astype(F32)).astype(BF16)
    gcp = _chunk_cumsum(tril_ref[...], jnp.concatenate([g_hi, g_mid, g_lo], axis=1))
    gc_all = gcp[:, 0:LANES] + gcp[:, LANES:2 * LANES] + gcp[:, 2 * LANES:3 * LANES]
    eg_all = jnp.exp(gc_all)

    mask16 = m16_ref[...]
    hmask16 = hm16_ref[...]
    strict_cat, incl_cat, eye_cat = _cat_masks()
    lane_head = lax.broadcasted_iota(jnp.int32, (c, MXU_TILE), 1) // CHUNK

    qs, ks, kbs = [], [], []
    for h in range(nh):
        qh = qkv[:, h * hd:(h + 1) * hd]
        kh = qkv[:, w + h * hd:w + (h + 1) * hd]
        qs.append(qh * lax.rsqrt(jnp.sum(qh * qh, axis=-1, keepdims=True) + L2_EPS) * (hd ** -0.5))
        kh = kh * lax.rsqrt(jnp.sum(kh * kh, axis=-1, keepdims=True) + L2_EPS)
        ks.append(kh)
        kbs.append(kh * beta_all[:, nh + h:nh + h + 1])
    k_all = jnp.concatenate(ks, axis=1).astype(BF16)
    kb_all = jnp.concatenate(kbs, axis=1).astype(BF16)
    q_all = jnp.concatenate(qs, axis=1).astype(BF16)

    rs_of = lambda bi: slice(bi * c, (bi + 1) * c)
    gc_bs = [gc_all[rs_of(bi)] for bi in range(bb)]
    glasts = [gc_b[c - 1:c, :] for gc_b in gc_bs]
    erests = [jnp.exp(glast - gc_b) for glast, gc_b in zip(glasts, gc_bs)]
    elasts = [jnp.exp(glast) for glast in glasts]

    decays = []
    for gc_b in gc_bs:
        f_cat = jnp.zeros((c, MXU_TILE), F32)
        for h in range(nh):
            f_cat = jnp.where(lane_head == h, gc_b[:, h:h + 1], f_cat)
        r_cat = jnp.sum(f_cat * eye_cat, axis=0, keepdims=True)
        decays.append(jnp.where(incl_cat, jnp.exp(jnp.where(incl_cat, f_cat - r_cat, 0.0)), 0.0))

    kqs = [_dot_nt(jnp.concatenate([kb_all[rs_of(bi)], q_all[rs_of(bi)]], axis=0),
                   jnp.concatenate([k_all[rs_of(bi)]] * nh, axis=0) * hmask16) for bi in range(bb)]
    attns = [kq[c:] * decay for kq, decay in zip(kqs, decays)]
    tinvs = _unit_lower_inverse_cat([-jnp.where(strict_cat, kq[:c] * decay, 0.0) for kq, decay in zip(kqs, decays)],
                                    eye_cat, mask16)

    chains = [(bi, h) for bi in range(bb) for h in range(nh)]
    tl_of = lambda h: slice(h * c, (h + 1) * c)
    egs = [eg_all[rs_of(bi), h:h + 1] for bi, h in chains]
    uws = []
    for (bi, h), eg in zip(chains, egs):
        rs = rs_of(bi)
        vh = qkv[rs, 2 * w + h * hd:2 * w + (h + 1) * hd]
        beta = beta_all[rs, nh + h:nh + h + 1]
        uws.append(_dot(tinvs[bi][:, tl_of(h)], jnp.concatenate([vh * beta, kbs[h][rs] * eg], axis=1)))
    ss = [s_ref[bi, h] for bi, h in chains]
    wqs = [_dot(jnp.concatenate([uw[:, hd:], qs[h][rs_of(bi)] * eg], axis=0), s)
           for (bi, h), uw, eg, s in zip(chains, uws, egs, ss)]
    v_news = [uw[:, :hd] - wq[:c] for uw, wq in zip(uws, wqs)]
    os_ = [wq[c:] + _dot(attns[bi][:, tl_of(h)], v_new) for (bi, h), wq, v_new in zip(chains, wqs, v_news)]
    for (bi, h), s, v_new in zip(chains, ss, v_news):
        s_ref[bi, h] = s * elasts[bi][:, h:h + 1] + _dot_tn(ks[h][rs_of(bi)] * erests[bi][:, h:h + 1], v_new)
    for (bi, h), o in zip(chains, os_):
        sl = slice(h * hd, (h + 1) * hd)
        on = o * lax.rsqrt(jnp.mean(o * o, axis=-1, keepdims=True) + NORM_EPS) * nw_ref[...]
        zh = z_ref[bi, :, sl].astype(F32)
        y_ref[bi, :, sl] = (on * (zh * _sigmoid(zh))).astype(y_ref.dtype)


def _gdn(qkv3, z3, ab3, alog, dtb, nw, mask16, hmask16, tril16, bb, rows_per_step):
    b, t, wd = qkv3.shape
    c = rows_per_step
    full = lambda arr: pl.BlockSpec(arr.shape, lambda i, j: (0,) * arr.ndim)
    return pl.pallas_call(
        _gdn_body,
        grid=(b // bb, t // c),
        in_specs=[
            pl.BlockSpec((bb, c, wd), lambda i, j: (i, j, 0)),
            pl.BlockSpec((bb, c, GDN_WIDTH), lambda i, j: (i, j, 0)),
            pl.BlockSpec((bb, c, AB_PAD), lambda i, j: (i, j, 0)),
            full(alog), full(dtb), full(nw), full(mask16), full(hmask16), full(tril16),
        ],
        out_specs=pl.BlockSpec((bb, c, GDN_WIDTH), lambda i, j: (i, j, 0)),
        out_shape=jax.ShapeDtypeStruct((b, t, GDN_WIDTH), BF16),
        scratch_shapes=[pltpu.VMEM((bb, GDN_HEADS, GDN_HEAD_DIM, GDN_HEAD_DIM), F32)],
        compiler_params=pltpu.CompilerParams(dimension_semantics=("arbitrary", "arbitrary"),
                                             vmem_limit_bytes=VMEM_LIMIT),
        name="gdn",
    )(qkv3, z3, ab3, alog, dtb, nw, mask16, hmask16, tril16)


def _mixffn_body(x_ref, ya_ref, yb_ref, gate_ref, *refs, tile_rows, col_block, apply_final):
    *const_refs, o_ref, prev_ref = refs

    @pl.when(pl.program_id(1) == 0)
    def _():
        prev_ref[...] = jnp.zeros_like(prev_ref)

    for r0 in range(0, x_ref.shape[0], tile_rows):
        rs = slice(r0, r0 + tile_rows)
        _mixffn_tile(x_ref.at[rs, :], ya_ref.at[rs, :], yb_ref.at[rs, :], gate_ref.at[rs, :], *const_refs,
                     o_ref.at[rs, :], prev_ref, col_block=col_block, apply_final=apply_final)


def _mixffn_tile(x_ref, ya_ref, yb_ref, gate_ref, rp_ref, gp_ref, wo_ref, g_ref, up_ref, cw_ref, down_ref, fg_ref,
                 o_ref, prev_ref, *, col_block, apply_final):
    tm, d = x_ref.shape
    hidden = down_ref.shape[0]

    ya = jnp.dot(ya_ref[...], rp_ref[...], preferred_element_type=F32)
    yb = jnp.dot(yb_ref[...], gp_ref[...], preferred_element_type=F32)
    mixed = _sigmoid(gate_ref[:, :d].astype(F32)) * ya + _sigmoid(gate_ref[:, d:].astype(F32)) * yb
    x = x_ref[...] + jnp.dot(mixed.astype(BF16), wo_ref[...], preferred_element_type=F32)

    ms = jnp.mean(x * x, axis=-1, keepdims=True)
    xn = (x * lax.rsqrt(ms + NORM_EPS) * g_ref[...]).astype(BF16)

    def up_cols(lo):
        return jnp.dot(xn, up_ref[:, lo:lo + col_block], preferred_element_type=F32)

    def conv_cols(h, lo):
        hp = jnp.concatenate([prev_ref[:, lo:lo + col_block], h], axis=0)
        out = h * cw_ref[FFN_CONV - 1:FFN_CONV, lo:lo + col_block]
        for i in range(FFN_CONV - 1):
            s = SUBLANES - (FFN_CONV - 1) + i
            out = out + hp[s:s + tm, :] * cw_ref[i:i + 1, lo:lo + col_block]
        prev_ref[:, lo:lo + col_block] = h[tm - SUBLANES:tm, :]
        return out

    nblk = hidden // col_block
    acc = x
    hs = (up_cols(0), up_cols(hidden))
    for j in range(nblk):
        lo = j * col_block
        nxt = (up_cols(lo + col_block), up_cols(hidden + lo + col_block)) if j + 1 < nblk else None
        hg = conv_cols(hs[0], lo)
        hu = conv_cols(hs[1], hidden + lo)
        act = (hg * _sigmoid(hg) * hu).astype(BF16)
        acc = acc + jnp.dot(act, down_ref[lo:lo + col_block, :], preferred_element_type=F32)
        hs = nxt
    if apply_final:
        ms2 = jnp.mean(acc * acc, axis=-1, keepdims=True)
        acc = acc * lax.rsqrt(ms2 + NORM_EPS) * fg_ref[...]
    o_ref[...] = acc


def _mixffn(x3, ya3, yb3, gate3, rp, gp, wo, gain, up, cw, down, final_g, tm, tiles_per_step, apply_final):
    b, t, d = x3.shape
    hidden = down.shape[0]
    full = lambda arr: pl.BlockSpec(arr.shape, lambda i, j: (0, 0))
    single = lambda arr: pl.BlockSpec(arr.shape, lambda i, j: (0, 0), pipeline_mode=pl.Buffered(1))
    rows = tm * tiles_per_step
    tile = lambda arr: pl.BlockSpec((None, rows, arr.shape[-1]), lambda i, j: (i, j, 0))
    body = functools.partial(_mixffn_body, tile_rows=tm, col_block=MXU_TILE, apply_final=apply_final)
    return pl.pallas_call(
        body,
        grid=(b, t // rows),
        in_specs=[tile(x3), tile(ya3), tile(yb3), tile(gate3), single(rp), single(gp), single(wo),
                  full(gain), single(up), full(cw), single(down), full(final_g)],
        out_specs=pl.BlockSpec((None, rows, d), lambda i, j: (i, j, 0)),
        out_shape=jax.ShapeDtypeStruct((b, t, d), F32),
        scratch_shapes=[pltpu.VMEM((SUBLANES, 2 * hidden), F32)],
        compiler_params=pltpu.CompilerParams(dimension_semantics=("arbitrary", "arbitrary"),
                                             vmem_limit_bytes=VMEM_LIMIT),
        name="mixffn",
    )(x3, ya3, yb3, gate3, rp, gp, wo, gain, up, cw, down, final_g)


def _row(vec):
    return vec.reshape(1, -1).astype(F32)


def _pad_lanes(vec, offset=0):
    out = jnp.zeros((1, LANES), F32)
    return lax.dynamic_update_slice(out, vec.reshape(1, -1).astype(F32), (0, offset))


def kernel(x, norm1_g, w_in, rwkv_mu, rwkv_w0, rwkv_w2, rwkv_a0, rwkv_a2, rwkv_g2, rwkv_k_k, rwkv_k_a,
           rwkv_r_k, rwkv_ln_w, rwkv_ln_b, rwkv_proj, gdn_conv_w, gdn_a_log, gdn_dt_bias, gdn_norm_w,
           gdn_proj, w_out, norm2_g, ffn_up, ffn_conv_w, ffn_down, final_g):
    b, t, d = x.shape
    n = b * t
    depth = norm1_g.shape[0]
    tm = 256
    tm_in = 1024
    bb = next(c for c in (8, 4, 2, 1) if b % c == 0)
    scan_rows = 2 * CHUNK if t % (2 * CHUNK) == 0 else CHUNK
    ffn_tiles = 2 if t % (2 * tm) == 0 else 1
    assert t % tm_in == 0 and t % tm == 0 and t % CHUNK == 0, "sequence length must be a multiple of the tiles"
    assert (bb * CHUNK) % min(bb * CHUNK, MXU_TILE) == 0

    blk = jnp.arange(MXU_TILE) // CHUNK
    maskf = (blk[:, None] == blk[None, :]).astype(F32)
    mask16 = maskf.astype(BF16)
    rr = jnp.arange(min(bb * CHUNK, MXU_TILE))
    tril16 =((rr[:, None] >= rr[None, :]) & (rr[:, None] // CHUNK == rr[None, :] // CHUNK)).astype(BF16)
    hmask16 = (jnp.arange(GDN_HEADS * CHUNK)[:, None] // CHUNK == jnp.arange(GDN_WIDTH)[None, :] // GDN_HEAD_DIM).astype(BF16)

    widths = (RWKV_SHIFT_WIDTH, 3 * GDN_WIDTH, GDN_WIDTH, AB_PAD, 2 * d)
    dtypes = (BF16, BF16, BF16, F32, BF16)

    for l in range(depth):
        p_rwkv, qkv, z, ab, gate = _inproj(x.reshape(n, d), _row(norm1_g[l]), w_in[l].astype(BF16),
                                           gdn_conv_w[l].astype(F32), widths, dtypes, tm_in, tm, t // tm_in)

        rwkv_params = (_row(rwkv_mu[l]), _row(rwkv_w0[l]), rwkv_w2[l].astype(BF16), _row(rwkv_a0[l]),
                       rwkv_a2[l].astype(BF16), rwkv_g2[l].astype(BF16), _row(rwkv_k_k[l]), _row(rwkv_k_a[l]),
                       _row(rwkv_r_k[l]), _row(rwkv_ln_w[l]), _row(rwkv_ln_b[l]))
        y_a = _rwkv(p_rwkv.reshape(b, t, -1), rwkv_params, mask16, maskf, tril16, bb, scan_rows)

        y_b = _gdn(qkv.reshape(b, t, -1), z.reshape(b, t, -1), ab.reshape(b, t, -1),
                   _pad_lanes(gdn_a_log[l]), _pad_lanes(gdn_dt_bias[l]), _row(gdn_norm_w[l]), mask16, hmask16, tril16, bb,
                   scan_rows)

        x = _mixffn(x, y_a, y_b, gate.reshape(b, t, -1), rwkv_proj[l].astype(BF16), gdn_proj[l].astype(BF16),
                    w_out[l].astype(BF16), _row(norm2_g[l]), ffn_up[l].astype(BF16), ffn_conv_w[l].astype(F32),
                    ffn_down[l].astype(BF16), _row(final_g), tm, ffn_tiles, apply_final=(l == depth - 1))
    return x
```

```python
import functools
import math

import jax
import jax.numpy as jnp
from jax import lax
from jax.experimental import pallas as pl
from jax.experimental.pallas import tpu as pltpu

F32 = jnp.float32
BF16 = jnp.bfloat16

CHUNK = 64
RWKV_HEADS = 8
RWKV_HEAD_DIM = 64
RWKV_WIDTH = RWKV_HEADS * RWKV_HEAD_DIM
DECAY_LORA = 64
ICLR_LORA = 64
GATE_LORA = 128
GDN_HEADS = 4
GDN_HEAD_DIM = 128
GDN_WIDTH = GDN_HEADS * GDN_HEAD_DIM
GDN_CONV = 4
FFN_CONV = 3
NORM_EPS = 1e-6
L2_EPS = 1e-6
RWKV_GN_EPS = 64e-5
RWKV_SHIFT_WIDTH = 3 * RWKV_WIDTH + DECAY_LORA + ICLR_LORA + GATE_LORA
LANES = 128
SUBLANES = 8
MXU_TILE = 256
AB_PAD = LANES
GROUP = MXU_TILE // CHUNK
RWKV_GROUPS = RWKV_WIDTH // MXU_TILE
assert RWKV_HEAD_DIM == CHUNK and GDN_HEADS == GROUP and RWKV_HEADS % GROUP == 0

VMEM_LIMIT = 58 * 1024 * 1024


def _sigmoid(x):
    return 1.0 / (1.0 + jnp.exp(-x))


def _softplus(x):
    return jnp.maximum(x, 0.0) + jnp.log(1.0 + jnp.exp(-jnp.abs(x)))


def _dot(a, b):
    return jnp.dot(a.astype(BF16), b.astype(BF16), preferred_element_type=F32)


def _dot_nt(a, b):
    return lax.dot_general(a.astype(BF16), b.astype(BF16), (((1,), (1,)), ((), ())), preferred_element_type=F32)


def _dot_tn(a, b):
    return lax.dot_general(a.astype(BF16), b.astype(BF16), (((0,), (0,)), ((), ())), preferred_element_type=F32)


def _bdiag(x, mask16):
    reps = MXU_TILE // x.shape[0]
    return jnp.concatenate([x.astype(BF16)] * reps, axis=0) * mask16


def _chunk_cumsum(tril16, x16):
    tr = tril16.shape[0]
    return jnp.concatenate([jnp.dot(tril16, x16[r0:r0 + tr], preferred_element_type=F32)
                            for r0 in range(0, x16.shape[0], tr)], axis=0)


def _cat_masks():
    ri = lax.broadcasted_iota(jnp.int32, (CHUNK, MXU_TILE), 0)
    ci = lax.broadcasted_iota(jnp.int32, (CHUNK, MXU_TILE), 1) % CHUNK
    return ri > ci, ri >= ci, (ri == ci).astype(F32)


def _unit_lower_inverse_cat(ms, eye_cat, mask16):
    ts = [eye_cat + m for m in ms]
    mks = [_dot(m, _bdiag(m, mask16)) for m in ms]
    levels = CHUNK.bit_length() - 1
    for _ in range(levels - 2):
        boths = [_dot(jnp.concatenate([mk, t], axis=0), _bdiag(mk, mask16)) for mk, t in zip(mks, ts)]
        mks = [both[:CHUNK] for both in boths]
        ts = [t + both[CHUNK:] for t, both in zip(ts, boths)]
    return [t + _dot(t, _bdiag(mk, mask16)) for t, mk in zip(ts, mks)]


def _inproj_body(x_ref, g_ref, w_ref, cw_ref, o_rwkv, o_qkv, o_z, o_ab, o_gate, wg_ref, tail_ref, *, tile_rows,
                 steps_per_seq):
    @pl.when(pl.program_id(0) == 0)
    def _():
        wg_ref[...] = w_ref[:, w_ref.shape[-1] - o_gate.shape[-1]:]

    @pl.when(pl.program_id(0) % steps_per_seq == 0)
    def _():
        tail_ref[...] = jnp.zeros_like(tail_ref)

    for r0 in range(0, x_ref.shape[0], tile_rows):
        rs = slice(r0, r0 + tile_rows)
        _inproj_tile(x_ref.at[rs, :], g_ref, w_ref, cw_ref, *[o.at[rs, :] for o in (o_rwkv, o_qkv, o_z, o_ab, o_gate)],
                     wg_ref, tail_ref)


def _inproj_tile(x_ref, g_ref, w_ref, cw_ref, o_rwkv, o_qkv, o_z, o_ab, o_gate, wg_ref, tail_ref):
    ab_start = o_rwkv.shape[-1] + o_qkv.shape[-1] + o_z.shape[-1]
    x = x_ref[...]
    ms = jnp.mean(x * x, axis=-1, keepdims=True)
    u = (x * lax.rsqrt(ms + NORM_EPS) * g_ref[...]).astype(BF16)
    tm = x_ref.shape[0]
    w_r, w_q, w_z = o_rwkv.shape[-1], o_qkv.shape[-1], o_z.shape[-1]

    def conv_silu(h, lo):
        cols = slice(lo, lo + MXU_TILE)
        hp = jnp.concatenate([tail_ref[:, cols], h], axis=0)
        conv = h * cw_ref[GDN_CONV - 1:GDN_CONV, cols]
        for i in range(GDN_CONV - 1):
            s = SUBLANES - (GDN_CONV - 1) + i
            conv = conv + hp[s:s + tm, :] * cw_ref[i:i + 1, cols]
        tail_ref[:, cols] = h[tm - SUBLANES:tm, :]
        o_qkv[:, cols] = (conv * _sigmoid(conv)).astype(o_qkv.dtype)

    def project(ref, wref, out_lo, w_lo):
        ref[:, out_lo:out_lo + MXU_TILE] = jnp.dot(
            u, wref[:, w_lo:w_lo + MXU_TILE], preferred_element_type=F32).astype(ref.dtype)

    plain = [functools.partial(project, o_rwkv, w_ref, k, k) for k in range(0, w_r, MXU_TILE)]
    plain += [functools.partial(project, o_z, w_ref, k, w_r + w_q + k) for k in range(0, w_z, MXU_TILE)]
    plain += [functools.partial(project, o_gate, wg_ref, k, k) for k in range(0, o_gate.shape[-1], MXU_TILE)]
    qcol = lambda lo: jnp.dot(u, w_ref[:, w_r + lo:w_r + lo + MXU_TILE], preferred_element_type=F32)
    hq = qcol(0)
    for lo in range(0, w_q, MXU_TILE):
        nxt = qcol(lo + MXU_TILE) if lo + MXU_TILE < w_q else None
        for _ in range(2):
            plain.pop(0)()
        conv_silu(hq, lo)
        hq = nxt
    for emit in plain:
        emit()
    o_ab[...] = jnp.dot(u, w_ref[:, ab_start:ab_start + o_ab.shape[-1]], preferred_element_type=F32)


def _inproj(x2, gain, w16, cw, widths, dtypes, tm, tile_rows, steps_per_seq):
    n, d = x2.shape
    return pl.pallas_call(
        functools.partial(_inproj_body, tile_rows=tile_rows, steps_per_seq=steps_per_seq),
        grid=(n // tm,),
        in_specs=[
            pl.BlockSpec((tm, d), lambda i: (i, 0)),
            pl.BlockSpec((1, d), lambda i: (0, 0)),
            pl.BlockSpec(w16.shape, lambda i: (0, 0), pipeline_mode=pl.Buffered(1)),
            pl.BlockSpec(cw.shape, lambda i: (0, 0)),
        ],
        out_specs=[pl.BlockSpec((tm, w), lambda i: (i, 0)) for w in widths],
        out_shape=[jax.ShapeDtypeStruct((n, w), dt) for w, dt in zip(widths, dtypes)],
        scratch_shapes=[pltpu.VMEM((d, widths[-1]), BF16), pltpu.VMEM((SUBLANES, widths[1]), F32)],
        compiler_params=pltpu.CompilerParams(dimension_semantics=("arbitrary",), vmem_limit_bytes=VMEM_LIMIT),
        name="inproj",
    )(x2, gain, w16, cw)


def _rwkv_body(p_ref, *refs):
    *const_refs, y_ref, prev_ref, s_ref = refs

    @pl.when(pl.program_id(1) == 0)
    def _():
        prev_ref[...] = jnp.zeros_like(prev_ref)
        s_ref[...] = jnp.zeros_like(s_ref)

    for r0 in range(0, p_ref.shape[1], CHUNK):
        _rwkv_chunk(p_ref.at[:, r0:r0 + CHUNK, :], *const_refs, y_ref.at[:, r0:r0 + CHUNK, :], prev_ref, s_ref)


def _rwkv_chunk(p_ref, mu_ref, w0_ref, w2_ref, a0_ref, a2_ref, g2_ref, kk_ref, ka_ref, rk_ref, lnw_ref,
                lnb_ref, m16_ref, mf_ref, tril_ref, y_ref, prev_ref, s_ref):
    bb, c, wd = p_ref.shape
    rows = bb * c
    hd = RWKV_HEAD_DIM
    row8 =lax.broadcasted_iota(jnp.int32, (SUBLANES, 1), 0)
    ps, shifts = [], []
    for bi in range(bb):
        pb_ = p_ref[bi].astype(F32)
        rolled = pltpu.roll(pb_, 1, 0)
        shifts += [jnp.where(row8 == 0, prev_ref[bi, 0:1, :], rolled[:SUBLANES]), rolled[SUBLANES:]]
        prev_ref[bi, 0:1, :] = pb_[c - 1:c, :]
        ps.append(pb_)
    p = jnp.concatenate(ps, axis=0)
    pm = p + (jnp.concatenate(shifts, axis=0) - p) * mu_ref[...]

    w = RWKV_WIDTH
    r = pm[:, 0:w]
    k = pm[:, w:2 * w]
    v = pm[:, 2 * w:3 * w]
    o0 = 3 * w
    wl = pm[:, o0:o0 + DECAY_LORA]
    al = pm[:, o0 + DECAY_LORA:o0 + DECAY_LORA + ICLR_LORA]
    gl = pm[:, o0 + DECAY_LORA + ICLR_LORA:o0 + DECAY_LORA + ICLR_LORA + GATE_LORA]

    z = w0_ref[...] + _dot(jnp.tanh(wl), w2_ref[...])
    lw = (-math.exp(-0.5) * _sigmoid(z)).astype(BF16).astype(F32)
    a = _sigmoid(a0_ref[...] + _dot(al, a2_ref[...]))
    g = _dot(_sigmoid(gl), g2_ref[...])

    mask16 = m16_ref[...]
    maskf = mf_ref[...]

    def head_sums(*xs):
        stacked = jnp.concatenate([x[:, q * MXU_TILE:(q + 1) * MXU_TILE] for x in xs for q in range(RWKV_GROUPS)], axis=0)
        res = jnp.dot(stacked.astype(BF16), mask16, preferred_element_type=F32)
        outs = []
        for i in range(len(xs)):
            base = i * RWKV_GROUPS * rows
            outs.append(jnp.concatenate([res[base + q * rows:base + (q + 1) * rows] for q in range(RWKV_GROUPS)], axis=1))
        return outs

    kx = k * kk_ref[...]
    k2 = k * (1.0 + (a - 1.0) * ka_ref[...])
    kx_ss, rk_sum = head_sums(kx * kx, r * k2 * rk_ref[...])
    kk = kx * lax.rsqrt(kx_ss + L2_EPS)

    cum = _chunk_cumsum(tril_ref[...], lw.astype(BF16))
    e_pos = jnp.exp(cum)
    e_neg = 1.0 / e_pos
    e_prev = jnp.exp(cum - lw)
    w_tot = [e_pos[(bi + 1) * c - 1:(bi + 1) * c, :] for bi in range(bb)]
    e_rest = jnp.concatenate([e_neg[bi * c:(bi + 1) * c] * w_tot[bi] for bi in range(bb)], axis=0)

    b_vec = kk * a
    a_t = (-kk * e_prev).astype(BF16)
    r_t = (r * e_pos).astype(BF16)
    b_t = (b_vec * e_neg).astype(BF16)
    k_t = (k2 * e_neg).astype(BF16)
    b_h = (b_vec * e_rest).astype(BF16)
    k_h = (k2 * e_rest).astype(BF16)
    v16 = v.astype(BF16)

    strict_cat, incl_cat, eye_cat = _cat_masks()
    mask2 = jnp.concatenate([strict_cat, incl_cat], axis=0)

    chains = [(bi, q) for bi in range(bb) for q in range(RWKV_GROUPS)]
    rs_of = lambda bi: slice(bi * c, (bi + 1) * c)
    gs_of = lambda q: slice(q * MXU_TILE, (q + 1) * MXU_TILE)
    s_bds = [s_ref[bi, q] for bi, q in chains]
    pxs = []
    for (bi, q), s_bd in zip(chains, s_bds):
        rs, gs = rs_of(bi), gs_of(q)
        l2 = jnp.concatenate([a_t[rs, gs], r_t[rs, gs]], axis=0)
        w3 = jnp.concatenate([_bdiag(b_t[rs, gs], mask16), _bdiag(k_t[rs, gs], mask16), s_bd.astype(BF16)], axis=0)
        pxs.append(_dot_nt(l2, w3))
    pbs = [jnp.where(mask2, px[:, 0:MXU_TILE], 0.0) for px in pxs]
    pks = [jnp.where(mask2, px[:, MXU_TILE:2 * MXU_TILE], 0.0) for px in pxs]
    xas = [px[:, 2 * MXU_TILE:] + _dot(pk, _bdiag(v16[rs_of(bi), gs_of(q)], mask16))
           for (bi, q), px, pk in zip(chains, pxs, pks)]
    tinvs = _unit_lower_inverse_cat([pb[:c] for pb in pbs], eye_cat, mask16)
    us = [_dot(tinv, _bdiag(xa[:c], mask16)) for tinv, xa in zip(tinvs, xas)]
    ys = [xa[c:] + _dot(pb[c:], _bdiag(u, mask16)) for xa, pb, u in zip(xas, pbs, us)]
    for (bi, q), s_bd, u in zip(chains, s_bds, us):
        rs, gs = rs_of(bi), gs_of(q)
        uv = jnp.concatenate([u.astype(BF16), v16[rs, gs]], axis=0)
        bk = jnp.concatenate([b_h[rs, gs], k_h[rs, gs]], axis=0)
        s_ref[bi, q] = s_bd * w_tot[bi][:, gs] + _dot_tn(uv, bk) * maskf
    y_rows = [jnp.concatenate(ys[bi * RWKV_GROUPS:(bi + 1) * RWKV_GROUPS], axis=1) for bi in range(bb)]

    y = jnp.concatenate(y_rows, axis=0)
    inv_hd = 1.0 / hd
    (mean,) = head_sums(y)
    yc = y - mean * inv_hd
    (var,) = head_sums(yc * yc)
    yn = yc * lax.rsqrt(var * inv_hd + RWKV_GN_EPS) * lnw_ref[...] + lnb_ref[...]
    out = (yn + rk_sum * v) * g
    for bi in range(bb):
        y_ref[bi] = out[bi * c:(bi + 1) * c, :].astype(y_ref.dtype)


def _rwkv(p3, params, mask16, maskf, tril16, bb, rows_per_step):
    b, t, wd = p3.shape
    c = rows_per_step
    full = lambda arr: pl.BlockSpec(arr.shape, lambda i, j: (0,) * arr.ndim)
    return pl.pallas_call(
        _rwkv_body,
        grid=(b // bb, t // c),
        in_specs=[pl.BlockSpec((bb, c, wd), lambda i, j: (i, j, 0))] + [full(a) for a in params]
        + [full(mask16), full(maskf), full(tril16)],
        out_specs=pl.BlockSpec((bb, c, RWKV_WIDTH), lambda i, j: (i, j, 0)),
        out_shape=jax.ShapeDtypeStruct((b, t, RWKV_WIDTH), BF16),
        scratch_shapes=[
            pltpu.VMEM((bb, SUBLANES, wd), F32),
            pltpu.VMEM((bb, RWKV_GROUPS, MXU_TILE, MXU_TILE), F32),
        ],
        compiler_params=pltpu.CompilerParams(dimension_semantics=("arbitrary", "arbitrary"),
                                             vmem_limit_bytes=VMEM_LIMIT),
        name="rwkv",
    )(p3, *params, mask16, maskf, tril16)


def _gdn_body(qkv_ref, z_ref, ab_ref, *refs):
    *const_refs, y_ref, s_ref = refs

    @pl.when(pl.program_id(1) == 0)
    def _():
        s_ref[...] = jnp.zeros_like(s_ref)

    for r0 in range(0, qkv_ref.shape[1], CHUNK):
        rs = slice(r0, r0 + CHUNK)
        _gdn_chunk(qkv_ref.at[:, rs, :], z_ref.at[:, rs, :], ab_ref.at[:, rs, :], *const_refs, y_ref.at[:, rs, :], s_ref)


def _gdn_chunk(qkv_ref, z_ref, ab_ref, alog_ref, dtb_ref, nw_ref, m16_ref, hm16_ref, tril_ref, y_ref, s_ref):
    bb, c, _ = qkv_ref.shape
    hd = GDN_HEAD_DIM
    nh = GDN_HEADS
    qkv = jnp.concatenate([qkv_ref[bi] for bi in range(bb)], axis=0).astype(F32)
    w = GDN_WIDTH

    ab = jnp.concatenate([ab_ref[bi] for bi in range(bb)], axis=0)
    g_all = -jnp.exp(alog_ref[...]) * _softplus(ab + dtb_ref[...])
    beta_all = _sigmoid(ab)

    g_hi = g_all.astype(BF16)
    rem = g_all - g_hi.astype(F32)
    g_mid = rem.astype(BF16)
    g_lo = (rem - g_mid.astype(F32)).astype(BF16)
    gcp = _chunk_cumsum(tril_ref[...], jnp.concatenate([g_hi, g_mid, g_lo], axis=1))
    gc_all = gcp[:, 0:LANES] + gcp[:, LANES:2 * LANES] + gcp[:, 2 * LANES:3 * LANES]
    eg_all = jnp.exp(gc_all)

    mask16 = m16_ref[...]
    hmask16 = hm16_ref[...]
    strict_cat, incl_cat, eye_cat = _cat_masks()
    lane_head = lax.broadcasted_iota(jnp.int32, (c, MXU_TILE), 1) // CHUNK

    qs, ks, kbs = [], [], []
    for h in range(nh):
        qh = qkv[:, h * hd:(h + 1) * hd]
        kh = qkv[:, w + h * hd:w + (h + 1) * hd]
        qs.append(qh * lax.rsqrt(jnp.sum(qh * qh, axis=-1, keepdims=True) + L2_EPS) * (hd ** -0.5))
        kh = kh * lax.rsqrt(jnp.sum(kh * kh, axis=-1, keepdims=True) + L2_EPS)
        ks.append(kh)
        kbs.append(kh * beta_all[:, nh + h:nh + h + 1])
    k_all = jnp.concatenate(ks, axis=1).astype(BF16)
    kb_all = jnp.concatenate(kbs, axis=1).astype(BF16)
    q_all = jnp.concatenate(qs, axis=1).astype(BF16)

    rs_of = lambda bi: slice(bi * c, (bi + 1) * c)
    gc_bs = [gc_all[rs_of(bi)] for bi in range(bb)]
    glasts = [gc_b[c - 1:c, :] for gc_b in gc_bs]
    erests = [jnp.exp(glast - gc_b) for glast, gc_b in zip(glasts, gc_bs)]
    elasts = [jnp.exp(glast) for glast in glasts]

    decays = []
    for gc_b in gc_bs:
        f_cat = jnp.zeros((c, MXU_TILE), F32)
        for h in range(nh):
            f_cat = jnp.where(lane_head == h, gc_b[:, h:h + 1], f_cat)
        r_cat = jnp.sum(f_cat * eye_cat, axis=0, keepdims=True)
        decays.append(jnp.where(incl_cat, jnp.exp(jnp.where(incl_cat, f_cat - r_cat, 0.0)), 0.0))

    kqs = [_dot_nt(jnp.concatenate([kb_all[rs_of(bi)], q_all[rs_of(bi)]], axis=0),
                   jnp.concatenate([k_all[rs_of(bi)]] * nh, axis=0) * hmask16) for bi in range(bb)]
    attns = [kq[c:] * decay for kq, decay in zip(kqs, decays)]
    tinvs = _unit_lower_inverse_cat([-jnp.where(strict_cat, kq[:c] * decay, 0.0) for kq, decay in zip(kqs, decays)],
                                    eye_cat, mask16)

    chains = [(bi, h) for bi in range(bb) for h in range(nh)]
    tl_of = lambda h: slice(h * c, (h + 1) * c)
    egs = [eg_all[rs_of(bi), h:h + 1] for bi, h in chains]
    uws = []
    for (bi, h), eg in zip(chains, egs):
        rs = rs_of(bi)
        vh = qkv[rs, 2 * w + h * hd:2 * w + (h + 1) * hd]
        beta = beta_all[rs, nh + h:nh + h + 1]
        uws.append(_dot(tinvs[bi][:, tl_of(h)], jnp.concatenate([vh * beta, kbs[h][rs] * eg], axis=1)))
    ss = [s_ref[bi, h] for bi, h in chains]
    wqs = [_dot(jnp.concatenate([uw[:, hd:], qs[h][rs_of(bi)] * eg], axis=0), s)
           for (bi, h), uw, eg, s in zip(chains, uws, egs, ss)]
    v_news = [uw[:, :hd] - wq[:c] for uw, wq in zip(uws, wqs)]
    os_ = [wq[c:] + _dot(attns[bi][:, tl_of(h)], v_new) for (bi, h), wq, v_new in zip(chains, wqs, v_news)]
    for (bi, h), s, v_new in zip(chains, ss, v_news):
        s_ref[bi, h] = s * elasts[bi][:, h:h + 1] + _dot_tn(ks[h][rs_of(bi)] * erests[bi][:, h:h + 1], v_new)
    for (bi, h), o in zip(chains, os_):
        sl = slice(h * hd, (h + 1) * hd)
        on = o * lax.rsqrt(jnp.mean(o * o, axis=-1, keepdims=True) + NORM_EPS) * nw_ref[...]
        zh = z_ref[bi, :, sl].astype(F32)
        y_ref[bi, :, sl] = (on * (zh * _sigmoid(zh))).astype(y_ref.dtype)


def _gdn(qkv3, z3, ab3, alog, dtb, nw, mask16, hmask16, tril16, bb, rows_per_step):
    b, t, wd = qkv3.shape
    c = rows_per_step
    full = lambda arr: pl.BlockSpec(arr.shape, lambda i, j: (0,) * arr.ndim)
    return pl.pallas_call(
        _gdn_body,
        grid=(b // bb, t // c),
        in_specs=[
            pl.BlockSpec((bb, c, wd), lambda i, j: (i, j, 0)),
            pl.BlockSpec((bb, c, GDN_WIDTH), lambda i, j: (i, j, 0)),
            pl.BlockSpec((bb, c, AB_PAD), lambda i, j: (i, j, 0)),
            full(alog), full(dtb), full(nw), full(mask16), full(hmask16), full(tril16),
        ],
        out_specs=pl.BlockSpec((bb, c, GDN_WIDTH), lambda i, j: (i, j, 0)),
        out_shape=jax.ShapeDtypeStruct((b, t, GDN_WIDTH), BF16),
        scratch_shapes=[pltpu.VMEM((bb, GDN_HEADS, GDN_HEAD_DIM, GDN_HEAD_DIM), F32)],
        compiler_params=pltpu.CompilerParams(dimension_semantics=("arbitrary", "arbitrary"),
                                             vmem_limit_bytes=VMEM_LIMIT),
        name="gdn",
    )(qkv3, z3, ab3, alog, dtb, nw, mask16, hmask16, tril16)


def _mixffn_body(x_ref, ya_ref, yb_ref, gate_ref, *refs, tile_rows, col_block, apply_final):
    *const_refs, o_ref, prev_ref = refs

    @pl.when(pl.program_id(1) == 0)
    def _():
        prev_ref[...] = jnp.zeros_like(prev_ref)

    for r0 in range(0, x_ref.shape[0], tile_rows):
        rs = slice(r0, r0 + tile_rows)
        _mixffn_tile(x_ref.at[rs, :], ya_ref.at[rs, :], yb_ref.at[rs, :], gate_ref.at[rs, :], *const_refs,
                     o_ref.at[rs, :], prev_ref, col_block=col_block, apply_final=apply_final)


def _mixffn_tile(x_ref, ya_ref, yb_ref, gate_ref, rp_ref, gp_ref, wo_ref, g_ref, up_ref, cw_ref, down_ref, fg_ref,
                 o_ref, prev_ref, *, col_block, apply_final):
    tm, d = x_ref.shape
    hidden = down_ref.shape[0]

    ya = jnp.dot(ya_ref[...], rp_ref[...], preferred_element_type=F32)
    yb = jnp.dot(yb_ref[...], gp_ref[...], preferred_element_type=F32)
    mixed = _sigmoid(gate_ref[:, :d].astype(F32)) * ya + _sigmoid(gate_ref[:, d:].astype(F32)) * yb
    x = x_ref[...] + jnp.dot(mixed.astype(BF16), wo_ref[...], preferred_element_type=F32)

    ms = jnp.mean(x * x, axis=-1, keepdims=True)
    xn = (x * lax.rsqrt(ms + NORM_EPS) * g_ref[...]).astype(BF16)

    def up_cols(lo):
        return jnp.dot(xn, up_ref[:, lo:lo + col_block], preferred_element_type=F32)

    def conv_cols(h, lo):
        hp = jnp.concatenate([prev_ref[:, lo:lo + col_block], h], axis=0)
        out = h * cw_ref[FFN_CONV - 1:FFN_CONV, lo:lo + col_block]
        for i in range(FFN_CONV - 1):
            s = SUBLANES - (FFN_CONV - 1) + i
            out = out + hp[s:s + tm, :] * cw_ref[i:i + 1, lo:lo + col_block]
        prev_ref[:, lo:lo + col_block] = h[tm - SUBLANES:tm, :]
        return out

    nblk = hidden // col_block
    acts = []
    hs = (up_cols(0), up_cols(hidden))
    for j in range(nblk):
        lo = j * col_block
        nxt = (up_cols(lo + col_block), up_cols(hidden + lo + col_block)) if j + 1 < nblk else None
        hg = conv_cols(hs[0], lo)
        hu = conv_cols(hs[1], hidden + lo)
        acts.append((hg * _sigmoid(hg) * hu).astype(BF16))
        hs = nxt
    acc = x + jnp.dot(jnp.concatenate(acts, axis=1), down_ref[...], preferred_element_type=F32)
    if apply_final:
        ms2 = jnp.mean(acc * acc, axis=-1, keepdims=True)
        acc = acc * lax.rsqrt(ms2 + NORM_EPS) * fg_ref[...]
    o_ref[...] = acc


def _mixffn(x3, ya3, yb3, gate3, rp, gp, wo, gain, up, cw, down, final_g, tm, tiles_per_step, apply_final):
    b, t, d = x3.shape
    hidden = down.shape[0]
    full = lambda arr: pl.BlockSpec(arr.shape, lambda i, j: (0, 0))
    single = lambda arr: pl.BlockSpec(arr.shape, lambda i, j: (0, 0), pipeline_mode=pl.Buffered(1))
    rows = tm * tiles_per_step
    tile = lambda arr: pl.BlockSpec((None, rows, arr.shape[-1]), lambda i, j: (i, j, 0))
    body = functools.partial(_mixffn_body, tile_rows=tm, col_block=MXU_TILE, apply_final=apply_final)
    return pl.pallas_call(
        body,
        grid=(b, t // rows),
        in_specs=[tile(x3), tile(ya3), tile(yb3), tile(gate3), single(rp), single(gp), single(wo),
                  full(gain), single(up), full(cw), single(down), full(final_g)],
        out_specs=pl.BlockSpec((None, rows, d), lambda i, j: (i, j, 0)),
        out_shape=jax.ShapeDtypeStruct((b, t, d), F32),
        scratch_shapes=[pltpu.VMEM((SUBLANES, 2 * hidden), F32)],
        compiler_params=pltpu.CompilerParams(dimension_semantics=("arbitrary", "arbitrary"),
                                             vmem_limit_bytes=VMEM_LIMIT),
        name="mixffn",
    )(x3, ya3, yb3, gate3, rp, gp, wo, gain, up, cw, down, final_g)


def _row(vec):
    return vec.reshape(1, -1).astype(F32)


def _pad_lanes(vec, offset=0):
    out = jnp.zeros((1, LANES), F32)
    return lax.dynamic_update_slice(out, vec.reshape(1, -1).astype(F32), (0, offset))


def kernel(x, norm1_g, w_in, rwkv_mu, rwkv_w0, rwkv_w2, rwkv_a0, rwkv_a2, rwkv_g2, rwkv_k_k, rwkv_k_a,
           rwkv_r_k, rwkv_ln_w, rwkv_ln_b, rwkv_proj, gdn_conv_w, gdn_a_log, gdn_dt_bias, gdn_norm_w,
           gdn_proj, w_out, norm2_g, ffn_up, ffn_conv_w, ffn_down, final_g):
    b, t, d = x.shape
    n = b * t
    depth = norm1_g.shape[0]
    tm = 256
    tm_in = 1024
    bb = next(c for c in (8, 4, 2, 1) if b % c == 0)
    scan_rows = 2 * CHUNK if t % (2 * CHUNK) == 0 else CHUNK
    ffn_tiles = 2 if t % (2 * tm) == 0 else 1
    assert t % tm_in == 0 and t % tm == 0 and t % CHUNK == 0, "sequence length must be a multiple of the tiles"
    assert (bb * CHUNK) % min(bb * CHUNK, MXU_TILE) == 0

    blk = jnp.arange(MXU_TILE) // CHUNK
    maskf = (blk[:, None] == blk[None, :]).astype(F32)
    mask16 = maskf.astype(BF16)
    rr = jnp.arange(min(bb * CHUNK, MXU_TILE))
    tril16 =((rr[:, None] >= rr[None, :]) & (rr[:, None] // CHUNK == rr[None, :] // CHUNK)).astype(BF16)
    hmask16 = (jnp.arange(GDN_HEADS * CHUNK)[:, None] // CHUNK == jnp.arange(GDN_WIDTH)[None, :] // GDN_HEAD_DIM).astype(BF16)

    widths = (RWKV_SHIFT_WIDTH, 3 * GDN_WIDTH, GDN_WIDTH, AB_PAD, 2 * d)
    dtypes = (BF16, BF16, BF16, F32, BF16)

    for l in range(depth):
        p_rwkv, qkv, z, ab, gate = _inproj(x.reshape(n, d), _row(norm1_g[l]), w_in[l].astype(BF16),
                                           gdn_conv_w[l].astype(F32), widths, dtypes, tm_in, tm, t // tm_in)

        rwkv_params = (_row(rwkv_mu[l]), _row(rwkv_w0[l]), rwkv_w2[l].astype(BF16), _row(rwkv_a0[l]),
                       rwkv_a2[l].astype(BF16), rwkv_g2[l].astype(BF16), _row(rwkv_k_k[l]), _row(rwkv_k_a[l]),
                       _row(rwkv_r_k[l]), _row(rwkv_ln_w[l]), _row(rwkv_ln_b[l]))
        y_a = _rwkv(p_rwkv.reshape(b, t, -1), rwkv_params, mask16, maskf, tril16, bb, scan_rows)

        y_b = _gdn(qkv.reshape(b, t, -1), z.reshape(b, t, -1), ab.reshape(b, t, -1),
                   _pad_lanes(gdn_a_log[l]), _pad_lanes(gdn_dt_bias[l]), _row(gdn_norm_w[l]), mask16, hmask16, tril16, bb,
                   scan_rows)

        x = _mixffn(x, y_a, y_b, gate.reshape(b, t, -1), rwkv_proj[l].astype(BF16), gdn_proj[l].astype(BF16),
                    w_out[l].astype(BF16), _row(norm2_g[l]), ffn_up[l].astype(BF16), ffn_conv_w[l].astype(F32),
                    ffn_down[l].astype(BF16), _row(final_g), tm, ffn_tiles, apply_final=(l == depth - 1))
    return x
```
